```python
import jax, jax.numpy as jnp
from jax import lax
import numpy as np

D_MODEL = 1024
BATCH = 8
SEQ = 4096
DEPTH = 1
DEC_BATCH = 32
DEC_SEQ = 8
PAST_LEN = 16384
PAGE_SIZE = 128

F32 = jnp.float32
EPS = 1e-6
N_META = 16
FOX_HEADS = 8
FOX_HEAD_DIM = D_MODEL // 16
FOX_WIDTH = FOX_HEADS * FOX_HEAD_DIM
FOX_FORGET_BIAS_MIN = 4.0
FOX_FORGET_BIAS_MAX = 10.0
Q_BLOCK = 128
GLA_HEADS = 4
GLA_KEY_DIM = D_MODEL // 16
GLA_VAL_DIM = D_MODEL // 8
GLA_K = GLA_HEADS * GLA_KEY_DIM
GLA_V = GLA_HEADS * GLA_VAL_DIM
GLA_RANK = 16
GLA_TAU = 16.0
GLA_CHUNK = 128
N_GROUPS = 4
EXPERTS_PER_GROUP = 8
N_EXPERTS = N_GROUPS * EXPERTS_PER_GROUP
TOP_K_IN_GROUP = 2
D_EXPERT = D_MODEL // 4
PROJ_SIZES = (FOX_WIDTH, FOX_WIDTH, FOX_WIDTH, FOX_HEADS,
              GLA_K, GLA_K, GLA_V, GLA_RANK, GLA_V,
              D_MODEL, D_MODEL)
PROJ_WIDTH = sum(PROJ_SIZES)
SPLIT_POINTS = tuple(sum(PROJ_SIZES[:i + 1]) for i in range(len(PROJ_SIZES) - 1))

kernel_name = "fox_gla_gated_hier_moe_step"


def rmsnorm(x, g):
    xf = x.astype(F32)
    y = xf * lax.rsqrt(jnp.mean(xf * xf, axis=-1, keepdims=True) + EPS)
    return (y * g.astype(F32)).astype(x.dtype)


def project(h, w_in, fox_f_bias, gla_w_a2, gla_b_a):
    n, l, _ = h.shape
    z = jnp.einsum('nld,dp->nlp', h, w_in)
    qf, kf, vf, ff, qg, kg, vg, lrg, rg, gate_f, gate_g = jnp.split(z, SPLIT_POINTS, axis=-1)
    log_f = jax.nn.log_sigmoid(ff.astype(F32) + fox_f_bias.astype(F32))
    log_a = jax.nn.log_sigmoid((lrg @ gla_w_a2 + gla_b_a).astype(F32)) / GLA_TAU
    fh = lambda t: t.reshape(n, l, FOX_HEADS, FOX_HEAD_DIM)
    gh = lambda t: t.reshape(n, l, GLA_HEADS, -1)
    return (fh(qf), fh(kf), fh(vf), log_f, gh(qg), gh(kg), gh(vg), gh(log_a), rg, gate_f, gate_g)


def fox_attend_prompt(q, k, v, log_f):
    n, l = q.shape[:2]
    scale = FOX_HEAD_DIM ** -0.5
    c = jnp.cumsum(log_f, axis=1)
    c_keys = c.transpose(0, 2, 1)[:, :, None, :]
    key_pos = jnp.arange(l)

    def attend(qb, cb, qpos):
        s = jnp.einsum('nthd,nshd->nhts', qb, k, preferred_element_type=F32) * scale
        s = s + cb.transpose(0, 2, 1)[..., None] - c_keys
        s = jnp.where((key_pos[None, :] <= qpos[:, None])[None, None], s, -jnp.inf)
        p = jax.nn.softmax(s, axis=-1)
        return jnp.einsum('nhts,nshd->nthd', p.astype(v.dtype), v)

    o_meta = attend(q[:, :N_META], c[:, :N_META], jnp.arange(N_META))
    n_blocks = (l - N_META) // Q_BLOCK

    def block(i):
        start = N_META + i * Q_BLOCK
        qb = lax.dynamic_slice_in_dim(q, start, Q_BLOCK, axis=1)
        cb = lax.dynamic_slice_in_dim(c, start, Q_BLOCK, axis=1)
        return attend(qb, cb, start + jnp.arange(Q_BLOCK))

    o_blocks = lax.map(block, jnp.arange(n_blocks))
    o_real = o_blocks.transpose(1, 0, 2, 3, 4).reshape(n, l - N_META, FOX_HEADS, FOX_HEAD_DIM)
    return jnp.concatenate([o_meta, o_real], axis=1)


def fox_attend_sample(q, k, v, log_f, k_past, v_past, log_f_past):
    scale = FOX_HEAD_DIM ** -0.5
    t = q.shape[1]
    p_len = k_past.shape[1]
    lfp = log_f_past.astype(F32)
    r = lax.cumsum(lfp, axis=1, reverse=True) - lfp
    a = jnp.cumsum(log_f, axis=1)
    a_q = a.transpose(0, 2, 1)[..., None]
    s_past = (jnp.einsum('nthd,nshd->nhts', q, k_past, preferred_element_type=F32) * scale
              + a_q + r.transpose(0, 2, 1)[:, :, None, :])
    s_new = (jnp.einsum('nthd,nshd->nhts', q, k, preferred_element_type=F32) * scale
             + a_q - a.transpose(0, 2, 1)[:, :, None, :])
    s_new = jnp.where(jnp.tril(jnp.ones((t, t), bool)), s_new, -jnp.inf)
    p = jax.nn.softmax(jnp.concatenate([s_past, s_new], axis=-1), axis=-1).astype(v.dtype)
    return (jnp.einsum('nhts,nshd->nthd', p[..., :p_len], v_past)
            + jnp.einsum('nhts,nshd->nthd', p[..., p_len:], v))


def gla_chunk(q, k, v, log_a, s0):
    c = q.shape[1]
    qf = q.astype(F32) * GLA_KEY_DIM ** -0.5
    kf = k.astype(F32)
    vf = v.astype(F32)
    b = jnp.cumsum(log_a, axis=1)
    o_inter = jnp.einsum('nchk,nhkv->nchv', qf * jnp.exp(b), s0)
    causal = jnp.tril(jnp.ones((c, c), bool))[None, :, :, None, None]
    decay = jnp.exp(jnp.where(causal, b[:, :, None] - b[:, None, :], -jnp.inf))
    scores = jnp.einsum('ntshk,nshk->ntsh', qf[:, :, None] * decay, kf)
    o_intra = jnp.einsum('ntsh,nshv->nthv', scores, vf)
    b_end = b[:, -1]
    k_dec = kf * jnp.exp(b_end[:, None] - b)
    s_new = jnp.exp(b_end)[..., None] * s0 + jnp.einsum('nchk,nchv->nhkv', k_dec, vf)
    return o_inter + o_intra, s_new


def gla_prompt(q, k, v, log_a):
    n, l = q.shape[:2]
    s0 = jnp.zeros((n, GLA_HEADS, GLA_KEY_DIM, GLA_VAL_DIM), F32)
    o_meta, s_meta = gla_chunk(q[:, :N_META], k[:, :N_META], v[:, :N_META], log_a[:, :N_META], s0)
    n_chunks = (l - N_META) // GLA_CHUNK

    def chunks(t):
        t = t[:, N_META:]
        return t.reshape((n, n_chunks, GLA_CHUNK) + t.shape[2:]).swapaxes(0, 1)

    def step(s, inp):
        o, s_next = gla_chunk(*inp, s)
        return s_next, o

    s_final, o_chunks = lax.scan(step, s_meta, (chunks(q), chunks(k), chunks(v), chunks(log_a)))
    o_real = o_chunks.swapaxes(0, 1).reshape(n, l - N_META, GLA_HEADS, GLA_VAL_DIM)
    return jnp.concatenate([o_meta, o_real], axis=1), s_final


def merge_branches(o_fox, o_gla, r, gate_f, gate_g, gla_norm_g, w_fox_out, w_gla_out, w_o):
    n, l = o_fox.shape[:2]
    of = o_fox.reshape(n, l, FOX_WIDTH)
    ogn = o_gla * lax.rsqrt(jnp.mean(o_gla * o_gla, axis=-1, keepdims=True) + EPS)
    og = (ogn.reshape(n, l, GLA_V) * gla_norm_g.astype(F32) * jax.nn.silu(r.astype(F32))).astype(r.dtype)
    branch_f = of @ w_fox_out
    branch_g = og @ w_gla_out
    merged = jax.nn.sigmoid(gate_f) * branch_f + jax.nn.sigmoid(gate_g) * branch_g
    return merged @ w_o


def hier_moe(h, w_group, b_group, w_er, b_er, w_g, w_u, w_d):
    n, l, d = h.shape
    hf = h.reshape(n * l, d)
    g_logits = (hf @ w_group + b_group).astype(F32)
    g_prob = jax.nn.softmax(g_logits, axis=-1)
    g_idx = jnp.argmax(g_logits, axis=-1)
    g_sel = jnp.max(g_prob, axis=-1)
    e_logits = (hf @ w_er + b_er).astype(F32).reshape(n * l, N_GROUPS, EXPERTS_PER_GROUP)
    e_in_group = jnp.sum(e_logits * jax.nn.one_hot(g_idx, N_GROUPS, dtype=F32)[..., None], axis=1)
    vals, idx = lax.top_k(e_in_group, TOP_K_IN_GROUP)
    w_sel = jax.nn.softmax(vals, axis=-1) * g_sel[:, None]
    e_id = g_idx[:, None] * EXPERTS_PER_GROUP + idx
    gates = jnp.sum(jax.nn.one_hot(e_id, N_EXPERTS, dtype=F32) * w_sel[..., None], axis=1).astype(h.dtype)
    y = jnp.zeros_like(hf)
    for e in range(N_EXPERTS):
        he = jax.nn.silu(hf @ w_g[e]) * (hf @ w_u[e])
        y = y + gates[:, e:e + 1] * (he @ w_d[e])
    return y.reshape(n, l, d)


def setup_inputs(seed: int = 0) -> dict:
    key = jax.random.key(seed)
    ks = jax.random.split(key, 32)
    n_pages = PAST_LEN // PAGE_SIZE
    n_used = DEC_BATCH * n_pages
    n_phys = (5 * n_used + 3) // 4
    nrm = lambda k, shape, scale=1.0: scale * jax.random.normal(k, shape, F32)
    head_bias = jnp.linspace(FOX_FORGET_BIAS_MIN, FOX_FORGET_BIAS_MAX, FOX_HEADS, dtype=F32)
    return {
        "x_prompt": nrm(ks[0], (BATCH, SEQ, D_MODEL)),
        "x_sample": nrm(ks[1], (DEC_BATCH, DEC_SEQ, D_MODEL)),
        "cache_k": nrm(ks[2], (DEPTH, n_phys, PAGE_SIZE, FOX_HEADS, FOX_HEAD_DIM)),
        "cache_v": nrm(ks[3], (DEPTH, n_phys, PAGE_SIZE, FOX_HEADS, FOX_HEAD_DIM)),
        "cache_log_f": jax.nn.log_sigmoid(head_bias + nrm(ks[4], (DEPTH, n_phys, PAGE_SIZE, FOX_HEADS))),
        "state_gla": nrm(ks[5], (DEPTH, DEC_BATCH, GLA_HEADS, GLA_KEY_DIM, GLA_VAL_DIM), 0.5),
        "page_table": jax.random.permutation(ks[6], n_phys)[:n_used].reshape(DEC_BATCH, n_pages).astype(jnp.int32),
        "meta_tokens": nrm(ks[7], (N_META, D_MODEL)),
        "norm_mix_g": 1.0 + nrm(ks[8], (DEPTH, D_MODEL), 0.02),
        "w_in": nrm(ks[9], (DEPTH, D_MODEL, PROJ_WIDTH), D_MODEL ** -0.5),
        "fox_f_bias": head_bias + nrm(ks[10], (DEPTH, FOX_HEADS), 0.1),
        "gla_w_a2": nrm(ks[11], (DEPTH, GLA_RANK, GLA_K), GLA_RANK ** -0.5),
        "gla_b_a": nrm(ks[12], (DEPTH, GLA_K), 0.1),
        "gla_norm_g": 1.0 + nrm(ks[13], (DEPTH, GLA_V), 0.02),
        "w_fox_out": nrm(ks[14], (DEPTH, FOX_WIDTH, D_MODEL), FOX_WIDTH ** -0.5),
        "w_gla_out": nrm(ks[15], (DEPTH, GLA_V, D_MODEL), GLA_V ** -0.5),
        "w_o": nrm(ks[16], (DEPTH, D_MODEL, D_MODEL), D_MODEL ** -0.5),
        "norm_ffn_g": 1.0 + nrm(ks[17], (DEPTH, D_MODEL), 0.02),
        "w_group_router": nrm(ks[18], (DEPTH, D_MODEL, N_GROUPS), D_MODEL ** -0.5),
        "b_group_router": nrm(ks[19], (DEPTH, N_GROUPS), 0.01),
        "w_expert_router": nrm(ks[20], (DEPTH, D_MODEL, N_EXPERTS), D_MODEL ** -0.5),
        "b_expert_router": nrm(ks[21], (DEPTH, N_EXPERTS), 0.01),
        "w_expert_gate": nrm(ks[22], (DEPTH, N_EXPERTS, D_MODEL, D_EXPERT), D_MODEL ** -0.5),
        "w_expert_up": nrm(ks[23], (DEPTH, N_EXPERTS, D_MODEL, D_EXPERT), D_MODEL ** -0.5),
        "w_expert_down": nrm(ks[24], (DEPTH, N_EXPERTS, D_EXPERT, D_MODEL), D_EXPERT ** -0.5),
        "norm_final_g": 1.0 + nrm(ks[25], (D_MODEL,), 0.02),
    }


def reference(x_prompt, x_sample, cache_k, cache_v, cache_log_f, state_gla, page_table,
              meta_tokens, norm_mix_g, w_in, fox_f_bias, gla_w_a2, gla_b_a, gla_norm_g,
              w_fox_out, w_gla_out, w_o, norm_ffn_g, w_group_router, b_group_router,
              w_expert_router, b_expert_router, w_expert_gate, w_expert_up, w_expert_down,
              norm_final_g):
    b = x_prompt.shape[0]
    db = x_sample.shape[0]
    past = page_table.shape[1] * PAGE_SIZE
    meta = jnp.broadcast_to(meta_tokens.astype(x_prompt.dtype)[None], (b, N_META, D_MODEL))
    xp = jnp.concatenate([meta, x_prompt], axis=1)
    xs = x_sample
    kp, vp, lfp, sgp, ksm, vsm, lfs, sgs = [], [], [], [], [], [], [], []
    for layer in range(DEPTH):
        proj_w = (w_in[layer], fox_f_bias[layer], gla_w_a2[layer], gla_b_a[layer])
        merge_w = (gla_norm_g[layer], w_fox_out[layer], w_gla_out[layer], w_o[layer])
        moe_w = (w_group_router[layer], b_group_router[layer], w_expert_router[layer],
                 b_expert_router[layer], w_expert_gate[layer], w_expert_up[layer], w_expert_down[layer])
        qf, kf, vf, lf, qg, kg, vg, la, r, gf, gg = project(rmsnorm(xp, norm_mix_g[layer]), *proj_w)
        o_fox = fox_attend_prompt(qf, kf, vf, lf)
        o_gla, s_gla = gla_prompt(qg, kg, vg, la)
        xp = xp + merge_branches(o_fox, o_gla, r, gf, gg, *merge_w)
        xp = xp + hier_moe(rmsnorm(xp, norm_ffn_g[layer]), *moe_w)
        kp.append(kf); vp.append(vf); lfp.append(lf); sgp.append(s_gla)
        qf, kf, vf, lf, qg, kg, vg, la, r, gf, gg = project(rmsnorm(xs, norm_mix_g[layer]), *proj_w)
        k_past = cache_k[layer][page_table].reshape(db, past, FOX_HEADS, FOX_HEAD_DIM)
        v_past = cache_v[layer][page_table].reshape(db, past, FOX_HEADS, FOX_HEAD_DIM)
        lf_past = cache_log_f[layer][page_table].reshape(db, past, FOX_HEADS)
        o_fox = fox_attend_sample(qf, kf, vf, lf, k_past, v_past, lf_past)
        o_gla, s_gla = gla_chunk(qg, kg, vg, la, state_gla[layer].astype(F32))
        xs = xs + merge_branches(o_fox, o_gla, r, gf, gg, *merge_w)
        xs = xs + hier_moe(rmsnorm(xs, norm_ffn_g[layer]), *moe_w)
        ksm.append(kf); vsm.append(vf); lfs.append(lf); sgs.append(s_gla)
    y_prompt = rmsnorm(xp, norm_final_g)[:, N_META:]
    y_sample = rmsnorm(xs, norm_final_g)
    new_k_prompt = jnp.stack(kp)
    new_v_prompt = jnp.stack(vp)
    new_log_f_prompt = jnp.stack(lfp)
    new_gla_prompt = jnp.stack(sgp)
    new_k_sample = jnp.stack(ksm)
    new_v_sample = jnp.stack(vsm)
    new_log_f_sample = jnp.stack(lfs)
    new_gla_sample = jnp.stack(sgs)
    return (y_prompt, y_sample, new_k_prompt, new_v_prompt, new_log_f_prompt, new_gla_prompt,
            new_k_sample, new_v_sample, new_log_f_sample, new_gla_sample)
```

```python
import functools

import numpy as np
import jax
import jax.numpy as jnp
from jax import lax
from jax.experimental import pallas as pl
from jax.experimental.pallas import tpu as pltpu

F32 = jnp.float32
BF16 = jnp.bfloat16
I32 = jnp.int32
EPS = 1e-6
N_META = 16
FOX_HEADS = 8
FOX_HEAD_DIM = 64
FOX_WIDTH = FOX_HEADS * FOX_HEAD_DIM
GLA_HEADS = 4
GLA_KEY_DIM = 64
GLA_VAL_DIM = 128
GLA_K = GLA_HEADS * GLA_KEY_DIM
GLA_V = GLA_HEADS * GLA_VAL_DIM
GLA_RANK = 16
GLA_TAU = 16.0
GLA_CHUNK = 128
N_GROUPS = 4
EXPERTS_PER_GROUP = 8
N_EXPERTS = N_GROUPS * EXPERTS_PER_GROUP
PAGE_SIZE = 128

LANES = 128
SUBLANES = 8
VMEM_LIMIT = 56 * 1024 * 1024

PROJ_TILE = 512
ATTN_TILE = 512
MERGE_TILE = 256
MOE_TOKENS = 512
MOE_ROWS = 256
RUN_ALIGN = SUBLANES
PAGES_PER_STEP = 16
W_COLS = LANES


def _cparams(sem):
    return pltpu.CompilerParams(dimension_semantics=sem, vmem_limit_bytes=VMEM_LIMIT)


def _dot(a, b):
    return jnp.dot(a, b, preferred_element_type=F32)


def _dot_nt(a, b):
    return lax.dot_general(a, b, (((1,), (1,)), ((), ())), preferred_element_type=F32)


def _dot_tn(a, b):
    return lax.dot_general(a, b, (((0,), (0,)), ((), ())), preferred_element_type=F32)


def _split3(a):
    hi = a.astype(BF16)
    r1 = a - hi.astype(F32)
    mid = r1.astype(BF16)
    lo = (r1 - mid.astype(F32)).astype(BF16)
    return hi, mid, lo


def _dot_sel_l(m, a):
    hi, mid, lo = _split3(a)
    return _dot(m, lo) + _dot(m, mid) + _dot(m, hi)


def _dot_sel_r(a, m):
    hi, mid, lo = _split3(a)
    return _dot(lo, m) + _dot(mid, m) + _dot(hi, m)


def _rmsnorm(x, g):
    return x * lax.rsqrt(jnp.mean(x * x, axis=-1, keepdims=True) + EPS) * g


def _log_sigmoid(x):
    return jnp.minimum(x, 0.0) - jnp.log1p(jnp.exp(-jnp.abs(x)))


def _sigmoid(x):
    return 1.0 / (1.0 + jnp.exp(-x))


def _col_from_row(row, n):
    r = lax.broadcasted_iota(I32, (n, n), 0)
    c = lax.broadcasted_iota(I32, (n, n), 1)
    return jnp.sum(jnp.where(r == c, jnp.broadcast_to(row, (n, n)), 0.0), axis=1, keepdims=True)


def _proj_kernel(x_ref, g_ref, wqkv_ref, wfft_ref, wgla_ref, wlr_ref, wa2_ref, ba_ref, fb_ref,
                 q_ref, k_ref, v_ref, kb_ref, vb_ref, lft_ref, qg_ref, kg_ref, vg_ref, la_ref):
    h = _rmsnorm(x_ref[0], g_ref[...]).astype(BF16)
    z = _dot(h, wqkv_ref[...])
    q_ref[0] = (z[:, :FOX_WIDTH] * FOX_HEAD_DIM ** -0.5).astype(BF16)
    k = z[:, FOX_WIDTH:2 * FOX_WIDTH]
    v = z[:, 2 * FOX_WIDTH:]
    k_ref[0] = k
    v_ref[0] = v
    kb_ref[0] = k.astype(BF16)
    vb_ref[0] = v.astype(BF16)
    fft = _dot_nt(wfft_ref[...], h)[:FOX_HEADS]
    lft_ref[0] = _log_sigmoid(fft + fb_ref[...])
    zg = _dot(h, wgla_ref[...])
    qg_ref[0] = zg[:, :GLA_K] * GLA_KEY_DIM ** -0.5
    kg_ref[0] = zg[:, GLA_K:2 * GLA_K]
    vg_ref[0] = zg[:, 2 * GLA_K:]
    lr = _dot(h, wlr_ref[...])
    xa = _dot(lr.astype(BF16), wa2_ref[...]) + ba_ref[...]
    la_ref[0] = _log_sigmoid(xa) / GLA_TAU


def _project(x3, tm, pw):
    bx, lx, d = x3.shape
    assert lx % tm == 0
    grid = (bx, lx // tm)
    row = lambda w: pl.BlockSpec((1, tm, w), lambda b, i: (b, i, 0))
    full = lambda a: pl.BlockSpec(a.shape, lambda b, i: (0,) * a.ndim)
    ws = (pw["g_mix"], pw["w_qkv"], pw["w_fft"], pw["w_gla"], pw["w_lr"], pw["w_a2"], pw["b_a"], pw["f_bias"])
    out_shape = (
        jax.ShapeDtypeStruct((bx, lx, FOX_WIDTH), BF16),
        jax.ShapeDtypeStruct((bx, lx, FOX_WIDTH), F32),
        jax.ShapeDtypeStruct((bx, lx, FOX_WIDTH), F32),
        jax.ShapeDtypeStruct((bx, lx, FOX_WIDTH), BF16),
        jax.ShapeDtypeStruct((bx, lx, FOX_WIDTH), BF16),
        jax.ShapeDtypeStruct((bx, FOX_HEADS, lx), F32),
        jax.ShapeDtypeStruct((bx, lx, GLA_K), F32),
        jax.ShapeDtypeStruct((bx, lx, GLA_K), F32),
        jax.ShapeDtypeStruct((bx, lx, GLA_V), F32),
        jax.ShapeDtypeStruct((bx, lx, GLA_K), F32),
    )
    out_specs = (row(FOX_WIDTH), row(FOX_WIDTH), row(FOX_WIDTH), row(FOX_WIDTH), row(FOX_WIDTH),
                 pl.BlockSpec((1, FOX_HEADS, tm), lambda b, i: (b, 0, i)),
                 row(GLA_K), row(GLA_K), row(GLA_V), row(GLA_K))
    return pl.pallas_call(
        _proj_kernel, grid=grid,
        in_specs=[row(d)] + [full(a) for a in ws],
        out_specs=out_specs, out_shape=out_shape,
        compiler_params=_cparams(("parallel", "parallel")), name="proj",
    )(x3, *ws)


def _online_update(s, v, m, l, acc):
    m_new = jnp.maximum(m, jnp.max(s, axis=1, keepdims=True))
    alpha = jnp.exp(m - m_new)
    p = jnp.exp(s - m_new)
    l = alpha * l + jnp.sum(p, axis=1, keepdims=True)
    acc = alpha * acc + _dot(p.astype(BF16), v)
    return m_new, l, acc


def _fox_prompt_kernel(q_ref, kb_ref, vb_ref, lft_ref, km_ref, vm_ref, lfm_ref, tri_ref, um_ref,
                       o_ref, c_ref, *, tq):
    qi = pl.program_id(1)
    n_blk = kb_ref.shape[1] // tq

    @pl.when(qi == 0)
    def _():
        carry = jnp.zeros((FOX_HEADS, 1), F32)
        for jb in range(n_blk):
            for js in range(tq // LANES):
                off = jb * tq + js * LANES
                cs = _dot_sel_r(lft_ref[0, :, off:off + LANES], tri_ref[...]) + carry
                c_ref[jb, :, js * LANES:(js + 1) * LANES] = cs
                carry = cs[:, LANES - 1:LANES]

    bias_meta = _dot_sel_r(lfm_ref[...], um_ref[...])
    lane = lax.broadcasted_iota(I32, (tq, LANES), 1)
    row = lax.broadcasted_iota(I32, (tq, tq), 0)
    col = lax.broadcasted_iota(I32, (tq, tq), 1)
    half = FOX_HEAD_DIM
    for hp in range(FOX_HEADS // 2):
        sl = slice(hp * LANES, (hp + 1) * LANES)
        qp = q_ref[0, :, sl].astype(F32)
        km = km_ref[:, sl]
        vm = vm_ref[:, sl]
        outs = []
        for sub in range(2):
            h = 2 * hp + sub
            own = (lane < half) if sub == 0 else (lane >= half)
            qh = jnp.where(own, qp, 0.0).astype(BF16)
            s = _dot_nt(qh, km) + bias_meta[h:h + 1, :]
            m = jnp.max(s, axis=1, keepdims=True)
            p = jnp.exp(s - m)
            l = jnp.sum(p, axis=1, keepdims=True)
            acc = _dot(p.astype(BF16), vm)

            def body(j, carry, qh=qh, h=h, sl=sl):
                m, l, acc = carry
                start = pl.multiple_of(j * tq, tq)
                kc = kb_ref[0, pl.ds(start, tq), sl]
                vc = vb_ref[0, pl.ds(start, tq), sl]
                s = _dot_nt(qh, kc) - c_ref[j, h:h + 1, :]
                return _online_update(s, vc, m, l, acc)

            m, l, acc = lax.fori_loop(0, qi, body, (m, l, acc))
            start = pl.multiple_of(qi * tq, tq)
            kc = kb_ref[0, pl.ds(start, tq), sl]
            vc = vb_ref[0, pl.ds(start, tq), sl]
            s = _dot_nt(qh, kc) - c_ref[qi, h:h + 1, :]
            s = jnp.where(col <= row, s, -jnp.inf)
            m, l, acc = _online_update(s, vc, m, l, acc)
            outs.append(acc / l)
        o_ref[0, :, sl] = jnp.where(lane < half, outs[0], outs[1]).astype(BF16)


def _fox_prompt(q, kb, vb, lft, kb_meta, vb_meta, lft_meta, tq):
    b, l, w = q.shape
    assert l % tq == 0
    n_blk = l // tq
    tri = jnp.asarray(np.triu(np.ones((LANES, LANES), np.float32)), BF16)
    um = jnp.asarray(np.tril(np.ones((N_META, N_META), np.float32), -1), BF16)
    full = lambda a: pl.BlockSpec(a.shape, lambda bi, qi: (0,) * a.ndim)
    return pl.pallas_call(
        functools.partial(_fox_prompt_kernel, tq=tq), grid=(b, n_blk),
        in_specs=[pl.BlockSpec((1, tq, w), lambda bi, qi: (bi, qi, 0)),
                  pl.BlockSpec((1, l, w), lambda bi, qi: (bi, 0, 0)),
                  pl.BlockSpec((1, l, w), lambda bi, qi: (bi, 0, 0)),
                  pl.BlockSpec((1, FOX_HEADS, l), lambda bi, qi: (bi, 0, 0)),
                  full(kb_meta), full(vb_meta), full(lft_meta), full(tri), full(um)],
        out_specs=pl.BlockSpec((1, tq, w), lambda bi, qi: (bi, qi, 0)),
        out_shape=jax.ShapeDtypeStruct((b, l, w), BF16),
        scratch_shapes=[pltpu.VMEM((n_blk, FOX_HEADS, tq), F32)],
        compiler_params=_cparams(("parallel", "arbitrary")), name="fox_prompt",
    )(q, kb, vb, lft, kb_meta, vb_meta, lft_meta, tri, um)


def _fox_sample_kernel(pt_ref, q_ref, kn_ref, vn_ref, lfn_ref, u_ref, t8_ref, eh_ref, *rest, g, t):
    ck = rest[:g]
    cv = rest[g:2 * g]
    clf = rest[2 * g:3 * g]
    o_ref, qbd_ref, m_ref, l_ref, acc_ref, r_ref = rest[3 * g:]
    gi = pl.program_id(1)
    rows = FOX_HEADS * t
    rowh = lax.broadcasted_iota(I32, (rows, FOX_WIDTH), 0) // t
    laneh = lax.broadcasted_iota(I32, (rows, FOX_WIDTH), 1) // FOX_HEAD_DIM

    @pl.when(gi == 0)
    def _():
        q = q_ref[0].astype(F32)
        qt = jnp.concatenate([q] * FOX_HEADS, axis=0)
        qbd = jnp.where(rowh == laneh, qt, 0.0).astype(BF16)
        qbd_ref[...] = qbd
        a_t = _dot_sel_r(lfn_ref[0], t8_ref[...])
        bias = _dot_sel_l(eh_ref[...], -a_t)
        s = _dot_nt(qbd, kn_ref[0]) + bias
        rt = lax.broadcasted_iota(I32, (rows, t), 0) % t
        ct = lax.broadcasted_iota(I32, (rows, t), 1)
        s = jnp.where(ct <= rt, s, -jnp.inf)
        m = jnp.max(s, axis=1, keepdims=True)
        p = jnp.exp(s - m)
        m_ref[...] = m
        l_ref[...] = jnp.sum(p, axis=1, keepdims=True)
        acc_ref[...] = _dot(p.astype(BF16), vn_ref[0])
        r_ref[...] = jnp.zeros_like(r_ref)

    qbd = qbd_ref[...]
    r_run = r_ref[:, 0:1]
    scores = []
    for jj in reversed(range(g)):
        lft = clf[jj][0]
        r_t = _dot_sel_r(lft, u_ref[...]) + r_run
        r_run = r_run + jnp.sum(lft, axis=1, keepdims=True)
        bias = _dot_sel_l(eh_ref[...], r_t)
        scores.append(_dot_nt(qbd, ck[jj][0].astype(BF16)) + bias)
    r_ref[...] = jnp.broadcast_to(r_run, r_ref.shape)
    m = m_ref[...]
    m_new = m
    for s in scores:
        m_new = jnp.maximum(m_new, jnp.max(s, axis=1, keepdims=True))
    alpha = jnp.exp(m - m_new)
    l = alpha * l_ref[...]
    acc = alpha * acc_ref[...]
    for s, jj in zip(scores, reversed(range(g))):
        p = jnp.exp(s - m_new)
        l = l + jnp.sum(p, axis=1, keepdims=True)
        acc = acc + _dot(p.astype(BF16), cv[jj][0].astype(BF16))
    m_ref[...] = m_new
    l_ref[...] = l
    acc_ref[...] = acc

    @pl.when(gi == pl.num_programs(1) - 1)
    def _():
        o = jnp.where(rowh == laneh, acc / l, 0.0)
        out = o[0:t]
        for h in range(1, FOX_HEADS):
            out = out + o[h * t:(h + 1) * t]
        o_ref[0] = out.astype(BF16)


def _fox_sample(q, kb_new, vb_new, lft_new, cache_k, cache_v, cache_lft, page_table):
    db, t, w = q.shape
    n_pages = page_table.shape[1]
    g = min(PAGES_PER_STEP, n_pages)
    assert n_pages % g == 0
    ng = n_pages // g
    rows = FOX_HEADS * t
    u = jnp.asarray(np.tril(np.ones((PAGE_SIZE, PAGE_SIZE), np.float32), -1), BF16)
    t8 = jnp.asarray(np.triu(np.ones((t, t), np.float32)), BF16)
    eh = jnp.asarray(np.repeat(np.eye(FOX_HEADS, dtype=np.float32), t, axis=0), BF16)
    tok = lambda wd: pl.BlockSpec((1, t, wd), lambda b, gi, pt: (b, 0, 0))
    full = lambda a: pl.BlockSpec(a.shape, lambda b, gi, pt: (0,) * a.ndim)

    def page(shape, jj):
        return pl.BlockSpec(shape, lambda b, gi, pt: (pt[b * n_pages + (ng - 1 - gi) * g + jj], 0, 0))

    in_specs = ([tok(w), tok(w), tok(w), pl.BlockSpec((1, FOX_HEADS, t), lambda b, gi, pt: (b, 0, 0)),
                 full(u), full(t8), full(eh)]
                + [page((1, PAGE_SIZE, w), jj) for jj in range(g)]
                + [page((1, PAGE_SIZE, w), jj) for jj in range(g)]
                + [page((1, FOX_HEADS, PAGE_SIZE), jj) for jj in range(g)])
    grid_spec = pltpu.PrefetchScalarGridSpec(
        num_scalar_prefetch=1, grid=(db, ng), in_specs=in_specs,
        out_specs=pl.BlockSpec((1, t, w), lambda b, gi, pt: (b, 0, 0)),
        scratch_shapes=[pltpu.VMEM((rows, w), BF16), pltpu.VMEM((rows, 1), F32), pltpu.VMEM((rows, 1), F32),
                        pltpu.VMEM((rows, w), F32), pltpu.VMEM((FOX_HEADS, LANES), F32)])
    return pl.pallas_call(
        functools.partial(_fox_sample_kernel, g=g, t=t), grid_spec=grid_spec,
        out_shape=jax.ShapeDtypeStruct((db, t, w), BF16),
        compiler_params=_cparams(("parallel", "arbitrary")), name="fox_sample",
    )(page_table.reshape(-1), q, kb_new, vb_new, lft_new, u, t8, eh,
      *([cache_k] * g), *([cache_v] * g), *([cache_lft] * g))


def _gla_tables(c):
    idx = np.arange(c)
    ws, ms, ups = [], [np.eye(c, dtype=np.float32)], []
    sz = 2
    while sz <= c:
        seg = idx // sz
        upper = (idx % sz) >= sz // 2
        mid = seg * sz + sz // 2 - 1
        j = idx[None, :]
        r = idx[:, None]
        w_up = (j > mid[:, None]) & (j <= r)
        w_lo = (j > r) & (j <= mid[:, None])
        ws.append(np.where(upper[:, None], w_up, w_lo).astype(np.float32))
        ms.append(((seg[:, None] == seg[None, :]) & upper[:, None] & ~upper[None, :]).astype(np.float32))
        ups.append(np.broadcast_to(upper[:, None].astype(np.float32), (c, LANES)))
        sz *= 2
    tri = np.tril(np.ones((c, c), np.float32))
    return (jnp.asarray(np.stack(ws), BF16), jnp.asarray(np.stack(ms), F32),
            jnp.asarray(np.stack(ups), F32), jnp.asarray(tri, BF16))


def _gla_kernel(q_ref, k_ref, v_ref, la_ref, s0_ref, w_ref, msk_ref, up_ref, tri_ref,
                o_ref, sfin_ref, state_ref, *, c, n_levels):
    ci = pl.program_id(1)

    @pl.when(ci == 0)
    def _():
        state_ref[...] = s0_ref[0]

    lane = lax.broadcasted_iota(I32, (c, LANES), 1)
    r128 = lax.broadcasted_iota(I32, (LANES, LANES), 0)
    c128 = lax.broadcasted_iota(I32, (LANES, LANES), 1)
    half = GLA_KEY_DIM
    in_a = lane < half
    for p in range(GLA_HEADS // 2):
        ksl = slice(p * LANES, (p + 1) * LANES)
        q = q_ref[0, :, ksl]
        k = k_ref[0, :, ksl]
        la3 = _split3(la_ref[0, :, ksl])
        seg_sum = lambda m: _dot(m, la3[2]) + _dot(m, la3[1]) + _dot(m, la3[0])
        va = v_ref[0, :, (2 * p) * LANES:(2 * p + 1) * LANES].astype(BF16)
        vb = v_ref[0, :, (2 * p + 1) * LANES:(2 * p + 2) * LANES].astype(BF16)
        b = seg_sum(tri_ref[...])
        kb16 = k.astype(BF16)
        sa = msk_ref[0] * _dot_nt(jnp.where(in_a, q, 0.0).astype(BF16), kb16)
        sb = msk_ref[0] * _dot_nt(jnp.where(in_a, 0.0, q).astype(BF16), kb16)
        for lv in range(n_levels):
            x = jnp.where(up_ref[lv] > 0.5, q, k) * jnp.exp(seg_sum(w_ref[lv]))
            xb = x.astype(BF16)
            sa = sa + msk_ref[lv + 1] * _dot_nt(jnp.where(in_a, x, 0.0).astype(BF16), xb)
            sb = sb + msk_ref[lv + 1] * _dot_nt(jnp.where(in_a, 0.0, x).astype(BF16), xb)
        st = state_ref[p]
        st16 = st.astype(BF16)
        qe = q * jnp.exp(b)
        oa = _dot(jnp.where(in_a, qe, 0.0).astype(BF16), st16) + _dot(sa.astype(BF16), va)
        ob = _dot(jnp.where(in_a, 0.0, qe).astype(BF16), st16) + _dot(sb.astype(BF16), vb)
        o_ref[0, :, (2 * p) * LANES:(2 * p + 1) * LANES] = oa
        o_ref[0, :, (2 * p + 1) * LANES:(2 * p + 2) * LANES] = ob
        b_end = b[c - 1:c, :]
        kdec = (k * jnp.exp(b_end - b)).astype(BF16)
        dcol = _col_from_row(jnp.exp(b_end), LANES)
        upd = jnp.where(r128 < half, _dot_tn(kdec, va), _dot_tn(kdec, vb))
        state_ref[p] = dcol * st + upd

    @pl.when(ci == pl.num_programs(1) - 1)
    def _():
        sfin_ref[0] = state_ref[...]


def _gla(qg, kg, vg, la, s0, c):
    b, l, _ = qg.shape
    assert l % c == 0
    w, msk, up, tri = _gla_tables(c)
    n_levels = w.shape[0]
    s0_map = (lambda bi, ci: (bi, 0, 0, 0)) if s0.shape[0] == b else (lambda bi, ci: (0, 0, 0, 0))
    tok = lambda wd: pl.BlockSpec((1, c, wd), lambda bi, ci: (bi, ci, 0))
    full = lambda a: pl.BlockSpec(a.shape, lambda bi, ci: (0,) * a.ndim)
    st_shape = (GLA_HEADS // 2, LANES, GLA_VAL_DIM)
    return pl.pallas_call(
        functools.partial(_gla_kernel, c=c, n_levels=n_levels), grid=(b, l // c),
        in_specs=[tok(GLA_K), tok(GLA_K), tok(GLA_V), tok(GLA_K),
                  pl.BlockSpec((1,) + st_shape, s0_map), full(w), full(msk), full(up), full(tri)],
        out_specs=(tok(GLA_V), pl.BlockSpec((1,) + st_shape, lambda bi, ci: (bi, 0, 0, 0))),
        out_shape=(jax.ShapeDtypeStruct((b, l, GLA_V), F32), jax.ShapeDtypeStruct((b,) + st_shape, F32)),
        scratch_shapes=[pltpu.VMEM(st_shape, F32)],
        compiler_params=_cparams(("parallel", "arbitrary")), name="gla",
    )(qg, kg, vg, la, s0, w, msk, up, tri)


def _merge_kernel(x_ref, of_ref, og_ref, gmix_ref, wrg_ref, wfo_ref, wgo_ref, wo_ref, gng_ref, gffn_ref,
                  wrt_ref, brt_ref, x2_ref, h2_ref, eid_ref, wsel_ref):
    x = x_ref[...]
    tm, d = x.shape
    h = _rmsnorm(x, gmix_ref[...]).astype(BF16)
    z = _dot(h, wrg_ref[...])
    r = z[:, :GLA_V]
    gate_f = z[:, GLA_V:GLA_V + d]
    gate_g = z[:, GLA_V + d:]
    og = og_ref[...]
    parts = []
    for hh in range(GLA_HEADS):
        blk = og[:, hh * GLA_VAL_DIM:(hh + 1) * GLA_VAL_DIM]
        parts.append(blk * lax.rsqrt(jnp.mean(blk * blk, axis=-1, keepdims=True) + EPS))
    ogn = jnp.concatenate(parts, axis=1)
    ogs = (ogn * gng_ref[...] * (r * _sigmoid(r))).astype(BF16)
    branch_f = _dot(of_ref[...], wfo_ref[...])
    branch_g = _dot(ogs, wgo_ref[...])
    merged = _sigmoid(gate_f) * branch_f + _sigmoid(gate_g) * branch_g
    x2 = x + _dot(merged.astype(BF16), wo_ref[...])
    x2_ref[...] = x2
    h2 = _rmsnorm(x2, gffn_ref[...])
    h2_ref[...] = h2.astype(BF16)

    a = _split3(h2)
    wr = (wrt_ref[0], wrt_ref[1], wrt_ref[2])
    lt = (_dot_nt(wr[1], a[1]) + _dot_nt(wr[0], a[2]) + _dot_nt(wr[2], a[0])
          + _dot_nt(wr[0], a[1]) + _dot_nt(wr[1], a[0]) + _dot_nt(wr[0], a[0])) + brt_ref[...]
    row8 = lax.broadcasted_iota(I32, (SUBLANES, tm), 0)
    gl = jnp.where(row8 < N_GROUPS, lt[:SUBLANES], -jnp.inf)
    gmax = jnp.max(gl, axis=0, keepdims=True)
    g_idx = jnp.min(jnp.where(gl == gmax, row8, SUBLANES), axis=0, keepdims=True)
    g_sel = 1.0 / jnp.sum(jnp.exp(gl - gmax), axis=0, keepdims=True)
    rowe = lax.broadcasted_iota(I32, (N_EXPERTS, tm), 0)
    e1 = jnp.where(rowe // EXPERTS_PER_GROUP == g_idx, lt[SUBLANES:], -jnp.inf)
    v1 = jnp.max(e1, axis=0, keepdims=True)
    i1 = jnp.min(jnp.where(e1 == v1, rowe, N_EXPERTS), axis=0, keepdims=True)
    e2 = jnp.where(rowe == i1, -jnp.inf, e1)
    v2 = jnp.max(e2, axis=0, keepdims=True)
    i2 = jnp.min(jnp.where(e2 == v2, rowe, N_EXPERTS), axis=0, keepdims=True)
    tt = jnp.exp(v2 - v1)
    eid_ref[...] = jnp.concatenate([i1, i2], axis=0)
    wsel_ref[...] = jnp.concatenate([g_sel / (1.0 + tt), g_sel * tt / (1.0 + tt)], axis=0)


def _merge(x, o_fox, o_gla, mw, tm):
    n, d = x.shape
    assert n % tm == 0
    row = lambda wd: pl.BlockSpec((tm, wd), lambda i: (i, 0))
    full = lambda a: pl.BlockSpec(a.shape, lambda i: (0,) * a.ndim)
    ws = (mw["g_mix"], mw["w_rg"], mw["w_fo"], mw["w_go"], mw["w_o"], mw["g_gla"], mw["g_ffn"],
          mw["w_rt"], mw["b_rt"])
    return pl.pallas_call(
        _merge_kernel, grid=(n // tm,),
        in_specs=[row(d), row(FOX_WIDTH), row(GLA_V)] + [full(a) for a in ws],
        out_specs=(row(d), row(d), pl.BlockSpec((2, tm), lambda i: (0, i)), pl.BlockSpec((2, tm), lambda i: (0, i))),
        out_shape=(jax.ShapeDtypeStruct((n, d), F32), jax.ShapeDtypeStruct((n, d), BF16),
                   jax.ShapeDtypeStruct((2, n), I32), jax.ShapeDtypeStruct((2, n), F32)),
        compiler_params=_cparams(("parallel",)), name="merge",
    )(x, o_fox, o_gla, *ws)


def _run_sizes(max_rows):
    sizes = []
    s = RUN_ALIGN
    while s <= max_rows:
        sizes.append(s)
        s *= 2
    return tuple(reversed(sizes))


def _for_each_run_piece(n_rows, a_off, b_off, sizes, fn):
    for size in sizes:
        bit = (n_rows & size) != 0

        @pl.when(bit)
        def _(a_off=a_off, b_off=b_off, size=size):
            fn(pl.multiple_of(a_off, RUN_ALIGN), pl.multiple_of(b_off, RUN_ALIGN), size)

        inc = jnp.where(bit, size, 0)
        a_off = a_off + inc
        b_off = b_off + inc


def _dispatch_kernel(p_ref, loc_ref, dst_ref, toff_ref, tlen_ref, nu_ref,
                     h2_ref, eid_ref, wsel_ref, locf_ref, triu_ref,
                     xs_ref, pos_ref, xloc_ref, zero_ref, sem, *, nt, bt, sl, d, n_tiles):
    i = pl.program_id(0)
    sizes = _run_sizes(bt)
    t = zero_ref.shape[0]

    def run_copy(lo, do, size):
        return pltpu.make_async_copy(xloc_ref.at[pl.ds(lo, size)], xs_ref.at[pl.ds(do, size)], sem)

    @pl.when(i < nt)
    def _():
        eid = eid_ref[...]
        wsel = wsel_ref[...]
        rowe = lax.broadcasted_iota(I32, (N_EXPERTS, bt), 0)
        oh0 = jnp.where(rowe == eid[0:1], 1.0, 0.0)
        oh1 = jnp.where(rowe == eid[1:2], 1.0, 0.0)
        pref0 = _dot(oh0.astype(BF16), triu_ref[...])
        pref1 = _dot(oh1.astype(BF16), triu_ref[...])
        n0 = jnp.sum(oh0, axis=1, keepdims=True)
        loc = locf_ref[0]
        pos0 = jnp.sum(oh0 * (loc + pref0), axis=0, keepdims=True)
        pos1 = jnp.sum(oh1 * (loc + n0 + pref1), axis=0, keepdims=True)
        pos_ref[...] = jnp.concatenate([pos0, pos1], axis=0)
        srow = lax.broadcasted_iota(I32, (sl, bt), 0).astype(F32)
        m0 = srow == pos0
        m1 = srow == pos1
        perm = jnp.where(m0, 1.0, jnp.where(m1, 1.0, 0.0)).astype(BF16)
        wslot = jnp.sum(jnp.where(m0, wsel[0:1], jnp.where(m1, wsel[1:2], 0.0)), axis=1, keepdims=True)
        xloc_ref[:, :d] = _dot(perm, h2_ref[...])
        xloc_ref[:, d:] = jnp.broadcast_to(wslot, (sl, W_COLS))

        def runs(act):
            def body(e, carry):
                idx = i * N_EXPERTS + e
                _for_each_run_piece(p_ref[idx], loc_ref[idx], dst_ref[idx], sizes,
                                    lambda lo, do, size: act(run_copy(lo, do, size)))
                return carry
            lax.fori_loop(0, N_EXPERTS, body, 0)

        runs(lambda cp: cp.start())
        runs(lambda cp: cp.wait())

    @pl.when(i == nt)
    def _():
        zero_ref[...] = jnp.zeros_like(zero_ref)
        tsizes = _run_sizes(t // 2)

        def tail_copy(zo, do, size):
            return pltpu.make_async_copy(zero_ref.at[pl.ds(0, size)], xs_ref.at[pl.ds(do, size)], sem)

        def tails(act):
            def body(e, carry):
                _for_each_run_piece(tlen_ref[e], jnp.int32(0), toff_ref[e], tsizes,
                                    lambda zo, do, size: act(tail_copy(zo, do, size)))
                return carry
            lax.fori_loop(0, N_EXPERTS, body, 0)

            def unused(j, carry):
                act(tail_copy(0, pl.multiple_of(j * t, t), t))
                return carry
            lax.fori_loop(nu_ref[0], n_tiles, unused, 0)

        tails(lambda cp: cp.start())
        tails(lambda cp: cp.wait())


def _ffn_kernel(te_ref, nu_ref, xs_ref, wg_ref, wu_ref, wd_ref, y_ref, *, d):
    i = pl.program_id(0)

    @pl.when(i < nu_ref[0])
    def _():
        x = xs_ref[:, :d].astype(BF16)
        g = _dot(x, wg_ref[0])
        u = _dot(x, wu_ref[0])
        a = (g * _sigmoid(g) * u).astype(BF16)
        y = _dot(a, wd_ref[0])
        wrow = xs_ref[:, d:]
        y_ref[...] = y * jnp.concatenate([wrow] * (d // W_COLS), axis=1)

    @pl.when(i >= nu_ref[0])
    def _():
        y_ref[...] = jnp.zeros_like(y_ref)


def _combine_kernel(p_ref, loc_ref, dst_ref, pos_ref, x2_ref, gfin_ref, y_hbm, out_ref, yloc_ref, sem,
                    *, bt, sl):
    i = pl.program_id(0)
    sizes = _run_sizes(bt)
    yloc_ref[...] = jnp.zeros_like(yloc_ref)

    def run_copy(lo, do, size):
        return pltpu.make_async_copy(y_hbm.at[pl.ds(do, size)], yloc_ref.at[pl.ds(lo, size)], sem)

    def runs(act):
        def body(e, carry):
            idx = i * N_EXPERTS + e
            _for_each_run_piece(p_ref[idx], loc_ref[idx], dst_ref[idx], sizes,
                                lambda lo, do, size: act(run_copy(lo, do, size)))
            return carry
        lax.fori_loop(0, N_EXPERTS, body, 0)

    runs(lambda cp: cp.start())
    runs(lambda cp: cp.wait())
    pos = pos_ref[...]
    pos0 = _col_from_row(pos[0:1], bt)
    pos1 = _col_from_row(pos[1:2], bt)
    scol = lax.broadcasted_iota(I32, (bt, sl), 1).astype(F32)
    perm_t = jnp.where(scol == pos0, 1.0, jnp.where(scol == pos1, 1.0, 0.0)).astype(BF16)
    moe = _dot(perm_t, yloc_ref[...].astype(BF16))
    out_ref[...] = _rmsnorm(x2_ref[...] + moe, gfin_ref[...])


def _moe_prompt(x2, h2, eid, wsel, wg, wu, wd, g_fin):
    n, d = x2.shape
    bt, t = MOE_TOKENS, MOE_ROWS
    assert n % bt == 0
    nt = n // bt
    sl = -(-(2 * bt + N_EXPERTS * (RUN_ALIGN - 1)) // LANES) * LANES
    onehot = eid.reshape(2, nt, bt)[..., None] == jnp.arange(N_EXPERTS, dtype=I32)
    cnt = jnp.sum(onehot, axis=(0, 2), dtype=I32)
    p = (cnt + RUN_ALIGN - 1) // RUN_ALIGN * RUN_ALIGN
    loc = jnp.cumsum(p, axis=1) - p
    tot = jnp.sum(p, axis=0)
    reg = (tot + t - 1) // t * t
    reg_end = jnp.cumsum(reg)
    reg_start = reg_end - reg
    dst = reg_start[None, :] + jnp.cumsum(p, axis=0) - p
    n_tiles = (2 * n + nt * N_EXPERTS * (RUN_ALIGN - 1) + N_EXPERTS * (t - 1)) // t + 1
    s_max = n_tiles * t
    n_used = (reg_end[-1] // t).astype(I32).reshape(1)
    tile_expert = jnp.minimum(
        jnp.searchsorted(reg_end, jnp.arange(n_tiles, dtype=I32) * t, side="right"), N_EXPERTS - 1).astype(I32)
    p_f, loc_f, dst_f = p.reshape(-1), loc.reshape(-1), dst.reshape(-1).astype(I32)
    locf = loc.astype(F32).reshape(nt, N_EXPERTS, 1)
    triu = jnp.asarray(np.triu(np.ones((bt, bt), np.float32), 1), BF16)

    last = nt - 1
    xs, pos = pl.pallas_call(
        functools.partial(_dispatch_kernel, nt=nt, bt=bt, sl=sl, d=d, n_tiles=n_tiles),
        grid_spec=pltpu.PrefetchScalarGridSpec(
            num_scalar_prefetch=6, grid=(nt + 1,),
            in_specs=[pl.BlockSpec((bt, d), lambda i, *_: (jnp.minimum(i, last), 0)),
                      pl.BlockSpec((2, bt), lambda i, *_: (0, jnp.minimum(i, last))),
                      pl.BlockSpec((2, bt), lambda i, *_: (0, jnp.minimum(i, last))),
                      pl.BlockSpec((1, N_EXPERTS, 1), lambda i, *_: (jnp.minimum(i, last), 0, 0)),
                      pl.BlockSpec((bt, bt), lambda i, *_: (0, 0))],
            out_specs=(pl.BlockSpec(memory_space=pl.ANY),
                       pl.BlockSpec((2, bt), lambda i, *_: (0, jnp.minimum(i, last)))),
            scratch_shapes=[pltpu.VMEM((sl, d + W_COLS), F32), pltpu.VMEM((t, d + W_COLS), F32),
                            pltpu.SemaphoreType.DMA(())]),
        out_shape=(jax.ShapeDtypeStruct((s_max, d + W_COLS), F32), jax.ShapeDtypeStruct((2, n), F32)),
        compiler_params=_cparams(("arbitrary",)), name="moe_dispatch",
    )(p_f, loc_f, dst_f, (reg_start + tot).astype(I32), (reg - tot).astype(I32), n_used, h2, eid, wsel, locf, triu)

    de = wg.shape[2]
    used = lambda i, te, nu: jnp.minimum(i, nu[0] - 1)
    y = pl.pallas_call(
        functools.partial(_ffn_kernel, d=d),
        grid_spec=pltpu.PrefetchScalarGridSpec(
            num_scalar_prefetch=2, grid=(n_tiles,),
            in_specs=[pl.BlockSpec((t, d + W_COLS), lambda i, te, nu: (used(i, te, nu), 0)),
                      pl.BlockSpec((1, d, de), lambda i, te, nu: (te[used(i, te, nu)], 0, 0)),
                      pl.BlockSpec((1, d, de), lambda i, te, nu: (te[used(i, te, nu)], 0, 0)),
                      pl.BlockSpec((1, de, d), lambda i, te, nu: (te[used(i, te, nu)], 0, 0))],
            out_specs=pl.BlockSpec((t, d), lambda i, te, nu: (i, 0))),
        out_shape=jax.ShapeDtypeStruct((s_max, d), F32),
        compiler_params=_cparams(("arbitrary",)), name="moe_ffn",
    )(tile_expert, n_used, xs, wg, wu, wd)

    return pl.pallas_call(
        functools.partial(_combine_kernel, bt=bt, sl=sl),
        grid_spec=pltpu.PrefetchScalarGridSpec(
            num_scalar_prefetch=3, grid=(nt,),
            in_specs=[pl.BlockSpec((2, bt), lambda i, *_: (0, i)),
                      pl.BlockSpec((bt, d), lambda i, *_: (i, 0)),
                      pl.BlockSpec((1, d), lambda i, *_: (0, 0)),
                      pl.BlockSpec(memory_space=pl.ANY)],
            out_specs=pl.BlockSpec((bt, d), lambda i, *_: (i, 0)),
            scratch_shapes=[pltpu.VMEM((sl, d), F32), pltpu.SemaphoreType.DMA(())]),
        out_shape=jax.ShapeDtypeStruct((n, d), F32),
        compiler_params=_cparams(("arbitrary",)), name="moe_combine",
    )(p_f, loc_f, dst_f, pos, x2, g_fin, y)


def _moe_dense_kernel(x2_ref, h2_ref, eid_ref, wsel_ref, gfin_ref, wg_ref, wu_ref, wd_ref, out_ref, acc_ref):
    e = pl.program_id(0)
    n = x2_ref.shape[0]

    @pl.when(e == 0)
    def _():
        acc_ref[...] = jnp.zeros_like(acc_ref)

    eid = eid_ref[...]
    wsel = wsel_ref[...]
    gate_row = jnp.where(eid[0:1] == e, wsel[0:1], 0.0) + jnp.where(eid[1:2] == e, wsel[1:2], 0.0)
    gate = _col_from_row(gate_row, n)
    x = h2_ref[...]
    g = _dot(x, wg_ref[0])
    u = _dot(x, wu_ref[0])
    a = (g * _sigmoid(g) * u).astype(BF16)
    acc_ref[...] += gate * _dot(a, wd_ref[0])

    @pl.when(e == pl.num_programs(0) - 1)
    def _():
        out_ref[...] = _rmsnorm(x2_ref[...] + acc_ref[...], gfin_ref[...])


def _moe_dense(x2, h2, eid, wsel, wg, wu, wd, g_fin):
    n, d = x2.shape
    de = wg.shape[2]
    full = lambda a: pl.BlockSpec(a.shape, lambda e: (0,) * a.ndim)
    return pl.pallas_call(
        _moe_dense_kernel, grid=(wg.shape[0],),
        in_specs=[full(x2), full(h2), full(eid), full(wsel), full(g_fin),
                  pl.BlockSpec((1, d, de), lambda e: (e, 0, 0)),
                  pl.BlockSpec((1, d, de), lambda e: (e, 0, 0)),
                  pl.BlockSpec((1, de, d), lambda e: (e, 0, 0))],
        out_specs=pl.BlockSpec((n, d), lambda e: (0, 0)),
        out_shape=jax.ShapeDtypeStruct((n, d), F32),
        scratch_shapes=[pltpu.VMEM((n, d), F32)],
        compiler_params=_cparams(("arbitrary",)), name="moe_dense",
    )(x2, h2, eid, wsel, g_fin, wg, wu, wd)


def _layer_weights(norm_mix_g, w_in, fox_f_bias, gla_w_a2, gla_b_a, gla_norm_g, w_fox_out, w_gla_out, w_o,
                   norm_ffn_g, w_group_router, b_group_router, w_expert_router, b_expert_router):
    d = w_in.shape[0]
    c = 0
    parts = {}
    for name, width in (("q", FOX_WIDTH), ("k", FOX_WIDTH), ("v", FOX_WIDTH), ("ff", FOX_HEADS),
                        ("qg", GLA_K), ("kg", GLA_K), ("vg", GLA_V), ("lr", GLA_RANK), ("r", GLA_V),
                        ("gf", d), ("gg", d)):
        parts[name] = w_in[:, c:c + width]
        c += width
    assert c == w_in.shape[1]
    g_mix = norm_mix_g.reshape(1, d)
    w_fft = jnp.zeros((2 * SUBLANES, d), F32).at[:FOX_HEADS].set(parts["ff"].T).astype(BF16)
    pw = dict(
        g_mix=g_mix,
        w_qkv=jnp.concatenate([parts["q"], parts["k"], parts["v"]], axis=1).astype(BF16),
        w_fft=w_fft,
        w_gla=jnp.concatenate([parts["qg"], parts["kg"], parts["vg"]], axis=1).astype(BF16),
        w_lr=parts["lr"].astype(BF16),
        w_a2=gla_w_a2.astype(BF16),
        b_a=gla_b_a.reshape(1, GLA_K),
        f_bias=fox_f_bias.reshape(FOX_HEADS, 1),
    )
    w_rt = jnp.zeros((SUBLANES + N_EXPERTS, d), F32)
    w_rt = w_rt.at[:N_GROUPS].set(w_group_router.T).at[SUBLANES:].set(w_expert_router.T)
    b_rt = jnp.zeros((SUBLANES + N_EXPERTS, 1), F32)
    b_rt = b_rt.at[:N_GROUPS, 0].set(b_group_router).at[SUBLANES:, 0].set(b_expert_router)
    mw = dict(
        g_mix=g_mix,
        w_rg=jnp.concatenate([parts["r"], parts["gf"], parts["gg"]], axis=1).astype(BF16),
        w_fo=w_fox_out.astype(BF16), w_go=w_gla_out.astype(BF16), w_o=w_o.astype(BF16),
        g_gla=gla_norm_g.reshape(1, GLA_V), g_ffn=norm_ffn_g.reshape(1, d),
        w_rt=jnp.stack(_split3(w_rt)), b_rt=b_rt,
    )
    return pw, mw


def kernel(x_prompt, x_sample, cache_k, cache_v, cache_log_f, state_gla, page_table, meta_tokens, norm_mix_g,
           w_in, fox_f_bias, gla_w_a2, gla_b_a, gla_norm_g, w_fox_out, w_gla_out, w_o, norm_ffn_g,
           w_group_router, b_group_router, w_expert_router, b_expert_router, w_expert_gate, w_expert_up,
           w_expert_down, norm_final_g):
    depth = w_in.shape[0]
    assert depth == 1, "meta rows are only carried through one layer"
    b, seq, d = x_prompt.shape
    db, t_dec, _ = x_sample.shape
    n_phys = cache_k.shape[1]
    pw, mw = _layer_weights(norm_mix_g[0], w_in[0], fox_f_bias[0], gla_w_a2[0], gla_b_a[0], gla_norm_g[0],
                            w_fox_out[0], w_gla_out[0], w_o[0], norm_ffn_g[0], w_group_router[0],
                            b_group_router[0], w_expert_router[0], b_expert_router[0])
    wg = w_expert_gate[0].astype(BF16)
    wu = w_expert_up[0].astype(BF16)
    wd = w_expert_down[0].astype(BF16)
    g_fin = norm_final_g.reshape(1, d)

    (_, k_m, v_m, kb_m, vb_m, lft_m, qg_m, kg_m, vg_m, la_m) = _project(
        meta_tokens.reshape(1, N_META, d), N_META, pw)
    zero_state = jnp.zeros((1, GLA_HEADS // 2, LANES, GLA_VAL_DIM), F32)
    _, s_meta = _gla(qg_m, kg_m, vg_m, la_m, zero_state, N_META)

    q, k, v, kb, vb, lft, qg, kg, vg, la = _project(x_prompt, min(PROJ_TILE, seq), pw)
    o_fox = _fox_prompt(q, kb, vb, lft, kb_m[0], vb_m[0], lft_m[0], min(ATTN_TILE, seq))
    o_gla, s_gla_p = _gla(qg, kg, vg, la, s_meta, min(GLA_CHUNK, seq))
    n = b * seq
    x2, h2, eid, wsel = _merge(x_prompt.reshape(n, d), o_fox.reshape(n, FOX_WIDTH), o_gla.reshape(n, GLA_V),
                               mw, min(MERGE_TILE, n))
    y_prompt = _moe_prompt(x2, h2, eid, wsel, wg, wu, wd, g_fin).reshape(b, seq, d)

    ns = db * t_dec
    qs, ks, vs, kbs, vbs, lfts, qgs, kgs, vgs, las = _project(x_sample.reshape(1, ns, d), ns, pw)
    tok = lambda a: a.reshape(db, t_dec, a.shape[-1])
    lft_s = lfts.reshape(FOX_HEADS, db, t_dec).transpose(1, 0, 2)
    cache_lft = cache_log_f[0].transpose(0, 2, 1)
    o_fox_s = _fox_sample(tok(qs), tok(kbs), tok(vbs), lft_s,
                          cache_k[0].reshape(n_phys, PAGE_SIZE, FOX_WIDTH),
                          cache_v[0].reshape(n_phys, PAGE_SIZE, FOX_WIDTH), cache_lft, page_table)
    s0 = state_gla[0].reshape(db, GLA_HEADS // 2, LANES, GLA_VAL_DIM)
    o_gla_s, s_gla_s = _gla(tok(qgs), tok(kgs), tok(vgs), tok(las), s0, t_dec)
    x2s, h2s, eids, wsels = _merge(x_sample.reshape(ns, d), o_fox_s.reshape(ns, FOX_WIDTH),
                                   o_gla_s.reshape(ns, GLA_V), mw, ns)
    y_sample = _moe_dense(x2s, h2s, eids, wsels, wg, wu, wd, g_fin).reshape(db, t_dec, d)

    heads = lambda a: a.reshape(a.shape[:-1] + (FOX_HEADS, FOX_HEAD_DIM))
    with_meta = lambda m, r: jnp.concatenate([jnp.broadcast_to(m, (b,) + m.shape[1:]), r], axis=1)
    new_k_prompt = heads(with_meta(k_m, k))[None]
    new_v_prompt = heads(with_meta(v_m, v))[None]
    new_log_f_prompt = with_meta(lft_m.transpose(0, 2, 1), lft.transpose(0, 2, 1))[None]
    new_gla_prompt = s_gla_p.reshape(1, b, GLA_HEADS, GLA_KEY_DIM, GLA_VAL_DIM)
    new_k_sample = heads(tok(ks))[None]
    new_v_sample = heads(tok(vs))[None]
    new_log_f_sample = lfts[0].T.reshape(1, db, t_dec, FOX_HEADS)
    new_gla_sample = s_gla_s.reshape(1, db, GLA_HEADS, GLA_KEY_DIM, GLA_VAL_DIM)
    return (y_prompt, y_sample, new_k_prompt, new_v_prompt, new_log_f_prompt, new_gla_prompt,
            new_k_sample, new_v_sample, new_log_f_sample, new_gla_sample)
```

```python
import functools

import numpy as np
import jax
import jax.numpy as jnp
from jax import lax
from jax.experimental import pallas as pl
from jax.experimental.pallas import tpu as pltpu

F32 = jnp.float32
BF16 = jnp.bfloat16
I32 = jnp.int32
EPS = 1e-6
N_META = 16
FOX_HEADS = 8
FOX_HEAD_DIM = 64
FOX_WIDTH = FOX_HEADS * FOX_HEAD_DIM
GLA_HEADS = 4
GLA_KEY_DIM = 64
GLA_VAL_DIM = 128
GLA_K = GLA_HEADS * GLA_KEY_DIM
GLA_V = GLA_HEADS * GLA_VAL_DIM
GLA_RANK = 16
GLA_TAU = 16.0
GLA_CHUNK = 128
N_GROUPS = 4
EXPERTS_PER_GROUP = 8
N_EXPERTS = N_GROUPS * EXPERTS_PER_GROUP
PAGE_SIZE = 128

LANES = 128
SUBLANES = 8
VMEM_LIMIT = 56 * 1024 * 1024

PROJ_TILE = 512
ATTN_TILE = 512
MERGE_TILE = 256
MOE_TOKENS = 512
MOE_ROWS = 256
RUN_ALIGN = SUBLANES
PAGES_PER_STEP = 8
GLA_SEQS_PER_STEP = 4
W_COLS = LANES


def _cparams(sem):
    return pltpu.CompilerParams(dimension_semantics=sem, vmem_limit_bytes=VMEM_LIMIT)


def _dot(a, b):
    return jnp.dot(a, b, preferred_element_type=F32)


def _dot_nt(a, b):
    return lax.dot_general(a, b, (((1,), (1,)), ((), ())), preferred_element_type=F32)


def _dot_tn(a, b):
    return lax.dot_general(a, b, (((0,), (0,)), ((), ())), preferred_element_type=F32)


def _split3(a):
    hi = a.astype(BF16)
    r1 = a - hi.astype(F32)
    mid = r1.astype(BF16)
    lo = (r1 - mid.astype(F32)).astype(BF16)
    return hi, mid, lo


def _dot_sel_l(m, a):
    hi, mid, lo = _split3(a)
    return _dot(m, lo) + _dot(m, mid) + _dot(m, hi)


def _dot_sel_r(a, m):
    hi, mid, lo = _split3(a)
    return _dot(lo, m) + _dot(mid, m) + _dot(hi, m)


def _rmsnorm(x, g):
    return x * lax.rsqrt(jnp.mean(x * x, axis=-1, keepdims=True) + EPS) * g


def _log_sigmoid(x):
    return jnp.minimum(x, 0.0) - jnp.log1p(jnp.exp(-jnp.abs(x)))


def _sigmoid(x):
    return 1.0 / (1.0 + jnp.exp(-x))


def _col_from_row(row, n):
    r = lax.broadcasted_iota(I32, (n, n), 0)
    c = lax.broadcasted_iota(I32, (n, n), 1)
    return jnp.sum(jnp.where(r == c, jnp.broadcast_to(row, (n, n)), 0.0), axis=1, keepdims=True)


def _proj_kernel(x_ref, g_ref, wqkv_ref, wfft_ref, wgla_ref, wlr_ref, wa2_ref, ba_ref, fb_ref,
                 q_ref, k_ref, v_ref, kb_ref, vb_ref, lft_ref, qg_ref, kg_ref, vg_ref, la_ref):
    h = _rmsnorm(x_ref[0], g_ref[...]).astype(BF16)
    z = _dot(h, wqkv_ref[...])
    q_ref[0] = (z[:, :FOX_WIDTH] * FOX_HEAD_DIM ** -0.5).astype(BF16)
    k = z[:, FOX_WIDTH:2 * FOX_WIDTH]
    v = z[:, 2 * FOX_WIDTH:]
    k_ref[0] = k
    v_ref[0] = v
    kb_ref[0] = k.astype(BF16)
    vb_ref[0] = v.astype(BF16)
    fft = _dot_nt(wfft_ref[...], h)[:FOX_HEADS]
    lft_ref[0] = _log_sigmoid(fft + fb_ref[...])
    zg = _dot(h, wgla_ref[...])
    qg_ref[0] = zg[:, :GLA_K] * GLA_KEY_DIM ** -0.5
    kg_ref[0] = zg[:, GLA_K:2 * GLA_K]
    vg_ref[0] = zg[:, 2 * GLA_K:]
    lr = _dot(h, wlr_ref[...])
    xa = _dot(lr.astype(BF16), wa2_ref[...]) + ba_ref[...]
    la_ref[0] = _log_sigmoid(xa) / GLA_TAU


def _project(x3, tm, pw):
    bx, lx, d = x3.shape
    assert lx % tm == 0
    grid = (bx, lx // tm)
    row = lambda w: pl.BlockSpec((1, tm, w), lambda b, i: (b, i, 0))
    full = lambda a: pl.BlockSpec(a.shape, lambda b, i: (0,) * a.ndim)
    ws = (pw["g_mix"], pw["w_qkv"], pw["w_fft"], pw["w_gla"], pw["w_lr"], pw["w_a2"], pw["b_a"], pw["f_bias"])
    out_shape = (
        jax.ShapeDtypeStruct((bx, lx, FOX_WIDTH), BF16),
        jax.ShapeDtypeStruct((bx, lx, FOX_WIDTH), F32),
        jax.ShapeDtypeStruct((bx, lx, FOX_WIDTH), F32),
        jax.ShapeDtypeStruct((bx, lx, FOX_WIDTH), BF16),
        jax.ShapeDtypeStruct((bx, lx, FOX_WIDTH), BF16),
        jax.ShapeDtypeStruct((bx, FOX_HEADS, lx), F32),
        jax.ShapeDtypeStruct((bx, lx, GLA_K), F32),
        jax.ShapeDtypeStruct((bx, lx, GLA_K), F32),
        jax.ShapeDtypeStruct((bx, lx, GLA_V), F32),
        jax.ShapeDtypeStruct((bx, lx, GLA_K), F32),
    )
    out_specs = (row(FOX_WIDTH), row(FOX_WIDTH), row(FOX_WIDTH), row(FOX_WIDTH), row(FOX_WIDTH),
                 pl.BlockSpec((1, FOX_HEADS, tm), lambda b, i: (b, 0, i)),
                 row(GLA_K), row(GLA_K), row(GLA_V), row(GLA_K))
    return pl.pallas_call(
        _proj_kernel, grid=grid,
        in_specs=[row(d)] + [full(a) for a in ws],
        out_specs=out_specs, out_shape=out_shape,
        compiler_params=_cparams(("parallel", "parallel")), name="proj",
    )(x3, *ws)


def _online_update(s, v, m, l, acc):
    m_new = jnp.maximum(m, jnp.max(s, axis=1, keepdims=True))
    alpha = jnp.exp(m - m_new)
    p = jnp.exp(s - m_new)
    l = alpha * l + jnp.sum(p, axis=1, keepdims=True)
    acc = alpha * acc + _dot(p.astype(BF16), v)
    return m_new, l, acc


def _fox_prompt_kernel(q_ref, kb_ref, vb_ref, lft_ref, km_ref, vm_ref, lfm_ref, tri_ref, um_ref,
                       o_ref, c_ref, *, tq):
    qi = pl.program_id(1)
    n_blk = kb_ref.shape[1] // tq

    @pl.when(qi == 0)
    def _():
        carry = jnp.zeros((FOX_HEADS, 1), F32)
        for jb in range(n_blk):
            for js in range(tq // LANES):
                off = jb * tq + js * LANES
                cs = _dot_sel_r(lft_ref[0, :, off:off + LANES], tri_ref[...]) + carry
                c_ref[jb, :, js * LANES:(js + 1) * LANES] = cs
                carry = cs[:, LANES - 1:LANES]

    bias_meta = _dot_sel_r(lfm_ref[...], um_ref[...])
    lane = lax.broadcasted_iota(I32, (tq, LANES), 1)
    row = lax.broadcasted_iota(I32, (tq, tq), 0)
    col = lax.broadcasted_iota(I32, (tq, tq), 1)
    half = FOX_HEAD_DIM
    for hp in range(FOX_HEADS // 2):
        sl = slice(hp * LANES, (hp + 1) * LANES)
        qp = q_ref[0, :, sl].astype(F32)
        km = km_ref[:, sl]
        vm = vm_ref[:, sl]
        outs = []
        for sub in range(2):
            h = 2 * hp + sub
            own = (lane < half) if sub == 0 else (lane >= half)
            qh = jnp.where(own, qp, 0.0).astype(BF16)
            s = _dot_nt(qh, km) + bias_meta[h:h + 1, :]
            m = jnp.max(s, axis=1, keepdims=True)
            p = jnp.exp(s - m)
            l = jnp.sum(p, axis=1, keepdims=True)
            acc = _dot(p.astype(BF16), vm)

            def body(j, carry, qh=qh, h=h, sl=sl):
                m, l, acc = carry
                start = pl.multiple_of(j * tq, tq)
                kc = kb_ref[0, pl.ds(start, tq), sl]
                vc = vb_ref[0, pl.ds(start, tq), sl]
                s = _dot_nt(qh, kc) - c_ref[j, h:h + 1, :]
                return _online_update(s, vc, m, l, acc)

            m, l, acc = lax.fori_loop(0, qi, body, (m, l, acc))
            start = pl.multiple_of(qi * tq, tq)
            kc = kb_ref[0, pl.ds(start, tq), sl]
            vc = vb_ref[0, pl.ds(start, tq), sl]
            s = _dot_nt(qh, kc) - c_ref[qi, h:h + 1, :]
            s = jnp.where(col <= row, s, -jnp.inf)
            m, l, acc = _online_update(s, vc, m, l, acc)
            outs.append(acc / l)
        o_ref[0, :, sl] = jnp.where(lane < half, outs[0], outs[1]).astype(BF16)


def _fox_prompt(q, kb, vb, lft, kb_meta, vb_meta, lft_meta, tq):
    b, l, w = q.shape
    assert l % tq == 0
    n_blk = l // tq
    tri = jnp.asarray(np.triu(np.ones((LANES, LANES), np.float32)), BF16)
    um = jnp.asarray(np.tril(np.ones((N_META, N_META), np.float32), -1), BF16)
    full = lambda a: pl.BlockSpec(a.shape, lambda bi, qi: (0,) * a.ndim)
    return pl.pallas_call(
        functools.partial(_fox_prompt_kernel, tq=tq), grid=(b, n_blk),
        in_specs=[pl.BlockSpec((1, tq, w), lambda bi, qi: (bi, qi, 0)),
                  pl.BlockSpec((1, l, w), lambda bi, qi: (bi, 0, 0)),
                  pl.BlockSpec((1, l, w), lambda bi, qi: (bi, 0, 0)),
                  pl.BlockSpec((1, FOX_HEADS, l), lambda bi, qi: (bi, 0, 0)),
                  full(kb_meta), full(vb_meta), full(lft_meta), full(tri), full(um)],
        out_specs=pl.BlockSpec((1, tq, w), lambda bi, qi: (bi, qi, 0)),
        out_shape=jax.ShapeDtypeStruct((b, l, w), BF16),
        scratch_shapes=[pltpu.VMEM((n_blk, FOX_HEADS, tq), F32)],
        compiler_params=_cparams(("parallel", "arbitrary")), name="fox_prompt",
    )(q, kb, vb, lft, kb_meta, vb_meta, lft_meta, tri, um)


def _fox_sample_kernel(pt_ref, q_ref, kn_ref, vn_ref, lfn_ref, t8_ref, eh_ref, ug_ref, *rest, g, t):
    ck = rest[:g]
    cv = rest[g:2 * g]
    clf = rest[2 * g:3 * g]
    o_ref, qm_ref, m_ref, l_ref, acc_ref, r_ref = rest[3 * g:]
    gi = pl.program_id(1)
    rows = FOX_HEADS * t
    flat = PAGE_SIZE * FOX_HEADS
    dh = FOX_HEAD_DIM

    @pl.when(gi == 0)
    def _():
        rowh = lax.broadcasted_iota(I32, (rows, FOX_WIDTH), 0) // t
        laneh = lax.broadcasted_iota(I32, (rows, FOX_WIDTH), 1) // dh
        q = q_ref[0].astype(F32)
        qt = jnp.concatenate([q] * FOX_HEADS, axis=0)
        qbd = jnp.where(rowh == laneh, qt, 0.0).astype(BF16)
        a_t = _dot_sel_r(lfn_ref[0], t8_ref[...])
        bias = _dot_sel_l(eh_ref[...], -a_t)
        s = _dot_nt(qbd, kn_ref[0]) + bias
        rt = lax.broadcasted_iota(I32, (rows, t), 0) % t
        ct = lax.broadcasted_iota(I32, (rows, t), 1)
        s = jnp.where(ct <= rt, s, -jnp.inf)
        m = jnp.max(s, axis=1, keepdims=True)
        p = jnp.exp(s - m)
        m_ref[...] = m
        l_ref[...] = jnp.sum(p, axis=1, keepdims=True)
        acc_w = _dot(p.astype(BF16), vn_ref[0])
        acc_ref[...] = jnp.concatenate(
            [acc_w[h * t:(h + 1) * t, h * dh:(h + 1) * dh] for h in range(FOX_HEADS)], axis=0)
        qm_ref[...] = jnp.concatenate(
            [q[:, h * dh:(h + 1) * dh] for h in range(FOX_HEADS)], axis=0).astype(BF16)
        r_ref[...] = jnp.zeros_like(r_ref)

    lane = lax.broadcasted_iota(I32, (g, flat), 1)
    x = jnp.concatenate([clf[jj][0] for jj in range(g)], axis=0)
    suf = x
    pre = x
    sh = FOX_HEADS
    while sh < flat:
        suf = suf + jnp.where(lane < flat - sh, pltpu.roll(suf, flat - sh, 1), 0.0)
        pre = pre + jnp.where(lane >= sh, pltpu.roll(pre, sh, 1), 0.0)
        sh *= 2
    total = suf + pre - x
    r_run = r_ref[0:1, :]
    bias_all = (suf - x) + _dot_sel_l(ug_ref[...], total) + r_run
    r_ref[...] = jnp.broadcast_to(r_run + jnp.sum(total, axis=0, keepdims=True), r_ref.shape)

    qm = qm_ref[...]
    valid = (lax.broadcasted_iota(I32, (rows, flat), 0) // t
             == lax.broadcasted_iota(I32, (rows, flat), 1) % FOX_HEADS)
    scores = []
    for jj in range(g):
        kf = ck[jj][0].reshape(flat, dh).astype(BF16)
        s = _dot_nt(qm, kf) + bias_all[jj:jj + 1, :]
        scores.append(jnp.where(valid, s, -jnp.inf))
    m = m_ref[...]
    m_new = m
    for s in scores:
        m_new = jnp.maximum(m_new, jnp.max(s, axis=1, keepdims=True))
    alpha = jnp.exp(m - m_new)
    l = alpha * l_ref[...]
    acc = alpha * acc_ref[...]
    for jj, s in enumerate(scores):
        p = jnp.exp(s - m_new)
        l = l + jnp.sum(p, axis=1, keepdims=True)
        acc = acc + _dot(p.astype(BF16), cv[jj][0].reshape(flat, dh).astype(BF16))
    m_ref[...] = m_new
    l_ref[...] = l
    acc_ref[...] = acc

    @pl.when(gi == pl.num_programs(1) - 1)
    def _():
        o = acc / l
        o_ref[0] = jnp.concatenate([o[h * t:(h + 1) * t] for h in range(FOX_HEADS)], axis=1).astype(BF16)


def _fox_sample(q, kb_new, vb_new, lft_new, cache_k, cache_v, cache_lf_flat, page_table):
    db, t, w = q.shape
    n_pages = page_table.shape[1]
    g = min(PAGES_PER_STEP, n_pages)
    assert n_pages % g == 0
    ng = n_pages // g
    rows = FOX_HEADS * t
    flat = PAGE_SIZE * FOX_HEADS
    t8 = jnp.asarray(np.triu(np.ones((t, t), np.float32)), BF16)
    eh = jnp.asarray(np.repeat(np.eye(FOX_HEADS, dtype=np.float32), t, axis=0), BF16)
    ug = jnp.asarray(np.triu(np.ones((g, g), np.float32), 1), BF16)
    tok = lambda wd: pl.BlockSpec((1, t, wd), lambda b, gi, pt: (b, 0, 0))
    full = lambda a: pl.BlockSpec(a.shape, lambda b, gi, pt: (0,) * a.ndim)

    def page(shape, jj):
        zeros = (0,) * (len(shape) - 1)
        return pl.BlockSpec(shape, lambda b, gi, pt: (pt[b * n_pages + (ng - 1 - gi) * g + jj],) + zeros)

    kv_page = (1, PAGE_SIZE, FOX_HEADS, FOX_HEAD_DIM)
    in_specs = ([tok(w), tok(w), tok(w), pl.BlockSpec((1, FOX_HEADS, t), lambda b, gi, pt: (b, 0, 0)),
                 full(t8), full(eh), full(ug)]
                + [page(kv_page, jj) for jj in range(g)]
                + [page(kv_page, jj) for jj in range(g)]
                + [page((1, 1, flat), jj) for jj in range(g)])
    grid_spec = pltpu.PrefetchScalarGridSpec(
        num_scalar_prefetch=1, grid=(db, ng), in_specs=in_specs,
        out_specs=pl.BlockSpec((1, t, w), lambda b, gi, pt: (b, 0, 0)),
        scratch_shapes=[pltpu.VMEM((rows, FOX_HEAD_DIM), BF16), pltpu.VMEM((rows, 1), F32),
                        pltpu.VMEM((rows, 1), F32), pltpu.VMEM((rows, FOX_HEAD_DIM), F32),
                        pltpu.VMEM((SUBLANES, flat), F32)])
    return pl.pallas_call(
        functools.partial(_fox_sample_kernel, g=g, t=t), grid_spec=grid_spec,
        out_shape=jax.ShapeDtypeStruct((db, t, w), BF16),
        compiler_params=_cparams(("parallel", "arbitrary")), name="fox_sample",
    )(page_table.reshape(-1), q, kb_new, vb_new, lft_new, t8, eh, ug,
      *([cache_k] * g), *([cache_v] * g), *([cache_lf_flat] * g))


def _gla_tables(c):
    idx = np.arange(c)
    ws = [np.tril(np.ones((c, c), np.float32))]
    ms = [np.eye(c, dtype=np.float32)]
    sz = 2
    while sz <= c:
        seg = idx // sz
        upper = (idx % sz) >= sz // 2
        mid = seg * sz + sz // 2 - 1
        j = idx[None, :]
        r = idx[:, None]
        w_up = (j > mid[:, None]) & (j <= r)
        w_lo = (j > r) & (j <= mid[:, None])
        ws.append(np.where(upper[:, None], w_up, w_lo).astype(np.float32))
        ms.append(((seg[:, None] == seg[None, :]) & upper[:, None] & ~upper[None, :]).astype(np.float32))
        sz *= 2
    return (jnp.asarray(np.concatenate(ws, axis=0), BF16),
            jnp.asarray(np.stack([np.concatenate([m, m], axis=0) for m in ms]), F32))


def _gla_kernel(q_ref, k_ref, v_ref, la_ref, s0_ref, wall_ref, msk_ref, o_ref, sfin_ref, state_ref,
                *, c, n_levels, nb, shared_s0):
    ci = pl.program_id(1)

    @pl.when(ci == 0)
    def _():
        for bi in range(nb):
            state_ref[bi] = s0_ref[0 if shared_s0 else bi]

    lane = lax.broadcasted_iota(I32, (c, LANES), 1)
    r128 = lax.broadcasted_iota(I32, (LANES, LANES), 0)
    half = GLA_KEY_DIM
    in_a = lane < half

    def stack_heads(x):
        return jnp.concatenate([jnp.where(in_a, x, 0.0), jnp.where(in_a, 0.0, x)], axis=0).astype(BF16)

    for bi in range(nb):
        la3 = _split3(la_ref[bi])
        seg = _dot(wall_ref[...], jnp.concatenate(la3, axis=1))
        seg = seg[:, 2 * GLA_K:] + seg[:, GLA_K:2 * GLA_K] + seg[:, :GLA_K]
        b_all = seg[:c]
        e_all = jnp.exp(seg[c:])
        for p in range(GLA_HEADS // 2):
            ksl = slice(p * LANES, (p + 1) * LANES)
            q = q_ref[bi, :, ksl]
            k = k_ref[bi, :, ksl]
            v16 = v_ref[bi, :, 2 * p * LANES:(2 * p + 2) * LANES].astype(BF16)
            sab = msk_ref[0] * _dot_nt(stack_heads(q), k.astype(BF16))
            for lv in range(n_levels):
                e = e_all[lv * c:(lv + 1) * c, ksl]
                sab = sab + msk_ref[lv + 1] * _dot_nt(stack_heads(q * e), (k * e).astype(BF16))
            st = state_ref[bi, p]
            b = b_all[:, ksl]
            inter = _dot(stack_heads(q * jnp.exp(b)), st.astype(BF16))
            s16 = sab.astype(BF16)
            o_ref[bi, :, 2 * p * LANES:(2 * p + 1) * LANES] = inter[:c] + _dot(s16[:c], v16[:, :LANES])
            o_ref[bi, :, (2 * p + 1) * LANES:(2 * p + 2) * LANES] = inter[c:] + _dot(s16[c:], v16[:, LANES:])
            b_end = b[c - 1:c, :]
            kdec = (k * jnp.exp(b_end - b)).astype(BF16)
            upd = _dot_tn(kdec, v16)
            upd = jnp.where(r128 < half, upd[:, :LANES], upd[:, LANES:])
            state_ref[bi, p] = _col_from_row(jnp.exp(b_end), LANES) * st + upd

    @pl.when(ci == pl.num_programs(1) - 1)
    def _():
        sfin_ref[...] = state_ref[...]


def _gla(qg, kg, vg, la, s0, c):
    b, l, _ = qg.shape
    assert l % c == 0
    nb = GLA_SEQS_PER_STEP
    while b % nb:
        nb //= 2
    wall, msk = _gla_tables(c)
    n_levels = msk.shape[0] - 1
    shared_s0 = s0.shape[0] != b
    st_shape = (GLA_HEADS // 2, LANES, GLA_VAL_DIM)
    s0_spec = (pl.BlockSpec((1,) + st_shape, lambda bi, ci: (0, 0, 0, 0)) if shared_s0
               else pl.BlockSpec((nb,) + st_shape, lambda bi, ci: (bi, 0, 0, 0)))
    tok = lambda wd: pl.BlockSpec((nb, c, wd), lambda bi, ci: (bi, ci, 0))
    full = lambda a: pl.BlockSpec(a.shape, lambda bi, ci: (0,) * a.ndim)
    return pl.pallas_call(
        functools.partial(_gla_kernel, c=c, n_levels=n_levels, nb=nb, shared_s0=shared_s0),
        grid=(b // nb, l // c),
        in_specs=[tok(GLA_K), tok(GLA_K), tok(GLA_V), tok(GLA_K), s0_spec, full(wall), full(msk)],
        out_specs=(tok(GLA_V), pl.BlockSpec((nb,) + st_shape, lambda bi, ci: (bi, 0, 0, 0))),
        out_shape=(jax.ShapeDtypeStruct((b, l, GLA_V), F32), jax.ShapeDtypeStruct((b,) + st_shape, F32)),
        scratch_shapes=[pltpu.VMEM((nb,) + st_shape, F32)],
        compiler_params=_cparams(("parallel", "arbitrary")), name="gla",
    )(qg, kg, vg, la, s0, wall, msk)


def _merge_kernel(x_ref, of_ref, og_ref, gmix_ref, wrg_ref, wfo_ref, wgo_ref, wo_ref, gng_ref, gffn_ref,
                  wrt_ref, brt_ref, x2_ref, h2_ref, eid_ref, wsel_ref):
    x = x_ref[...]
    tm, d = x.shape
    h = _rmsnorm(x, gmix_ref[...]).astype(BF16)
    z = _dot(h, wrg_ref[...])
    r = z[:, :GLA_V]
    gate_f = z[:, GLA_V:GLA_V + d]
    gate_g = z[:, GLA_V + d:]
    og = og_ref[...]
    parts = []
    for hh in range(GLA_HEADS):
        blk = og[:, hh * GLA_VAL_DIM:(hh + 1) * GLA_VAL_DIM]
        parts.append(blk * lax.rsqrt(jnp.mean(blk * blk, axis=-1, keepdims=True) + EPS))
    ogn = jnp.concatenate(parts, axis=1)
    ogs = (ogn * gng_ref[...] * (r * _sigmoid(r))).astype(BF16)
    branch_f = _dot(of_ref[...], wfo_ref[...])
    branch_g = _dot(ogs, wgo_ref[...])
    merged = _sigmoid(gate_f) * branch_f + _sigmoid(gate_g) * branch_g
    x2 = x + _dot(merged.astype(BF16), wo_ref[...])
    x2_ref[...] = x2
    h2 = _rmsnorm(x2, gffn_ref[...])
    h2_ref[...] = h2.astype(BF16)

    a = _split3(h2)
    wr = (wrt_ref[0], wrt_ref[1], wrt_ref[2])
    lt = (_dot_nt(wr[1], a[1]) + _dot_nt(wr[0], a[2]) + _dot_nt(wr[2], a[0])
          + _dot_nt(wr[0], a[1]) + _dot_nt(wr[1], a[0]) + _dot_nt(wr[0], a[0])) + brt_ref[...]
    row8 = lax.broadcasted_iota(I32, (SUBLANES, tm), 0)
    gl = jnp.where(row8 < N_GROUPS, lt[:SUBLANES], -jnp.inf)
    gmax = jnp.max(gl, axis=0, keepdims=True)
    g_idx = jnp.min(jnp.where(gl == gmax, row8, SUBLANES), axis=0, keepdims=True)
    g_sel = 1.0 / jnp.sum(jnp.exp(gl - gmax), axis=0, keepdims=True)
    rowe = lax.broadcasted_iota(I32, (N_EXPERTS, tm), 0)
    e1 = jnp.where(rowe // EXPERTS_PER_GROUP == g_idx, lt[SUBLANES:], -jnp.inf)
    v1 = jnp.max(e1, axis=0, keepdims=True)
    i1 = jnp.min(jnp.where(e1 == v1, rowe, N_EXPERTS), axis=0, keepdims=True)
    e2 = jnp.where(rowe == i1, -jnp.inf, e1)
    v2 = jnp.max(e2, axis=0, keepdims=True)
    i2 = jnp.min(jnp.where(e2 == v2, rowe, N_EXPERTS), axis=0, keepdims=True)
    tt = jnp.exp(v2 - v1)
    eid_ref[...] = jnp.concatenate([i1, i2], axis=0)
    wsel_ref[...] = jnp.concatenate([g_sel / (1.0 + tt), g_sel * tt / (1.0 + tt)], axis=0)


def _merge(x, o_fox, o_gla, mw, tm):
    n, d = x.shape
    assert n % tm == 0
    row = lambda wd: pl.BlockSpec((tm, wd), lambda i: (i, 0))
    full = lambda a: pl.BlockSpec(a.shape, lambda i: (0,) * a.ndim)
    ws = (mw["g_mix"], mw["w_rg"], mw["w_fo"], mw["w_go"], mw["w_o"], mw["g_gla"], mw["g_ffn"],
          mw["w_rt"], mw["b_rt"])
    return pl.pallas_call(
        _merge_kernel, grid=(n // tm,),
        in_specs=[row(d), row(FOX_WIDTH), row(GLA_V)] + [full(a) for a in ws],
        out_specs=(row(d), row(d), pl.BlockSpec((2, tm), lambda i: (0, i)), pl.BlockSpec((2, tm), lambda i: (0, i))),
        out_shape=(jax.ShapeDtypeStruct((n, d), F32), jax.ShapeDtypeStruct((n, d), BF16),
                   jax.ShapeDtypeStruct((2, n), I32), jax.ShapeDtypeStruct((2, n), F32)),
        compiler_params=_cparams(("parallel",)), name="merge",
    )(x, o_fox, o_gla, *ws)


def _run_sizes(max_rows):
    sizes = []
    s = RUN_ALIGN
    while s <= max_rows:
        sizes.append(s)
        s *= 2
    return tuple(reversed(sizes))


def _for_each_run_piece(n_rows, a_off, b_off, sizes, fn):
    for size in sizes:
        bit = (n_rows & size) != 0

        @pl.when(bit)
        def _(a_off=a_off, b_off=b_off, size=size):
            fn(pl.multiple_of(a_off, RUN_ALIGN), pl.multiple_of(b_off, RUN_ALIGN), size)

        inc = jnp.where(bit, size, 0)
        a_off = a_off + inc
        b_off = b_off + inc


def _dispatch_kernel(p_ref, loc_ref, dst_ref, toff_ref, tlen_ref, nu_ref,
                     h2_ref, eid_ref, wsel_ref, locf_ref, triu_ref,
                     xs_ref, pos_ref, xloc_ref, zero_ref, sem, *, nt, bt, sl, d, n_tiles):
    i = pl.program_id(0)
    sizes = _run_sizes(bt)
    t = zero_ref.shape[0]

    def run_copy(lo, do, size):
        return pltpu.make_async_copy(xloc_ref.at[pl.ds(lo, size)], xs_ref.at[pl.ds(do, size)], sem)

    @pl.when(i < nt)
    def _():
        eid = eid_ref[...]
        wsel = wsel_ref[...]
        rowe = lax.broadcasted_iota(I32, (N_EXPERTS, bt), 0)
        oh0 = jnp.where(rowe == eid[0:1], 1.0, 0.0)
        oh1 = jnp.where(rowe == eid[1:2], 1.0, 0.0)
        pref0 = _dot(oh0.astype(BF16), triu_ref[...])
        pref1 = _dot(oh1.astype(BF16), triu_ref[...])
        n0 = jnp.sum(oh0, axis=1, keepdims=True)
        loc = locf_ref[0]
        pos0 = jnp.sum(oh0 * (loc + pref0), axis=0, keepdims=True)
        pos1 = jnp.sum(oh1 * (loc + n0 + pref1), axis=0, keepdims=True)
        pos_ref[...] = jnp.concatenate([pos0, pos1], axis=0)
        srow = lax.broadcasted_iota(I32, (sl, bt), 0).astype(F32)
        m0 = srow == pos0
        m1 = srow == pos1
        perm = jnp.where(m0, 1.0, jnp.where(m1, 1.0, 0.0)).astype(BF16)
        wslot = jnp.sum(jnp.where(m0, wsel[0:1], jnp.where(m1, wsel[1:2], 0.0)), axis=1, keepdims=True)
        xloc_ref[:, :d] = _dot(perm, h2_ref[...])
        xloc_ref[:, d:] = jnp.broadcast_to(wslot, (sl, W_COLS))

        def runs(act):
            def body(e, carry):
                idx = i * N_EXPERTS + e
                _for_each_run_piece(p_ref[idx], loc_ref[idx], dst_ref[idx], sizes,
                                    lambda lo, do, size: act(run_copy(lo, do, size)))
                return carry
            lax.fori_loop(0, N_EXPERTS, body, 0)

        runs(lambda cp: cp.start())
        runs(lambda cp: cp.wait())

    @pl.when(i == nt)
    def _():
        zero_ref[...] = jnp.zeros_like(zero_ref)
        tsizes = _run_sizes(t // 2)

        def tail_copy(zo, do, size):
            return pltpu.make_async_copy(zero_ref.at[pl.ds(0, size)], xs_ref.at[pl.ds(do, size)], sem)

        def tails(act):
            def body(e, carry):
                _for_each_run_piece(tlen_ref[e], jnp.int32(0), toff_ref[e], tsizes,
                                    lambda zo, do, size: act(tail_copy(zo, do, size)))
                return carry
            lax.fori_loop(0, N_EXPERTS, body, 0)

            def unused(j, carry):
                act(tail_copy(0, pl.multiple_of(j * t, t), t))
                return carry
            lax.fori_loop(nu_ref[0], n_tiles, unused, 0)

        tails(lambda cp: cp.start())
        tails(lambda cp: cp.wait())


def _ffn_kernel(te_ref, nu_ref, xs_ref, wg_ref, wu_ref, wd_ref, y_ref, *, d):
    i = pl.program_id(0)

    @pl.when(i < nu_ref[0])
    def _():
        x = xs_ref[:, :d].astype(BF16)
        g = _dot(x, wg_ref[0])
        u = _dot(x, wu_ref[0])
        a = (g * _sigmoid(g) * u).astype(BF16)
        y = _dot(a, wd_ref[0])
        wrow = xs_ref[:, d:]
        y_ref[...] = y * jnp.concatenate([wrow] * (d // W_COLS), axis=1)

    @pl.when(i >= nu_ref[0])
    def _():
        y_ref[...] = jnp.zeros_like(y_ref)


def _combine_kernel(p_ref, loc_ref, dst_ref, pos_ref, x2_ref, gfin_ref, y_hbm, out_ref, yloc_ref, sem,
                    *, bt, sl):
    i = pl.program_id(0)
    sizes = _run_sizes(bt)
    yloc_ref[...] = jnp.zeros_like(yloc_ref)

    def run_copy(lo, do, size):
        return pltpu.make_async_copy(y_hbm.at[pl.ds(do, size)], yloc_ref.at[pl.ds(lo, size)], sem)

    def runs(act):
        def body(e, carry):
            idx = i * N_EXPERTS + e
            _for_each_run_piece(p_ref[idx], loc_ref[idx], dst_ref[idx], sizes,
                                lambda lo, do, size: act(run_copy(lo, do, size)))
            return carry
        lax.fori_loop(0, N_EXPERTS, body, 0)

    runs(lambda cp: cp.start())
    runs(lambda cp: cp.wait())
    pos = pos_ref[...]
    pos0 = _col_from_row(pos[0:1], bt)
    pos1 = _col_from_row(pos[1:2], bt)
    scol = lax.broadcasted_iota(I32, (bt, sl), 1).astype(F32)
    perm_t = jnp.where(scol == pos0, 1.0, jnp.where(scol == pos1, 1.0, 0.0)).astype(BF16)
    moe = _dot(perm_t, yloc_ref[...].astype(BF16))
    out_ref[...] = _rmsnorm(x2_ref[...] + moe, gfin_ref[...])


def _moe_prompt(x2, h2, eid, wsel, wg, wu, wd, g_fin):
    n, d = x2.shape
    bt, t = MOE_TOKENS, MOE_ROWS
    assert n % bt == 0
    nt = n // bt
    sl = -(-(2 * bt + N_EXPERTS * (RUN_ALIGN - 1)) // LANES) * LANES
    onehot = eid.reshape(2, nt, bt)[..., None] == jnp.arange(N_EXPERTS, dtype=I32)
    cnt = jnp.sum(onehot, axis=(0, 2), dtype=I32)
    p = (cnt + RUN_ALIGN - 1) // RUN_ALIGN * RUN_ALIGN
    loc = jnp.cumsum(p, axis=1) - p
    tot = jnp.sum(p, axis=0)
    reg = (tot + t - 1) // t * t
    reg_end = jnp.cumsum(reg)
    reg_start = reg_end - reg
    dst = reg_start[None, :] + jnp.cumsum(p, axis=0) - p
    n_tiles = (2 * n + nt * N_EXPERTS * (RUN_ALIGN - 1) + N_EXPERTS * (t - 1)) // t + 1
    s_max = n_tiles * t
    n_used = (reg_end[-1] // t).astype(I32).reshape(1)
    tile_start = jnp.arange(n_tiles, dtype=I32) * t
    tile_expert = jnp.minimum(jnp.sum(reg_end[None, :] <= tile_start[:, None], axis=1, dtype=I32), N_EXPERTS - 1)
    p_f, loc_f, dst_f = p.reshape(-1), loc.reshape(-1), dst.reshape(-1).astype(I32)
    locf = loc.astype(F32).reshape(nt, N_EXPERTS, 1)
    triu = jnp.asarray(np.triu(np.ones((bt, bt), np.float32), 1), BF16)

    last = nt - 1
    xs, pos = pl.pallas_call(
        functools.partial(_dispatch_kernel, nt=nt, bt=bt, sl=sl, d=d, n_tiles=n_tiles),
        grid_spec=pltpu.PrefetchScalarGridSpec(
            num_scalar_prefetch=6, grid=(nt + 1,),
            in_specs=[pl.BlockSpec((bt, d), lambda i, *_: (jnp.minimum(i, last), 0)),
                      pl.BlockSpec((2, bt), lambda i, *_: (0, jnp.minimum(i, last))),
                      pl.BlockSpec((2, bt), lambda i, *_: (0, jnp.minimum(i, last))),
                      pl.BlockSpec((1, N_EXPERTS, 1), lambda i, *_: (jnp.minimum(i, last), 0, 0)),
                      pl.BlockSpec((bt, bt), lambda i, *_: (0, 0))],
            out_specs=(pl.BlockSpec(memory_space=pl.ANY),
                       pl.BlockSpec((2, bt), lambda i, *_: (0, jnp.minimum(i, last)))),
            scratch_shapes=[pltpu.VMEM((sl, d + W_COLS), F32), pltpu.VMEM((t, d + W_COLS), F32),
                            pltpu.SemaphoreType.DMA(())]),
        out_shape=(jax.ShapeDtypeStruct((s_max, d + W_COLS), F32), jax.ShapeDtypeStruct((2, n), F32)),
        compiler_params=_cparams(("arbitrary",)), name="moe_dispatch",
    )(p_f, loc_f, dst_f, (reg_start + tot).astype(I32), (reg - tot).astype(I32), n_used, h2, eid, wsel, locf, triu)

    de = wg.shape[2]
    used = lambda i, te, nu: jnp.minimum(i, nu[0] - 1)
    y = pl.pallas_call(
        functools.partial(_ffn_kernel, d=d),
        grid_spec=pltpu.PrefetchScalarGridSpec(
            num_scalar_prefetch=2, grid=(n_tiles,),
            in_specs=[pl.BlockSpec((t, d + W_COLS), lambda i, te, nu: (used(i, te, nu), 0)),
                      pl.BlockSpec((1, d, de), lambda i, te, nu: (te[used(i, te, nu)], 0, 0)),
                      pl.BlockSpec((1, d, de), lambda i, te, nu: (te[used(i, te, nu)], 0, 0)),
                      pl.BlockSpec((1, de, d), lambda i, te, nu: (te[used(i, te, nu)], 0, 0))],
            out_specs=pl.BlockSpec((t, d), lambda i, te, nu: (i, 0))),
        out_shape=jax.ShapeDtypeStruct((s_max, d), F32),
        compiler_params=_cparams(("arbitrary",)), name="moe_ffn",
    )(tile_expert, n_used, xs, wg, wu, wd)

    return pl.pallas_call(
        functools.partial(_combine_kernel, bt=bt, sl=sl),
        grid_spec=pltpu.PrefetchScalarGridSpec(
            num_scalar_prefetch=3, grid=(nt,),
            in_specs=[pl.BlockSpec((2, bt), lambda i, *_: (0, i)),
                      pl.BlockSpec((bt, d), lambda i, *_: (i, 0)),
                      pl.BlockSpec((1, d), lambda i, *_: (0, 0)),
                      pl.BlockSpec(memory_space=pl.ANY)],
            out_specs=pl.BlockSpec((bt, d), lambda i, *_: (i, 0)),
            scratch_shapes=[pltpu.VMEM((sl, d), F32), pltpu.SemaphoreType.DMA(())]),
        out_shape=jax.ShapeDtypeStruct((n, d), F32),
        compiler_params=_cparams(("arbitrary",)), name="moe_combine",
    )(p_f, loc_f, dst_f, pos, x2, g_fin, y)


def _moe_dense_kernel(x2_ref, h2_ref, eid_ref, wsel_ref, gfin_ref, wg_ref, wu_ref, wd_ref, out_ref, acc_ref):
    e = pl.program_id(0)
    n = x2_ref.shape[0]

    @pl.when(e == 0)
    def _():
        acc_ref[...] = jnp.zeros_like(acc_ref)

    eid = eid_ref[...]
    wsel = wsel_ref[...]
    gate_row = jnp.where(eid[0:1] == e, wsel[0:1], 0.0) + jnp.where(eid[1:2] == e, wsel[1:2], 0.0)
    gate = _col_from_row(gate_row, n)
    x = h2_ref[...]
    g = _dot(x, wg_ref[0])
    u = _dot(x, wu_ref[0])
    a = (g * _sigmoid(g) * u).astype(BF16)
    acc_ref[...] += gate * _dot(a, wd_ref[0])

    @pl.when(e == pl.num_programs(0) - 1)
    def _():
        out_ref[...] = _rmsnorm(x2_ref[...] + acc_ref[...], gfin_ref[...])


def _moe_dense(x2, h2, eid, wsel, wg, wu, wd, g_fin):
    n, d = x2.shape
    de = wg.shape[2]
    full = lambda a: pl.BlockSpec(a.shape, lambda e: (0,) * a.ndim)
    return pl.pallas_call(
        _moe_dense_kernel, grid=(wg.shape[0],),
        in_specs=[full(x2), full(h2), full(eid), full(wsel), full(g_fin),
                  pl.BlockSpec((1, d, de), lambda e: (e, 0, 0)),
                  pl.BlockSpec((1, d, de), lambda e: (e, 0, 0)),
                  pl.BlockSpec((1, de, d), lambda e: (e, 0, 0))],
        out_specs=pl.BlockSpec((n, d), lambda e: (0, 0)),
        out_shape=jax.ShapeDtypeStruct((n, d), F32),
        scratch_shapes=[pltpu.VMEM((n, d), F32)],
        compiler_params=_cparams(("arbitrary",)), name="moe_dense",
    )(x2, h2, eid, wsel, g_fin, wg, wu, wd)


def _layer_weights(norm_mix_g, w_in, fox_f_bias, gla_w_a2, gla_b_a, gla_norm_g, w_fox_out, w_gla_out, w_o,
                   norm_ffn_g, w_group_router, b_group_router, w_expert_router, b_expert_router):
    d = w_in.shape[0]
    c = 0
    parts = {}
    for name, width in (("q", FOX_WIDTH), ("k", FOX_WIDTH), ("v", FOX_WIDTH), ("ff", FOX_HEADS),
                        ("qg", GLA_K), ("kg", GLA_K), ("vg", GLA_V), ("lr", GLA_RANK), ("r", GLA_V),
                        ("gf", d), ("gg", d)):
        parts[name] = w_in[:, c:c + width]
        c += width
    assert c == w_in.shape[1]
    g_mix = norm_mix_g.reshape(1, d)
    w_fft = jnp.zeros((2 * SUBLANES, d), F32).at[:FOX_HEADS].set(parts["ff"].T).astype(BF16)
    pw = dict(
        g_mix=g_mix,
        w_qkv=jnp.concatenate([parts["q"], parts["k"], parts["v"]], axis=1).astype(BF16),
        w_fft=w_fft,
        w_gla=jnp.concatenate([parts["qg"], parts["kg"], parts["vg"]], axis=1).astype(BF16),
        w_lr=parts["lr"].astype(BF16),
        w_a2=gla_w_a2.astype(BF16),
        b_a=gla_b_a.reshape(1, GLA_K),
        f_bias=fox_f_bias.reshape(FOX_HEADS, 1),
    )
    w_rt = jnp.zeros((SUBLANES + N_EXPERTS, d), F32)
    w_rt = w_rt.at[:N_GROUPS].set(w_group_router.T).at[SUBLANES:].set(w_expert_router.T)
    b_rt = jnp.zeros((SUBLANES + N_EXPERTS, 1), F32)
    b_rt = b_rt.at[:N_GROUPS, 0].set(b_group_router).at[SUBLANES:, 0].set(b_expert_router)
    mw = dict(
        g_mix=g_mix,
        w_rg=jnp.concatenate([parts["r"], parts["gf"], parts["gg"]], axis=1).astype(BF16),
        w_fo=w_fox_out.astype(BF16), w_go=w_gla_out.astype(BF16), w_o=w_o.astype(BF16),
        g_gla=gla_norm_g.reshape(1, GLA_V), g_ffn=norm_ffn_g.reshape(1, d),
        w_rt=jnp.stack(_split3(w_rt)), b_rt=b_rt,
    )
    return pw, mw


def kernel(x_prompt, x_sample, cache_k, cache_v, cache_log_f, state_gla, page_table, meta_tokens, norm_mix_g,
           w_in, fox_f_bias, gla_w_a2, gla_b_a, gla_norm_g, w_fox_out, w_gla_out, w_o, norm_ffn_g,
           w_group_router, b_group_router, w_expert_router, b_expert_router, w_expert_gate, w_expert_up,
           w_expert_down, norm_final_g):
    depth = w_in.shape[0]
    assert depth == 1, "meta rows are only carried through one layer"
    b, seq, d = x_prompt.shape
    db, t_dec, _ = x_sample.shape
    n_phys = cache_k.shape[1]
    pw, mw = _layer_weights(norm_mix_g[0], w_in[0], fox_f_bias[0], gla_w_a2[0], gla_b_a[0], gla_norm_g[0],
                            w_fox_out[0], w_gla_out[0], w_o[0], norm_ffn_g[0], w_group_router[0],
                            b_group_router[0], w_expert_router[0], b_expert_router[0])
    wg = w_expert_gate[0].astype(BF16)
    wu = w_expert_up[0].astype(BF16)
    wd = w_expert_down[0].astype(BF16)
    g_fin = norm_final_g.reshape(1, d)

    (_, k_m, v_m, kb_m, vb_m, lft_m, qg_m, kg_m, vg_m, la_m) = _project(
        meta_tokens.reshape(1, N_META, d), N_META, pw)
    zero_state = jnp.zeros((1, GLA_HEADS // 2, LANES, GLA_VAL_DIM), F32)
    _, s_meta = _gla(qg_m, kg_m, vg_m, la_m, zero_state, N_META)

    q, k, v, kb, vb, lft, qg, kg, vg, la = _project(x_prompt, min(PROJ_TILE, seq), pw)
    o_fox = _fox_prompt(q, kb, vb, lft, kb_m[0], vb_m[0], lft_m[0], min(ATTN_TILE, seq))
    o_gla, s_gla_p = _gla(qg, kg, vg, la, s_meta, min(GLA_CHUNK, seq))
    n = b * seq
    x2, h2, eid, wsel = _merge(x_prompt.reshape(n, d), o_fox.reshape(n, FOX_WIDTH), o_gla.reshape(n, GLA_V),
                               mw, min(MERGE_TILE, n))
    y_prompt = _moe_prompt(x2, h2, eid, wsel, wg, wu, wd, g_fin).reshape(b, seq, d)

    ns = db * t_dec
    qs, ks, vs, kbs, vbs, lfts, qgs, kgs, vgs, las = _project(x_sample.reshape(1, ns, d), ns, pw)
    tok = lambda a: a.reshape(db, t_dec, a.shape[-1])
    lft_s = lfts.reshape(FOX_HEADS, db, t_dec).transpose(1, 0, 2)
    cache_lf_flat = cache_log_f[0].reshape(n_phys, 1, PAGE_SIZE * FOX_HEADS)
    o_fox_s = _fox_sample(tok(qs), tok(kbs), tok(vbs), lft_s, cache_k[0], cache_v[0], cache_lf_flat, page_table)
    s0 = state_gla[0].reshape(db, GLA_HEADS // 2, LANES, GLA_VAL_DIM)
    o_gla_s, s_gla_s = _gla(tok(qgs), tok(kgs), tok(vgs), tok(las), s0, t_dec)
    x2s, h2s, eids, wsels = _merge(x_sample.reshape(ns, d), o_fox_s.reshape(ns, FOX_WIDTH),
                                   o_gla_s.reshape(ns, GLA_V), mw, ns)
    y_sample = _moe_dense(x2s, h2s, eids, wsels, wg, wu, wd, g_fin).reshape(db, t_dec, d)

    heads = lambda a: a.reshape(a.shape[:-1] + (FOX_HEADS, FOX_HEAD_DIM))
    with_meta = lambda m, r: jnp.concatenate([jnp.broadcast_to(m, (b,) + m.shape[1:]), r], axis=1)
    new_k_prompt = heads(with_meta(k_m, k))[None]
    new_v_prompt = heads(with_meta(v_m, v))[None]
    new_log_f_prompt = with_meta(lft_m.transpose(0, 2, 1), lft.transpose(0, 2, 1))[None]
    new_gla_prompt = s_gla_p.reshape(1, b, GLA_HEADS, GLA_KEY_DIM, GLA_VAL_DIM)
    new_k_sample = heads(tok(ks))[None]
    new_v_sample = heads(tok(vs))[None]
    new_log_f_sample = lfts[0].T.reshape(1, db, t_dec, FOX_HEADS)
    new_gla_sample = s_gla_s.reshape(1, db, GLA_HEADS, GLA_KEY_DIM, GLA_VAL_DIM)
    return (y_prompt, y_sample, new_k_prompt, new_v_prompt, new_log_f_prompt, new_gla_prompt,
            new_k_sample, new_v_sample, new_log_f_sample, new_gla_sample)
```

```python
import functools

import numpy as np
import jax
import jax.numpy as jnp
from jax import lax
from jax.experimental import pallas as pl
from jax.experimental.pallas import tpu as pltpu

F32 = jnp.float32
BF16 = jnp.bfloat16
I32 = jnp.int32
EPS = 1e-6
LOG2E = 1.4426950408889634
N_META = 16
FOX_HEADS = 8
FOX_HEAD_DIM = 64
FOX_WIDTH = FOX_HEADS * FOX_HEAD_DIM
GLA_HEADS = 4
GLA_KEY_DIM = 64
GLA_VAL_DIM = 128
GLA_K = GLA_HEADS * GLA_KEY_DIM
GLA_V = GLA_HEADS * GLA_VAL_DIM
GLA_RANK = 16
GLA_TAU = 16.0
GLA_CHUNK = 128
N_GROUPS = 4
EXPERTS_PER_GROUP = 8
N_EXPERTS = N_GROUPS * EXPERTS_PER_GROUP
PAGE_SIZE = 128

LANES = 128
SUBLANES = 8
VMEM_LIMIT = 56 * 1024 * 1024

PROJ_TILE = 512
ATTN_TILE = 512
MERGE_TILE = 512
MOE_TOKENS = 512
MOE_ROWS = 256
RUN_ALIGN = SUBLANES
PAGES_PER_STEP = 16
GLA_SEQS_PER_STEP = 4
W_COLS = LANES


def _cparams(sem):
    return pltpu.CompilerParams(dimension_semantics=sem, vmem_limit_bytes=VMEM_LIMIT)


def _dot(a, b):
    return jnp.dot(a, b, preferred_element_type=F32)


def _dot_nt(a, b):
    return lax.dot_general(a, b, (((1,), (1,)), ((), ())), preferred_element_type=F32)


def _dot_tn(a, b):
    return lax.dot_general(a, b, (((0,), (0,)), ((), ())), preferred_element_type=F32)


def _split3(a):
    hi = a.astype(BF16)
    r1 = a - hi.astype(F32)
    mid = r1.astype(BF16)
    lo = (r1 - mid.astype(F32)).astype(BF16)
    return hi, mid, lo


def _dot_sel_l(m, a):
    hi, mid, lo = _split3(a)
    return _dot(m, lo) + _dot(m, mid) + _dot(m, hi)


def _dot_sel_r(a, m):
    hi, mid, lo = _split3(a)
    return _dot(lo, m) + _dot(mid, m) + _dot(hi, m)


def _rmsnorm(x, g):
    return x * lax.rsqrt(jnp.mean(x * x, axis=-1, keepdims=True) + EPS) * g


def _log_sigmoid(x):
    return jnp.minimum(x, 0.0) - jnp.log1p(jnp.exp(-jnp.abs(x)))


def _sigmoid(x):
    return 1.0 / (1.0 + jnp.exp(-x))


def _col_from_row(row, n):
    r = lax.broadcasted_iota(I32, (n, n), 0)
    c = lax.broadcasted_iota(I32, (n, n), 1)
    return jnp.sum(jnp.where(r == c, jnp.broadcast_to(row, (n, n)), 0.0), axis=1, keepdims=True)


def _proj_kernel(x_ref, g_ref, wqkv_ref, wfft_ref, wgla_ref, wlr_ref, wa2_ref, ba_ref, fb_ref,
                 q_ref, k_ref, v_ref, kb_ref, vb_ref, lft_ref, qg_ref, kg_ref, vg_ref, la_ref):
    h = _rmsnorm(x_ref[0], g_ref[...]).astype(BF16)
    z = _dot(h, wqkv_ref[...])
    q_ref[0] = (z[:, :FOX_WIDTH] * (FOX_HEAD_DIM ** -0.5 * LOG2E)).astype(BF16)
    k = z[:, FOX_WIDTH:2 * FOX_WIDTH]
    v = z[:, 2 * FOX_WIDTH:]
    k_ref[0] = k
    v_ref[0] = v
    kb_ref[0] = k.astype(BF16)
    vb_ref[0] = v.astype(BF16)
    fft = _dot_nt(wfft_ref[...], h)[:FOX_HEADS]
    lft_ref[0] = _log_sigmoid(fft + fb_ref[...])
    zg = _dot(h, wgla_ref[...])
    qg_ref[0] = zg[:, :GLA_K] * GLA_KEY_DIM ** -0.5
    kg_ref[0] = zg[:, GLA_K:2 * GLA_K]
    vg_ref[0] = zg[:, 2 * GLA_K:]
    lr = _dot(h, wlr_ref[...])
    xa = _dot(lr.astype(BF16), wa2_ref[...]) + ba_ref[...]
    la_ref[0] = _log_sigmoid(xa) / GLA_TAU


def _project(x3, tm, pw):
    bx, lx, d = x3.shape
    assert lx % tm == 0
    grid = (bx, lx // tm)
    row = lambda w: pl.BlockSpec((1, tm, w), lambda b, i: (b, i, 0))
    full = lambda a: pl.BlockSpec(a.shape, lambda b, i: (0,) * a.ndim)
    ws = (pw["g_mix"], pw["w_qkv"], pw["w_fft"], pw["w_gla"], pw["w_lr"], pw["w_a2"], pw["b_a"], pw["f_bias"])
    out_shape = (
        jax.ShapeDtypeStruct((bx, lx, FOX_WIDTH), BF16),
        jax.ShapeDtypeStruct((bx, lx, FOX_WIDTH), F32),
        jax.ShapeDtypeStruct((bx, lx, FOX_WIDTH), F32),
        jax.ShapeDtypeStruct((bx, lx, FOX_WIDTH), BF16),
        jax.ShapeDtypeStruct((bx, lx, FOX_WIDTH), BF16),
        jax.ShapeDtypeStruct((bx, FOX_HEADS, lx), F32),
        jax.ShapeDtypeStruct((bx, lx, GLA_K), F32),
        jax.ShapeDtypeStruct((bx, lx, GLA_K), F32),
        jax.ShapeDtypeStruct((bx, lx, GLA_V), F32),
        jax.ShapeDtypeStruct((bx, lx, GLA_K), F32),
    )
    out_specs = (row(FOX_WIDTH), row(FOX_WIDTH), row(FOX_WIDTH), row(FOX_WIDTH), row(FOX_WIDTH),
                 pl.BlockSpec((1, FOX_HEADS, tm), lambda b, i: (b, 0, i)),
                 row(GLA_K), row(GLA_K), row(GLA_V), row(GLA_K))
    return pl.pallas_call(
        _proj_kernel, grid=grid,
        in_specs=[row(d)] + [full(a) for a in ws],
        out_specs=out_specs, out_shape=out_shape,
        compiler_params=_cparams(("parallel", "parallel")), name="proj",
    )(x3, *ws)


def _online_update(s, v, state):
    m, l, acc = state
    m_new = jnp.maximum(m, jnp.max(s, axis=1, keepdims=True))
    alpha = jnp.exp2(m - m_new)
    p = jnp.exp2(s - m_new)
    l = alpha * l + jnp.sum(p, axis=1, keepdims=True)
    acc = alpha * acc + _dot(p.astype(BF16), v)
    return m_new, l, acc


def _fox_prompt_kernel(q_ref, kb_ref, vb_ref, lft_ref, km_ref, vm_ref, lfm_ref, tri_ref, um_ref,
                       o_ref, c_ref, *, tq):
    qi = pl.program_id(1)
    n_blk = kb_ref.shape[1] // tq

    @pl.when(qi == 0)
    def _():
        carry = jnp.zeros((FOX_HEADS, 1), F32)
        for jb in range(n_blk):
            for js in range(tq // LANES):
                off = jb * tq + js * LANES
                cs = _dot_sel_r(lft_ref[0, :, off:off + LANES], tri_ref[...]) + carry
                c_ref[jb, :, js * LANES:(js + 1) * LANES] = cs * LOG2E
                carry = cs[:, LANES - 1:LANES]

    bias_meta = _dot_sel_r(lfm_ref[...], um_ref[...]) * LOG2E
    lane = lax.broadcasted_iota(I32, (tq, LANES), 1)
    row = lax.broadcasted_iota(I32, (tq, tq), 0)
    col = lax.broadcasted_iota(I32, (tq, tq), 1)
    in_a = lane < FOX_HEAD_DIM
    for hp in range(FOX_HEADS // 2):
        sl = slice(hp * LANES, (hp + 1) * LANES)
        ha, hb = 2 * hp, 2 * hp + 1
        qp = q_ref[0, :, sl].astype(F32)
        qa = jnp.where(in_a, qp, 0.0).astype(BF16)
        qb = jnp.where(in_a, 0.0, qp).astype(BF16)
        km = km_ref[:, sl]
        vm = vm_ref[:, sl]

        def first(qh, h, km=km, vm=vm):
            s = _dot_nt(qh, km) + bias_meta[h:h + 1, :]
            m = jnp.max(s, axis=1, keepdims=True)
            p = jnp.exp2(s - m)
            return m, jnp.sum(p, axis=1, keepdims=True), _dot(p.astype(BF16), vm)

        def block(j, states, masked, qa=qa, qb=qb, ha=ha, hb=hb, sl=sl):
            start = pl.multiple_of(j * tq, tq)
            kc = kb_ref[0, pl.ds(start, tq), sl]
            vc = vb_ref[0, pl.ds(start, tq), sl]
            out = []
            for qh, h, st in ((qa, ha, states[0]), (qb, hb, states[1])):
                s = _dot_nt(qh, kc) - c_ref[j, h:h + 1, :]
                if masked:
                    s = jnp.where(col <= row, s, -jnp.inf)
                out.append(_online_update(s, vc, st))
            return tuple(out)

        states = (first(qa, ha), first(qb, hb))
        states = lax.fori_loop(0, qi, lambda j, st: block(j, st, False), states)
        (_, la, acc_a), (_, lb, acc_b) = block(qi, states, True)
        o_ref[0, :, sl] = jnp.where(in_a, acc_a / la, acc_b / lb).astype(BF16)


def _fox_prompt(q, kb, vb, lft, kb_meta, vb_meta, lft_meta, tq):
    b, l, w = q.shape
    assert l % tq == 0
    n_blk = l // tq
    tri = jnp.asarray(np.triu(np.ones((LANES, LANES), np.float32)), BF16)
    um = jnp.asarray(np.tril(np.ones((N_META, N_META), np.float32), -1), BF16)
    full = lambda a: pl.BlockSpec(a.shape, lambda bi, qi: (0,) * a.ndim)
    return pl.pallas_call(
        functools.partial(_fox_prompt_kernel, tq=tq), grid=(b, n_blk),
        in_specs=[pl.BlockSpec((1, tq, w), lambda bi, qi: (bi, qi, 0)),
                  pl.BlockSpec((1, l, w), lambda bi, qi: (bi, 0, 0)),
                  pl.BlockSpec((1, l, w), lambda bi, qi: (bi, 0, 0)),
                  pl.BlockSpec((1, FOX_HEADS, l), lambda bi, qi: (bi, 0, 0)),
                  full(kb_meta), full(vb_meta), full(lft_meta), full(tri), full(um)],
        out_specs=pl.BlockSpec((1, tq, w), lambda bi, qi: (bi, qi, 0)),
        out_shape=jax.ShapeDtypeStruct((b, l, w), BF16),
        scratch_shapes=[pltpu.VMEM((n_blk, FOX_HEADS, tq), F32)],
        compiler_params=_cparams(("parallel", "arbitrary")), name="fox_prompt",
    )(q, kb, vb, lft, kb_meta, vb_meta, lft_meta, tri, um)


def _fox_sample_kernel(pt_ref, q_ref, kn_ref, vn_ref, lfn_ref, t8_ref, u_ref, *rest, g, t):
    ck = rest[:g]
    cv = rest[g:2 * g]
    clf = rest[2 * g:3 * g]
    o_ref, qbd_ref, m_ref, l_ref, acc_ref, r_ref = rest[3 * g:]
    gi = pl.program_id(1)
    rows = t * FOX_HEADS
    own = (lax.broadcasted_iota(I32, (rows, FOX_WIDTH), 0) % FOX_HEADS
           == lax.broadcasted_iota(I32, (rows, FOX_WIDTH), 1) // FOX_HEAD_DIM)

    @pl.when(gi == 0)
    def _():
        q = q_ref[0].astype(F32)
        qt = jnp.concatenate([jnp.broadcast_to(q[i:i + 1], (FOX_HEADS, FOX_WIDTH)) for i in range(t)], axis=0)
        qbd = jnp.where(own, qt, 0.0).astype(BF16)
        qbd_ref[...] = qbd
        a_t = _dot_sel_r(lfn_ref[0], t8_ref[...])
        s = _dot_nt(qbd, kn_ref[0]) - jnp.concatenate([a_t * LOG2E] * t, axis=0)
        rt = lax.broadcasted_iota(I32, (rows, t), 0) // FOX_HEADS
        ct = lax.broadcasted_iota(I32, (rows, t), 1)
        s = jnp.where(ct <= rt, s, -jnp.inf)
        m = jnp.max(s, axis=1, keepdims=True)
        p = jnp.exp2(s - m)
        m_ref[...] = m
        l_ref[...] = jnp.sum(p, axis=1, keepdims=True)
        acc_ref[...] = _dot(p.astype(BF16), vn_ref[0])
        r_ref[...] = jnp.zeros_like(r_ref)

    lf_all = jnp.concatenate([clf[jj][0] for jj in range(g)], axis=0)
    after = _dot_sel_r(lf_all, u_ref[...])
    r_run = r_ref[:, 0:1]
    qbd = qbd_ref[...]
    scores = [None] * g
    for jj in reversed(range(g)):
        bias = (after[jj * FOX_HEADS:(jj + 1) * FOX_HEADS] + r_run) * LOG2E
        r_run = r_run + jnp.sum(clf[jj][0], axis=1, keepdims=True)
        kt = ck[jj][0].reshape(FOX_WIDTH, PAGE_SIZE).astype(BF16)
        scores[jj] = _dot(qbd, kt) + jnp.concatenate([bias] * t, axis=0)
    r_ref[...] = jnp.broadcast_to(r_run, r_ref.shape)
    m = m_ref[...]
    m_new = m
    for s in scores:
        m_new = jnp.maximum(m_new, jnp.max(s, axis=1, keepdims=True))
    alpha = jnp.exp2(m - m_new)
    l = alpha * l_ref[...]
    acc = alpha * acc_ref[...]
    for jj, s in enumerate(scores):
        p = jnp.exp2(s - m_new)
        l = l + jnp.sum(p, axis=1, keepdims=True)
        acc = acc + _dot_nt(p.astype(BF16), cv[jj][0].reshape(FOX_WIDTH, PAGE_SIZE).astype(BF16))
    m_ref[...] = m_new
    l_ref[...] = l
    acc_ref[...] = acc

    @pl.when(gi == pl.num_programs(1) - 1)
    def _():
        o = jnp.where(own, acc / l, 0.0)
        o_ref[0] = jnp.concatenate(
            [jnp.sum(o[i * FOX_HEADS:(i + 1) * FOX_HEADS], axis=0, keepdims=True) for i in range(t)],
            axis=0).astype(BF16)


def _fox_sample(q, kb_new, vb_new, lft_new, cache_kt, cache_vt, cache_lft, page_table):
    db, t, w = q.shape
    n_pages = page_table.shape[1]
    g = min(PAGES_PER_STEP, n_pages)
    assert n_pages % g == 0
    ng = n_pages // g
    rows = FOX_HEADS * t
    t8 = jnp.asarray(np.triu(np.ones((t, t), np.float32)), BF16)
    u = jnp.asarray(np.tril(np.ones((PAGE_SIZE, PAGE_SIZE), np.float32), -1), BF16)
    tok = lambda wd: pl.BlockSpec((1, t, wd), lambda b, gi, pt: (b, 0, 0))
    full = lambda a: pl.BlockSpec(a.shape, lambda b, gi, pt: (0,) * a.ndim)

    def page(shape, jj):
        zeros = (0,) * (len(shape) - 1)
        return pl.BlockSpec(shape, lambda b, gi, pt: (pt[b * n_pages + (ng - 1 - gi) * g + jj],) + zeros)

    kv_page = (1, FOX_HEADS, FOX_HEAD_DIM, PAGE_SIZE)
    in_specs = ([tok(w), tok(w), tok(w), pl.BlockSpec((1, FOX_HEADS, t), lambda b, gi, pt: (b, 0, 0)),
                 full(t8), full(u)]
                + [page(kv_page, jj) for jj in range(g)]
                + [page(kv_page, jj) for jj in range(g)]
                + [page((1, FOX_HEADS, PAGE_SIZE), jj) for jj in range(g)])
    grid_spec = pltpu.PrefetchScalarGridSpec(
        num_scalar_prefetch=1, grid=(db, ng), in_specs=in_specs,
        out_specs=pl.BlockSpec((1, t, w), lambda b, gi, pt: (b, 0, 0)),
        scratch_shapes=[pltpu.VMEM((rows, w), BF16), pltpu.VMEM((rows, 1), F32), pltpu.VMEM((rows, 1), F32),
                        pltpu.VMEM((rows, w), F32), pltpu.VMEM((FOX_HEADS, LANES), F32)])
    return pl.pallas_call(
        functools.partial(_fox_sample_kernel, g=g, t=t), grid_spec=grid_spec,
        out_shape=jax.ShapeDtypeStruct((db, t, w), BF16),
        compiler_params=_cparams(("parallel", "arbitrary")), name="fox_sample",
    )(page_table.reshape(-1), q, kb_new, vb_new, lft_new, t8, u,
      *([cache_kt] * g), *([cache_vt] * g), *([cache_lft] * g))


def _gla_tables(c):
    idx = np.arange(c)
    ws = [np.tril(np.ones((c, c), np.float32))]
    ms = [np.eye(c, dtype=np.float32)]
    sz = 2
    while sz <= c:
        seg = idx // sz
        upper = (idx % sz) >= sz // 2
        mid = seg * sz + sz // 2 - 1
        j = idx[None, :]
        r = idx[:, None]
        w_up = (j > mid[:, None]) & (j <= r)
        w_lo = (j > r) & (j <= mid[:, None])
        ws.append(np.where(upper[:, None], w_up, w_lo).astype(np.float32))
        ms.append(((seg[:, None] == seg[None, :]) & upper[:, None] & ~upper[None, :]).astype(np.float32))
        sz *= 2
    return (jnp.asarray(np.concatenate(ws, axis=0), BF16),
            jnp.asarray(np.stack([np.concatenate([m, m], axis=0) for m in ms]), F32))


def _gla_kernel(q_ref, k_ref, v_ref, la_ref, s0_ref, wall_ref, msk_ref, o_ref, sfin_ref, state_ref,
                *, c, n_levels, nb, shared_s0):
    ci = pl.program_id(1)

    @pl.when(ci == 0)
    def _():
        for bi in range(nb):
            state_ref[bi] = s0_ref[0 if shared_s0 else bi]

    lane = lax.broadcasted_iota(I32, (c, LANES), 1)
    r128 = lax.broadcasted_iota(I32, (LANES, LANES), 0)
    half = GLA_KEY_DIM
    in_a = lane < half

    def stack_heads(x):
        return jnp.concatenate([jnp.where(in_a, x, 0.0), jnp.where(in_a, 0.0, x)], axis=0).astype(BF16)

    for bi in range(nb):
        la3 = _split3(la_ref[bi])
        seg = _dot(wall_ref[...], jnp.concatenate(la3, axis=1))
        seg = seg[:, 2 * GLA_K:] + seg[:, GLA_K:2 * GLA_K] + seg[:, :GLA_K]
        b_all = seg[:c]
        e_all = jnp.exp(seg[c:])
        for p in range(GLA_HEADS // 2):
            ksl = slice(p * LANES, (p + 1) * LANES)
            q = q_ref[bi, :, ksl]
            k = k_ref[bi, :, ksl]
            v16 = v_ref[bi, :, 2 * p * LANES:(2 * p + 2) * LANES].astype(BF16)
            sab = msk_ref[0] * _dot_nt(stack_heads(q), k.astype(BF16))
            for lv in range(n_levels):
                e = e_all[lv * c:(lv + 1) * c, ksl]
                sab = sab + msk_ref[lv + 1] * _dot_nt(stack_heads(q * e), (k * e).astype(BF16))
            st = state_ref[bi, p]
            b = b_all[:, ksl]
            inter = _dot(stack_heads(q * jnp.exp(b)), st.astype(BF16))
            s16 = sab.astype(BF16)
            o_ref[bi, :, 2 * p * LANES:(2 * p + 1) * LANES] = inter[:c] + _dot(s16[:c], v16[:, :LANES])
            o_ref[bi, :, (2 * p + 1) * LANES:(2 * p + 2) * LANES] = inter[c:] + _dot(s16[c:], v16[:, LANES:])
            b_end = b[c - 1:c, :]
            kdec = (k * jnp.exp(b_end - b)).astype(BF16)
            upd = _dot_tn(kdec, v16)
            upd = jnp.where(r128 < half, upd[:, :LANES], upd[:, LANES:])
            state_ref[bi, p] = _col_from_row(jnp.exp(b_end), LANES) * st + upd

    @pl.when(ci == pl.num_programs(1) - 1)
    def _():
        sfin_ref[...] = state_ref[...]


def _gla(qg, kg, vg, la, s0, c):
    b, l, _ = qg.shape
    assert l % c == 0
    nb = GLA_SEQS_PER_STEP
    while b % nb:
        nb //= 2
    wall, msk = _gla_tables(c)
    n_levels = msk.shape[0] - 1
    shared_s0 = s0.shape[0] != b
    st_shape = (GLA_HEADS // 2, LANES, GLA_VAL_DIM)
    s0_spec = (pl.BlockSpec((1,) + st_shape, lambda bi, ci: (0, 0, 0, 0)) if shared_s0
               else pl.BlockSpec((nb,) + st_shape, lambda bi, ci: (bi, 0, 0, 0)))
    tok = lambda wd: pl.BlockSpec((nb, c, wd), lambda bi, ci: (bi, ci, 0))
    full = lambda a: pl.BlockSpec(a.shape, lambda bi, ci: (0,) * a.ndim)
    return pl.pallas_call(
        functools.partial(_gla_kernel, c=c, n_levels=n_levels, nb=nb, shared_s0=shared_s0),
        grid=(b // nb, l // c),
        in_specs=[tok(GLA_K), tok(GLA_K), tok(GLA_V), tok(GLA_K), s0_spec, full(wall), full(msk)],
        out_specs=(tok(GLA_V), pl.BlockSpec((nb,) + st_shape, lambda bi, ci: (bi, 0, 0, 0))),
        out_shape=(jax.ShapeDtypeStruct((b, l, GLA_V), F32), jax.ShapeDtypeStruct((b,) + st_shape, F32)),
        scratch_shapes=[pltpu.VMEM((nb,) + st_shape, F32)],
        compiler_params=_cparams(("parallel", "arbitrary")), name="gla",
    )(qg, kg, vg, la, s0, wall, msk)


def _merge_kernel(x_ref, of_ref, og_ref, gmix_ref, wrg_ref, wfo_ref, wgo_ref, wo_ref, gng_ref, gffn_ref,
                  wrt_ref, brt_ref, x2_ref, h2_ref, eid_ref, wsel_ref):
    x = x_ref[...]
    tm, d = x.shape
    h = _rmsnorm(x, gmix_ref[...]).astype(BF16)
    z = _dot(h, wrg_ref[...])
    r = z[:, :GLA_V]
    gate_f = z[:, GLA_V:GLA_V + d]
    gate_g = z[:, GLA_V + d:]
    og = og_ref[...]
    parts = []
    for hh in range(GLA_HEADS):
        blk = og[:, hh * GLA_VAL_DIM:(hh + 1) * GLA_VAL_DIM]
        parts.append(blk * lax.rsqrt(jnp.mean(blk * blk, axis=-1, keepdims=True) + EPS))
    ogn = jnp.concatenate(parts, axis=1)
    ogs = (ogn * gng_ref[...] * (r * _sigmoid(r))).astype(BF16)
    branch_f = _dot(of_ref[...], wfo_ref[...])
    branch_g = _dot(ogs, wgo_ref[...])
    merged = _sigmoid(gate_f) * branch_f + _sigmoid(gate_g) * branch_g
    x2 = x + _dot(merged.astype(BF16), wo_ref[...])
    x2_ref[...] = x2
    h2 = _rmsnorm(x2, gffn_ref[...])
    h2_ref[...] = h2.astype(BF16)

    a = _split3(h2)
    wr = (wrt_ref[0], wrt_ref[1], wrt_ref[2])
    lt = (_dot_nt(wr[1], a[1]) + _dot_nt(wr[0], a[2]) + _dot_nt(wr[2], a[0])
          + _dot_nt(wr[0], a[1]) + _dot_nt(wr[1], a[0]) + _dot_nt(wr[0], a[0])) + brt_ref[...]
    row8 = lax.broadcasted_iota(I32, (SUBLANES, tm), 0)
    gl = jnp.where(row8 < N_GROUPS, lt[:SUBLANES], -jnp.inf)
    gmax = jnp.max(gl, axis=0, keepdims=True)
    g_idx = jnp.min(jnp.where(gl == gmax, row8, SUBLANES), axis=0, keepdims=True)
    g_sel = 1.0 / jnp.sum(jnp.exp(gl - gmax), axis=0, keepdims=True)
    rowe = lax.broadcasted_iota(I32, (N_EXPERTS, tm), 0)
    e1 = jnp.where(rowe // EXPERTS_PER_GROUP == g_idx, lt[SUBLANES:], -jnp.inf)
    v1 = jnp.max(e1, axis=0, keepdims=True)
    i1 = jnp.min(jnp.where(e1 == v1, rowe, N_EXPERTS), axis=0, keepdims=True)
    e2 = jnp.where(rowe == i1, -jnp.inf, e1)
    v2 = jnp.max(e2, axis=0, keepdims=True)
    i2 = jnp.min(jnp.where(e2 == v2, rowe, N_EXPERTS), axis=0, keepdims=True)
    tt = jnp.exp(v2 - v1)
    eid_ref[...] = jnp.concatenate([i1, i2], axis=0)
    wsel_ref[...] = jnp.concatenate([g_sel / (1.0 + tt), g_sel * tt / (1.0 + tt)], axis=0)


def _merge(x, o_fox, o_gla, mw, tm):
    n, d = x.shape
    assert n % tm == 0
    row = lambda wd: pl.BlockSpec((tm, wd), lambda i: (i, 0))
    full = lambda a: pl.BlockSpec(a.shape, lambda i: (0,) * a.ndim, pipeline_mode=pl.Buffered(1))
    ws = (mw["g_mix"], mw["w_rg"], mw["w_fo"], mw["w_go"], mw["w_o"], mw["g_gla"], mw["g_ffn"],
          mw["w_rt"], mw["b_rt"])
    return pl.pallas_call(
        _merge_kernel, grid=(n // tm,),
        in_specs=[row(d), row(FOX_WIDTH), row(GLA_V)] + [full(a) for a in ws],
        out_specs=(row(d), row(d), pl.BlockSpec((2, tm), lambda i: (0, i)), pl.BlockSpec((2, tm), lambda i: (0, i))),
        out_shape=(jax.ShapeDtypeStruct((n, d), F32), jax.ShapeDtypeStruct((n, d), BF16),
                   jax.ShapeDtypeStruct((2, n), I32), jax.ShapeDtypeStruct((2, n), F32)),
        compiler_params=_cparams(("parallel",)), name="merge",
    )(x, o_fox, o_gla, *ws)


def _run_sizes(max_rows):
    sizes = []
    s = RUN_ALIGN
    while s <= max_rows:
        sizes.append(s)
        s *= 2
    return tuple(reversed(sizes))


def _for_each_run_piece(n_rows, a_off, b_off, sizes, fn):
    for size in sizes:
        bit = (n_rows & size) != 0

        @pl.when(bit)
        def _(a_off=a_off, b_off=b_off, size=size):
            fn(pl.multiple_of(a_off, RUN_ALIGN), pl.multiple_of(b_off, RUN_ALIGN), size)

        inc = jnp.where(bit, size, 0)
        a_off = a_off + inc
        b_off = b_off + inc


def _dispatch_kernel(p_ref, loc_ref, dst_ref, toff_ref, tlen_ref, nu_ref,
                     h2_ref, eid_ref, wsel_ref, locf_ref, triu_ref,
                     xs_ref, pos_ref, xloc_ref, zero_ref, sem, *, nt, bt, sl, d, n_tiles):
    i = pl.program_id(0)
    sizes = _run_sizes(bt)
    t = zero_ref.shape[0]

    def run_copy(lo, do, size):
        return pltpu.make_async_copy(xloc_ref.at[pl.ds(lo, size)], xs_ref.at[pl.ds(do, size)], sem)

    @pl.when(i < nt)
    def _():
        eid = eid_ref[...]
        wsel = wsel_ref[...]
        rowe = lax.broadcasted_iota(I32, (N_EXPERTS, bt), 0)
        oh0 = jnp.where(rowe == eid[0:1], 1.0, 0.0)
        oh1 = jnp.where(rowe == eid[1:2], 1.0, 0.0)
        pref0 = _dot(oh0.astype(BF16), triu_ref[...])
        pref1 = _dot(oh1.astype(BF16), triu_ref[...])
        n0 = jnp.sum(oh0, axis=1, keepdims=True)
        loc = locf_ref[0]
        pos0 = jnp.sum(oh0 * (loc + pref0), axis=0, keepdims=True)
        pos1 = jnp.sum(oh1 * (loc + n0 + pref1), axis=0, keepdims=True)
        pos_ref[...] = jnp.concatenate([pos0, pos1], axis=0)
        srow = lax.broadcasted_iota(I32, (sl, bt), 0).astype(F32)
        m0 = srow == pos0
        m1 = srow == pos1
        perm = jnp.where(m0, 1.0, jnp.where(m1, 1.0, 0.0)).astype(BF16)
        wslot = jnp.sum(jnp.where(m0, wsel[0:1], jnp.where(m1, wsel[1:2], 0.0)), axis=1, keepdims=True)
        xloc_ref[:, :d] = _dot(perm, h2_ref[...])
        xloc_ref[:, d:] = jnp.broadcast_to(wslot, (sl, W_COLS))

        def runs(act):
            def body(e, carry):
                idx = i * N_EXPERTS + e
                _for_each_run_piece(p_ref[idx], loc_ref[idx], dst_ref[idx], sizes,
                                    lambda lo, do, size: act(run_copy(lo, do, size)))
                return carry
            lax.fori_loop(0, N_EXPERTS, body, 0)

        runs(lambda cp: cp.start())
        runs(lambda cp: cp.wait())

    @pl.when(i == nt)
    def _():
        zero_ref[...] = jnp.zeros_like(zero_ref)
        tsizes = _run_sizes(t // 2)

        def tail_copy(zo, do, size):
            return pltpu.make_async_copy(zero_ref.at[pl.ds(0, size)], xs_ref.at[pl.ds(do, size)], sem)

        def tails(act):
            def body(e, carry):
                _for_each_run_piece(tlen_ref[e], jnp.int32(0), toff_ref[e], tsizes,
                                    lambda zo, do, size: act(tail_copy(zo, do, size)))
                return carry
            lax.fori_loop(0, N_EXPERTS, body, 0)

            def unused(j, carry):
                act(tail_copy(0, pl.multiple_of(j * t, t), t))
                return carry
            lax.fori_loop(nu_ref[0], n_tiles, unused, 0)

        tails(lambda cp: cp.start())
        tails(lambda cp: cp.wait())


def _ffn_kernel(te_ref, nu_ref, xs_ref, wg_ref, wu_ref, wd_ref, y_ref, *, d):
    i = pl.program_id(0)

    @pl.when(i < nu_ref[0])
    def _():
        x = xs_ref[:, :d].astype(BF16)
        g = _dot(x, wg_ref[0])
        u = _dot(x, wu_ref[0])
        a = (g * _sigmoid(g) * u).astype(BF16)
        y = _dot(a, wd_ref[0])
        wrow = xs_ref[:, d:]
        y_ref[...] = y * jnp.concatenate([wrow] * (d // W_COLS), axis=1)

    @pl.when(i >= nu_ref[0])
    def _():
        y_ref[...] = jnp.zeros_like(y_ref)


def _combine_kernel(p_ref, loc_ref, dst_ref, pos_ref, x2_ref, gfin_ref, y_hbm, out_ref, yloc_ref, sem,
                    *, bt, sl):
    i = pl.program_id(0)
    sizes = _run_sizes(bt)
    yloc_ref[...] = jnp.zeros_like(yloc_ref)

    def run_copy(lo, do, size):
        return pltpu.make_async_copy(y_hbm.at[pl.ds(do, size)], yloc_ref.at[pl.ds(lo, size)], sem)

    def runs(act):
        def body(e, carry):
            idx = i * N_EXPERTS + e
            _for_each_run_piece(p_ref[idx], loc_ref[idx], dst_ref[idx], sizes,
                                lambda lo, do, size: act(run_copy(lo, do, size)))
            return carry
        lax.fori_loop(0, N_EXPERTS, body, 0)

    runs(lambda cp: cp.start())
    runs(lambda cp: cp.wait())
    pos = pos_ref[...]
    pos0 = _col_from_row(pos[0:1], bt)
    pos1 = _col_from_row(pos[1:2], bt)
    scol = lax.broadcasted_iota(I32, (bt, sl), 1).astype(F32)
    perm_t = jnp.where(scol == pos0, 1.0, jnp.where(scol == pos1, 1.0, 0.0)).astype(BF16)
    moe = _dot(perm_t, yloc_ref[...].astype(BF16))
    out_ref[...] = _rmsnorm(x2_ref[...] + moe, gfin_ref[...])


def _moe_prompt(x2, h2, eid, wsel, wg, wu, wd, g_fin):
    n, d = x2.shape
    bt, t = MOE_TOKENS, MOE_ROWS
    assert n % bt == 0
    nt = n // bt
    sl = -(-(2 * bt + N_EXPERTS * (RUN_ALIGN - 1)) // LANES) * LANES
    onehot = eid.reshape(2, nt, bt)[..., None] == jnp.arange(N_EXPERTS, dtype=I32)
    cnt = jnp.sum(onehot, axis=(0, 2), dtype=I32)
    p = (cnt + RUN_ALIGN - 1) // RUN_ALIGN * RUN_ALIGN
    loc = jnp.cumsum(p, axis=1) - p
    tot = jnp.sum(p, axis=0)
    reg = (tot + t - 1) // t * t
    reg_end = jnp.cumsum(reg)
    reg_start = reg_end - reg
    dst = reg_start[None, :] + jnp.cumsum(p, axis=0) - p
    n_tiles = (2 * n + nt * N_EXPERTS * (RUN_ALIGN - 1) + N_EXPERTS * (t - 1)) // t + 1
    s_max = n_tiles * t
    n_used = (reg_end[-1] // t).astype(I32).reshape(1)
    tile_start = jnp.arange(n_tiles, dtype=I32) * t
    tile_expert = jnp.minimum(jnp.sum(reg_end[None, :] <= tile_start[:, None], axis=1, dtype=I32), N_EXPERTS - 1)
    p_f, loc_f, dst_f = p.reshape(-1), loc.reshape(-1), dst.reshape(-1).astype(I32)
    locf = loc.astype(F32).reshape(nt, N_EXPERTS, 1)
    triu = jnp.asarray(np.triu(np.ones((bt, bt), np.float32), 1), BF16)

    last = nt - 1
    xs, pos = pl.pallas_call(
        functools.partial(_dispatch_kernel, nt=nt, bt=bt, sl=sl, d=d, n_tiles=n_tiles),
        grid_spec=pltpu.PrefetchScalarGridSpec(
            num_scalar_prefetch=6, grid=(nt + 1,),
            in_specs=[pl.BlockSpec((bt, d), lambda i, *_: (jnp.minimum(i, last), 0)),
                      pl.BlockSpec((2, bt), lambda i, *_: (0, jnp.minimum(i, last))),
                      pl.BlockSpec((2, bt), lambda i, *_: (0, jnp.minimum(i, last))),
                      pl.BlockSpec((1, N_EXPERTS, 1), lambda i, *_: (jnp.minimum(i, last), 0, 0)),
                      pl.BlockSpec((bt, bt), lambda i, *_: (0, 0))],
            out_specs=(pl.BlockSpec(memory_space=pl.ANY),
                       pl.BlockSpec((2, bt), lambda i, *_: (0, jnp.minimum(i, last)))),
            scratch_shapes=[pltpu.VMEM((sl, d + W_COLS), F32), pltpu.VMEM((t, d + W_COLS), F32),
                            pltpu.SemaphoreType.DMA(())]),
        out_shape=(jax.ShapeDtypeStruct((s_max, d + W_COLS), F32), jax.ShapeDtypeStruct((2, n), F32)),
        compiler_params=_cparams(("arbitrary",)), name="moe_dispatch",
    )(p_f, loc_f, dst_f, (reg_start + tot).astype(I32), (reg - tot).astype(I32), n_used, h2, eid, wsel, locf, triu)

    de = wg.shape[2]
    used = lambda i, te, nu: jnp.minimum(i, nu[0] - 1)
    y = pl.pallas_call(
        functools.partial(_ffn_kernel, d=d),
        grid_spec=pltpu.PrefetchScalarGridSpec(
            num_scalar_prefetch=2, grid=(n_tiles,),
            in_specs=[pl.BlockSpec((t, d + W_COLS), lambda i, te, nu: (used(i, te, nu), 0)),
                      pl.BlockSpec((1, d, de), lambda i, te, nu: (te[used(i, te, nu)], 0, 0)),
                      pl.BlockSpec((1, d, de), lambda i, te, nu: (te[used(i, te, nu)], 0, 0)),
                      pl.BlockSpec((1, de, d), lambda i, te, nu: (te[used(i, te, nu)], 0, 0))],
            out_specs=pl.BlockSpec((t, d), lambda i, te, nu: (i, 0))),
        out_shape=jax.ShapeDtypeStruct((s_max, d), F32),
        compiler_params=_cparams(("arbitrary",)), name="moe_ffn",
    )(tile_expert, n_used, xs, wg, wu, wd)

    return pl.pallas_call(
        functools.partial(_combine_kernel, bt=bt, sl=sl),
        grid_spec=pltpu.PrefetchScalarGridSpec(
            num_scalar_prefetch=3, grid=(nt,),
            in_specs=[pl.BlockSpec((2, bt), lambda i, *_: (0, i)),
                      pl.BlockSpec((bt, d), lambda i, *_: (i, 0)),
                      pl.BlockSpec((1, d), lambda i, *_: (0, 0)),
                      pl.BlockSpec(memory_space=pl.ANY)],
            out_specs=pl.BlockSpec((bt, d), lambda i, *_: (i, 0)),
            scratch_shapes=[pltpu.VMEM((sl, d), F32), pltpu.SemaphoreType.DMA(())]),
        out_shape=jax.ShapeDtypeStruct((n, d), F32),
        compiler_params=_cparams(("arbitrary",)), name="moe_combine",
    )(p_f, loc_f, dst_f, pos, x2, g_fin, y)


def _moe_dense_kernel(x2_ref, h2_ref, eid_ref, wsel_ref, gfin_ref, wg_ref, wu_ref, wd_ref, out_ref, acc_ref):
    e = pl.program_id(0)
    n = x2_ref.shape[0]

    @pl.when(e == 0)
    def _():
        acc_ref[...] = jnp.zeros_like(acc_ref)

    eid = eid_ref[...]
    wsel = wsel_ref[...]
    gate_row = jnp.where(eid[0:1] == e, wsel[0:1], 0.0) + jnp.where(eid[1:2] == e, wsel[1:2], 0.0)
    gate = _col_from_row(gate_row, n)
    x = h2_ref[...]
    g = _dot(x, wg_ref[0])
    u = _dot(x, wu_ref[0])
    a = (g * _sigmoid(g) * u).astype(BF16)
    acc_ref[...] += gate * _dot(a, wd_ref[0])

    @pl.when(e == pl.num_programs(0) - 1)
    def _():
        out_ref[...] = _rmsnorm(x2_ref[...] + acc_ref[...], gfin_ref[...])


def _moe_dense(x2, h2, eid, wsel, wg, wu, wd, g_fin):
    n, d = x2.shape
    de = wg.shape[2]
    full = lambda a: pl.BlockSpec(a.shape, lambda e: (0,) * a.ndim)
    return pl.pallas_call(
        _moe_dense_kernel, grid=(wg.shape[0],),
        in_specs=[full(x2), full(h2), full(eid), full(wsel), full(g_fin),
                  pl.BlockSpec((1, d, de), lambda e: (e, 0, 0)),
                  pl.BlockSpec((1, d, de), lambda e: (e, 0, 0)),
                  pl.BlockSpec((1, de, d), lambda e: (e, 0, 0))],
        out_specs=pl.BlockSpec((n, d), lambda e: (0, 0)),
        out_shape=jax.ShapeDtypeStruct((n, d), F32),
        scratch_shapes=[pltpu.VMEM((n, d), F32)],
        compiler_params=_cparams(("arbitrary",)), name="moe_dense",
    )(x2, h2, eid, wsel, g_fin, wg, wu, wd)


def _layer_weights(norm_mix_g, w_in, fox_f_bias, gla_w_a2, gla_b_a, gla_norm_g, w_fox_out, w_gla_out, w_o,
                   norm_ffn_g, w_group_router, b_group_router, w_expert_router, b_expert_router):
    d = w_in.shape[0]
    c = 0
    parts = {}
    for name, width in (("q", FOX_WIDTH), ("k", FOX_WIDTH), ("v", FOX_WIDTH), ("ff", FOX_HEADS),
                        ("qg", GLA_K), ("kg", GLA_K), ("vg", GLA_V), ("lr", GLA_RANK), ("r", GLA_V),
                        ("gf", d), ("gg", d)):
        parts[name] = w_in[:, c:c + width]
        c += width
    assert c == w_in.shape[1]
    g_mix = norm_mix_g.reshape(1, d)
    w_fft = jnp.zeros((2 * SUBLANES, d), F32).at[:FOX_HEADS].set(parts["ff"].T).astype(BF16)
    pw = dict(
        g_mix=g_mix,
        w_qkv=jnp.concatenate([parts["q"], parts["k"], parts["v"]], axis=1).astype(BF16),
        w_fft=w_fft,
        w_gla=jnp.concatenate([parts["qg"], parts["kg"], parts["vg"]], axis=1).astype(BF16),
        w_lr=parts["lr"].astype(BF16),
        w_a2=gla_w_a2.astype(BF16),
        b_a=gla_b_a.reshape(1, GLA_K),
        f_bias=fox_f_bias.reshape(FOX_HEADS, 1),
    )
    w_rt = jnp.zeros((SUBLANES + N_EXPERTS, d), F32)
    w_rt = w_rt.at[:N_GROUPS].set(w_group_router.T).at[SUBLANES:].set(w_expert_router.T)
    b_rt = jnp.zeros((SUBLANES + N_EXPERTS, 1), F32)
    b_rt = b_rt.at[:N_GROUPS, 0].set(b_group_router).at[SUBLANES:, 0].set(b_expert_router)
    mw = dict(
        g_mix=g_mix,
        w_rg=jnp.concatenate([parts["r"], parts["gf"], parts["gg"]], axis=1).astype(BF16),
        w_fo=w_fox_out.astype(BF16), w_go=w_gla_out.astype(BF16), w_o=w_o.astype(BF16),
        g_gla=gla_norm_g.reshape(1, GLA_V), g_ffn=norm_ffn_g.reshape(1, d),
        w_rt=jnp.stack(_split3(w_rt)), b_rt=b_rt,
    )
    return pw, mw


def kernel(x_prompt, x_sample, cache_k, cache_v, cache_log_f, state_gla, page_table, meta_tokens, norm_mix_g,
           w_in, fox_f_bias, gla_w_a2, gla_b_a, gla_norm_g, w_fox_out, w_gla_out, w_o, norm_ffn_g,
           w_group_router, b_group_router, w_expert_router, b_expert_router, w_expert_gate, w_expert_up,
           w_expert_down, norm_final_g):
    depth = w_in.shape[0]
    assert depth == 1, "meta rows are only carried through one layer"
    b, seq, d = x_prompt.shape
    db, t_dec, _ = x_sample.shape
    n_phys = cache_k.shape[1]
    pw, mw = _layer_weights(norm_mix_g[0], w_in[0], fox_f_bias[0], gla_w_a2[0], gla_b_a[0], gla_norm_g[0],
                            w_fox_out[0], w_gla_out[0], w_o[0], norm_ffn_g[0], w_group_router[0],
                            b_group_router[0], w_expert_router[0], b_expert_router[0])
    wg = w_expert_gate[0].astype(BF16)
    wu = w_expert_up[0].astype(BF16)
    wd = w_expert_down[0].astype(BF16)
    g_fin = norm_final_g.reshape(1, d)

    (_, k_m, v_m, kb_m, vb_m, lft_m, qg_m, kg_m, vg_m, la_m) = _project(
        meta_tokens.reshape(1, N_META, d), N_META, pw)
    zero_state = jnp.zeros((1, GLA_HEADS // 2, LANES, GLA_VAL_DIM), F32)
    _, s_meta = _gla(qg_m, kg_m, vg_m, la_m, zero_state, N_META)

    q, k, v, kb, vb, lft, qg, kg, vg, la = _project(x_prompt, min(PROJ_TILE, seq), pw)
    o_fox = _fox_prompt(q, kb, vb, lft, kb_m[0], vb_m[0], lft_m[0], min(ATTN_TILE, seq))
    o_gla, s_gla_p = _gla(qg, kg, vg, la, s_meta, min(GLA_CHUNK, seq))
    n = b * seq
    x2, h2, eid, wsel = _merge(x_prompt.reshape(n, d), o_fox.reshape(n, FOX_WIDTH), o_gla.reshape(n, GLA_V),
                               mw, min(MERGE_TILE, n))
    y_prompt = _moe_prompt(x2, h2, eid, wsel, wg, wu, wd, g_fin).reshape(b, seq, d)

    ns = db * t_dec
    qs, ks, vs, kbs, vbs, lfts, qgs, kgs, vgs, las = _project(x_sample.reshape(1, ns, d), ns, pw)
    tok = lambda a: a.reshape(db, t_dec, a.shape[-1])
    lft_s = lfts.reshape(FOX_HEADS, db, t_dec).transpose(1, 0, 2)
    o_fox_s = _fox_sample(tok(qs), tok(kbs), tok(vbs), lft_s, cache_k[0].transpose(0, 2, 3, 1),
                          cache_v[0].transpose(0, 2, 3, 1), cache_log_f[0].transpose(0, 2, 1), page_table)
    s0 = state_gla[0].reshape(db, GLA_HEADS // 2, LANES, GLA_VAL_DIM)
    o_gla_s, s_gla_s = _gla(tok(qgs), tok(kgs), tok(vgs), tok(las), s0, t_dec)
    x2s, h2s, eids, wsels = _merge(x_sample.reshape(ns, d), o_fox_s.reshape(ns, FOX_WIDTH),
                                   o_gla_s.reshape(ns, GLA_V), mw, ns)
    y_sample = _moe_dense(x2s, h2s, eids, wsels, wg, wu, wd, g_fin).reshape(db, t_dec, d)

    heads = lambda a: a.reshape(a.shape[:-1] + (FOX_HEADS, FOX_HEAD_DIM))
    with_meta = lambda m, r: jnp.concatenate([jnp.broadcast_to(m, (b,) + m.shape[1:]), r], axis=1)
    new_k_prompt = heads(with_meta(k_m, k))[None]
    new_v_prompt = heads(with_meta(v_m, v))[None]
    new_log_f_prompt = with_meta(lft_m.transpose(0, 2, 1), lft.transpose(0, 2, 1))[None]
    new_gla_prompt = s_gla_p.reshape(1, b, GLA_HEADS, GLA_KEY_DIM, GLA_VAL_DIM)
    new_k_sample = heads(tok(ks))[None]
    new_v_sample = heads(tok(vs))[None]
    new_log_f_sample = lfts[0].T.reshape(1, db, t_dec, FOX_HEADS)
    new_gla_sample = s_gla_s.reshape(1, db, GLA_HEADS, GLA_KEY_DIM, GLA_VAL_DIM)
    return (y_prompt, y_sample, new_k_prompt, new_v_prompt, new_log_f_prompt, new_gla_prompt,
            new_k_sample, new_v_sample, new_log_f_sample, new_gla_sample)
```

```python
import functools

import numpy as np
import jax
import jax.numpy as jnp
from jax import lax
from jax.experimental import pallas as pl
from jax.experimental.pallas import tpu as pltpu

F32 = jnp.float32
BF16 = jnp.bfloat16
I32 = jnp.int32
EPS = 1e-6
LOG2E = 1.4426950408889634
N_META = 16
FOX_HEADS = 8
FOX_HEAD_DIM = 64
FOX_WIDTH = FOX_HEADS * FOX_HEAD_DIM
GLA_HEADS = 4
GLA_KEY_DIM = 64
GLA_VAL_DIM = 128
GLA_K = GLA_HEADS * GLA_KEY_DIM
GLA_V = GLA_HEADS * GLA_VAL_DIM
GLA_RANK = 16
GLA_TAU = 16.0
GLA_CHUNK = 128
N_GROUPS = 4
EXPERTS_PER_GROUP = 8
N_EXPERTS = N_GROUPS * EXPERTS_PER_GROUP
PAGE_SIZE = 128

LANES = 128
SUBLANES = 8
VMEM_LIMIT = 56 * 1024 * 1024

PROJ_TILE = 512
ATTN_TILE = 512
MERGE_TILE = 512
MOE_TOKENS = 512
MOE_ROWS = 512
RUN_ALIGN = SUBLANES
PAGES_PER_STEP = 16
GLA_SEQS_PER_STEP = 4
W_COLS = LANES


def _cparams(sem):
    return pltpu.CompilerParams(dimension_semantics=sem, vmem_limit_bytes=VMEM_LIMIT)


def _dot(a, b):
    return jnp.dot(a, b, preferred_element_type=F32)


def _dot_nt(a, b):
    return lax.dot_general(a, b, (((1,), (1,)), ((), ())), preferred_element_type=F32)


def _dot_tn(a, b):
    return lax.dot_general(a, b, (((0,), (0,)), ((), ())), preferred_element_type=F32)


def _split3(a):
    hi = a.astype(BF16)
    r1 = a - hi.astype(F32)
    mid = r1.astype(BF16)
    lo = (r1 - mid.astype(F32)).astype(BF16)
    return hi, mid, lo


def _dot_sel_l(m, a):
    hi, mid, lo = _split3(a)
    return _dot(m, lo) + _dot(m, mid) + _dot(m, hi)


def _dot_sel_r(a, m):
    hi, mid, lo = _split3(a)
    return _dot(lo, m) + _dot(mid, m) + _dot(hi, m)


def _rmsnorm(x, g):
    return x * lax.rsqrt(jnp.mean(x * x, axis=-1, keepdims=True) + EPS) * g


def _log_sigmoid(x):
    return jnp.minimum(x, 0.0) - jnp.log1p(jnp.exp(-jnp.abs(x)))


def _sigmoid(x):
    return 1.0 / (1.0 + jnp.exp(-x))


def _col_from_row(row, n):
    r = lax.broadcasted_iota(I32, (n, n), 0)
    c = lax.broadcasted_iota(I32, (n, n), 1)
    return jnp.sum(jnp.where(r == c, jnp.broadcast_to(row, (n, n)), 0.0), axis=1, keepdims=True)


def _proj_kernel(x_ref, g_ref, wqkv_ref, wfft_ref, wgla_ref, wlr_ref, wa2_ref, ba_ref, fb_ref,
                 q_ref, k_ref, v_ref, kb_ref, vb_ref, lft_ref, qg_ref, kg_ref, vg_ref, la_ref):
    h = _rmsnorm(x_ref[0], g_ref[...]).astype(BF16)
    z = _dot(h, wqkv_ref[...])
    q_ref[0] = (z[:, :FOX_WIDTH] * (FOX_HEAD_DIM ** -0.5 * LOG2E)).astype(BF16)
    k = z[:, FOX_WIDTH:2 * FOX_WIDTH]
    v = z[:, 2 * FOX_WIDTH:]
    k_ref[0] = k
    v_ref[0] = v
    kb_ref[0] = k.astype(BF16)
    vb_ref[0] = v.astype(BF16)
    fft = _dot_nt(wfft_ref[...], h)[:FOX_HEADS]
    lft_ref[0] = _log_sigmoid(fft + fb_ref[...])
    zg = _dot(h, wgla_ref[...])
    qg_ref[0] = zg[:, :GLA_K] * GLA_KEY_DIM ** -0.5
    kg_ref[0] = zg[:, GLA_K:2 * GLA_K]
    vg_ref[0] = zg[:, 2 * GLA_K:]
    lr = _dot(h, wlr_ref[...])
    xa = _dot(lr.astype(BF16), wa2_ref[...]) + ba_ref[...]
    la_ref[0] = _log_sigmoid(xa) / GLA_TAU


def _project(x3, tm, pw):
    bx, lx, d = x3.shape
    assert lx % tm == 0
    grid = (bx, lx // tm)
    row = lambda w: pl.BlockSpec((1, tm, w), lambda b, i: (b, i, 0))
    full = lambda a: pl.BlockSpec(a.shape, lambda b, i: (0,) * a.ndim)
    ws = (pw["g_mix"], pw["w_qkv"], pw["w_fft"], pw["w_gla"], pw["w_lr"], pw["w_a2"], pw["b_a"], pw["f_bias"])
    out_shape = (
        jax.ShapeDtypeStruct((bx, lx, FOX_WIDTH), BF16),
        jax.ShapeDtypeStruct((bx, lx, FOX_WIDTH), F32),
        jax.ShapeDtypeStruct((bx, lx, FOX_WIDTH), F32),
        jax.ShapeDtypeStruct((bx, lx, FOX_WIDTH), BF16),
        jax.ShapeDtypeStruct((bx, lx, FOX_WIDTH), BF16),
        jax.ShapeDtypeStruct((bx, FOX_HEADS, lx), F32),
        jax.ShapeDtypeStruct((bx, lx, GLA_K), F32),
        jax.ShapeDtypeStruct((bx, lx, GLA_K), F32),
        jax.ShapeDtypeStruct((bx, lx, GLA_V), F32),
        jax.ShapeDtypeStruct((bx, lx, GLA_K), F32),
    )
    out_specs = (row(FOX_WIDTH), row(FOX_WIDTH), row(FOX_WIDTH), row(FOX_WIDTH), row(FOX_WIDTH),
                 pl.BlockSpec((1, FOX_HEADS, tm), lambda b, i: (b, 0, i)),
                 row(GLA_K), row(GLA_K), row(GLA_V), row(GLA_K))
    return pl.pallas_call(
        _proj_kernel, grid=grid,
        in_specs=[row(d)] + [full(a) for a in ws],
        out_specs=out_specs, out_shape=out_shape,
        compiler_params=_cparams(("parallel", "parallel")), name="proj",
    )(x3, *ws)


def _online_update(s, v, state):
    m, l, acc = state
    m_new = jnp.maximum(m, jnp.max(s, axis=1, keepdims=True))
    alpha = jnp.exp2(m - m_new)
    p = jnp.exp2(s - m_new)
    l = alpha * l + jnp.sum(p, axis=1, keepdims=True)
    acc = alpha * acc + _dot(p.astype(BF16), v)
    return m_new, l, acc


def _fox_prompt_kernel(q_ref, kb_ref, vb_ref, lft_ref, km_ref, vm_ref, lfm_ref, tri_ref, um_ref,
                       o_ref, c_ref, *, tq):
    qi = pl.program_id(1)
    n_blk = kb_ref.shape[1] // tq

    @pl.when(qi == 0)
    def _():
        carry = jnp.zeros((FOX_HEADS, 1), F32)
        for jb in range(n_blk):
            for js in range(tq // LANES):
                off = jb * tq + js * LANES
                cs = _dot_sel_r(lft_ref[0, :, off:off + LANES], tri_ref[...]) + carry
                c_ref[jb, :, js * LANES:(js + 1) * LANES] = cs * LOG2E
                carry = cs[:, LANES - 1:LANES]

    bias_meta = _dot_sel_r(lfm_ref[...], um_ref[...]) * LOG2E
    lane = lax.broadcasted_iota(I32, (tq, LANES), 1)
    row = lax.broadcasted_iota(I32, (tq, tq), 0)
    col = lax.broadcasted_iota(I32, (tq, tq), 1)
    in_a = lane < FOX_HEAD_DIM
    for hp in range(FOX_HEADS // 2):
        sl = slice(hp * LANES, (hp + 1) * LANES)
        ha, hb = 2 * hp, 2 * hp + 1
        qp = q_ref[0, :, sl].astype(F32)
        qa = jnp.where(in_a, qp, 0.0).astype(BF16)
        qb = jnp.where(in_a, 0.0, qp).astype(BF16)
        km = km_ref[:, sl]
        vm = vm_ref[:, sl]

        def first(qh, h, km=km, vm=vm):
            s = _dot_nt(qh, km) + bias_meta[h:h + 1, :]
            m = jnp.max(s, axis=1, keepdims=True)
            p = jnp.exp2(s - m)
            return m, jnp.sum(p, axis=1, keepdims=True), _dot(p.astype(BF16), vm)

        def block(j, states, masked, qa=qa, qb=qb, ha=ha, hb=hb, sl=sl):
            start = pl.multiple_of(j * tq, tq)
            kc = kb_ref[0, pl.ds(start, tq), sl]
            vc = vb_ref[0, pl.ds(start, tq), sl]
            out = []
            for qh, h, st in ((qa, ha, states[0]), (qb, hb, states[1])):
                s = _dot_nt(qh, kc) - c_ref[j, h:h + 1, :]
                if masked:
                    s = jnp.where(col <= row, s, -jnp.inf)
                out.append(_online_update(s, vc, st))
            return tuple(out)

        states = (first(qa, ha), first(qb, hb))
        states = lax.fori_loop(0, qi, lambda j, st: block(j, st, False), states)
        (_, la, acc_a), (_, lb, acc_b) = block(qi, states, True)
        o_ref[0, :, sl] = jnp.where(in_a, acc_a / la, acc_b / lb).astype(BF16)


def _fox_prompt(q, kb, vb, lft, kb_meta, vb_meta, lft_meta, tq):
    b, l, w = q.shape
    assert l % tq == 0
    n_blk = l // tq
    tri = jnp.asarray(np.triu(np.ones((LANES, LANES), np.float32)), BF16)
    um = jnp.asarray(np.tril(np.ones((N_META, N_META), np.float32), -1), BF16)
    full = lambda a: pl.BlockSpec(a.shape, lambda bi, qi: (0,) * a.ndim)
    return pl.pallas_call(
        functools.partial(_fox_prompt_kernel, tq=tq), grid=(b, n_blk),
        in_specs=[pl.BlockSpec((1, tq, w), lambda bi, qi: (bi, qi, 0)),
                  pl.BlockSpec((1, l, w), lambda bi, qi: (bi, 0, 0)),
                  pl.BlockSpec((1, l, w), lambda bi, qi: (bi, 0, 0)),
                  pl.BlockSpec((1, FOX_HEADS, l), lambda bi, qi: (bi, 0, 0)),
                  full(kb_meta), full(vb_meta), full(lft_meta), full(tri), full(um)],
        out_specs=pl.BlockSpec((1, tq, w), lambda bi, qi: (bi, qi, 0)),
        out_shape=jax.ShapeDtypeStruct((b, l, w), BF16),
        scratch_shapes=[pltpu.VMEM((n_blk, FOX_HEADS, tq), F32)],
        compiler_params=_cparams(("parallel", "arbitrary")), name="fox_prompt",
    )(q, kb, vb, lft, kb_meta, vb_meta, lft_meta, tri, um)


def _fox_sample_kernel(pt_ref, q_ref, kn_ref, vn_ref, lfn_ref, t8_ref, u_ref, *rest, g, t):
    ck = rest[:g]
    cv = rest[g:2 * g]
    clf = rest[2 * g:3 * g]
    o_ref, qbd_ref, m_ref, l_ref, acc_ref, r_ref = rest[3 * g:]
    gi = pl.program_id(1)
    rows = t * FOX_HEADS
    own = (lax.broadcasted_iota(I32, (rows, FOX_WIDTH), 0) % FOX_HEADS
           == lax.broadcasted_iota(I32, (rows, FOX_WIDTH), 1) // FOX_HEAD_DIM)

    @pl.when(gi == 0)
    def _():
        q = q_ref[0].astype(F32)
        qt = jnp.concatenate([jnp.broadcast_to(q[i:i + 1], (FOX_HEADS, FOX_WIDTH)) for i in range(t)], axis=0)
        qbd = jnp.where(own, qt, 0.0).astype(BF16)
        qbd_ref[...] = qbd
        a_t = _dot_sel_r(lfn_ref[0], t8_ref[...])
        s = _dot_nt(qbd, kn_ref[0]) - jnp.concatenate([a_t * LOG2E] * t, axis=0)
        rt = lax.broadcasted_iota(I32, (rows, t), 0) // FOX_HEADS
        ct = lax.broadcasted_iota(I32, (rows, t), 1)
        s = jnp.where(ct <= rt, s, -jnp.inf)
        m = jnp.max(s, axis=1, keepdims=True)
        p = jnp.exp2(s - m)
        m_ref[...] = m
        l_ref[...] = jnp.sum(p, axis=1, keepdims=True)
        acc_ref[...] = _dot(p.astype(BF16), vn_ref[0])
        r_ref[...] = jnp.zeros_like(r_ref)

    lf_all = jnp.concatenate([clf[jj][0] for jj in range(g)], axis=0)
    after = _dot_sel_r(lf_all, u_ref[...])
    r_run = r_ref[:, 0:1]
    qbd = qbd_ref[...]
    scores = [None] * g
    for jj in reversed(range(g)):
        bias = (after[jj * FOX_HEADS:(jj + 1) * FOX_HEADS] + r_run) * LOG2E
        r_run = r_run + jnp.sum(clf[jj][0], axis=1, keepdims=True)
        kt = ck[jj][0].reshape(FOX_WIDTH, PAGE_SIZE).astype(BF16)
        scores[jj] = _dot(qbd, kt) + jnp.concatenate([bias] * t, axis=0)
    r_ref[...] = jnp.broadcast_to(r_run, r_ref.shape)
    m = m_ref[...]
    m_new = m
    for s in scores:
        m_new = jnp.maximum(m_new, jnp.max(s, axis=1, keepdims=True))
    alpha = jnp.exp2(m - m_new)
    l = alpha * l_ref[...]
    acc = alpha * acc_ref[...]
    for jj, s in enumerate(scores):
        p = jnp.exp2(s - m_new)
        l = l + jnp.sum(p, axis=1, keepdims=True)
        acc = acc + _dot_nt(p.astype(BF16), cv[jj][0].reshape(FOX_WIDTH, PAGE_SIZE).astype(BF16))
    m_ref[...] = m_new
    l_ref[...] = l
    acc_ref[...] = acc

    @pl.when(gi == pl.num_programs(1) - 1)
    def _():
        o = jnp.where(own, acc / l, 0.0)
        o_ref[0] = jnp.concatenate(
            [jnp.sum(o[i * FOX_HEADS:(i + 1) * FOX_HEADS], axis=0, keepdims=True) for i in range(t)],
            axis=0).astype(BF16)


def _fox_sample(q, kb_new, vb_new, lft_new, cache_kt, cache_vt, cache_lft, page_table):
    db, t, w = q.shape
    n_pages = page_table.shape[1]
    g = min(PAGES_PER_STEP, n_pages)
    assert n_pages % g == 0
    ng = n_pages // g
    rows = FOX_HEADS * t
    t8 = jnp.asarray(np.triu(np.ones((t, t), np.float32)), BF16)
    u = jnp.asarray(np.tril(np.ones((PAGE_SIZE, PAGE_SIZE), np.float32), -1), BF16)
    tok = lambda wd: pl.BlockSpec((1, t, wd), lambda b, gi, pt: (b, 0, 0))
    full = lambda a: pl.BlockSpec(a.shape, lambda b, gi, pt: (0,) * a.ndim)

    def page(shape, jj):
        zeros = (0,) * (len(shape) - 1)
        return pl.BlockSpec(shape, lambda b, gi, pt: (pt[b * n_pages + (ng - 1 - gi) * g + jj],) + zeros)

    kv_page = (1, FOX_HEADS, FOX_HEAD_DIM, PAGE_SIZE)
    in_specs = ([tok(w), tok(w), tok(w), pl.BlockSpec((1, FOX_HEADS, t), lambda b, gi, pt: (b, 0, 0)),
                 full(t8), full(u)]
                + [page(kv_page, jj) for jj in range(g)]
                + [page(kv_page, jj) for jj in range(g)]
                + [page((1, FOX_HEADS, PAGE_SIZE), jj) for jj in range(g)])
    grid_spec = pltpu.PrefetchScalarGridSpec(
        num_scalar_prefetch=1, grid=(db, ng), in_specs=in_specs,
        out_specs=pl.BlockSpec((1, t, w), lambda b, gi, pt: (b, 0, 0)),
        scratch_shapes=[pltpu.VMEM((rows, w), BF16), pltpu.VMEM((rows, 1), F32), pltpu.VMEM((rows, 1), F32),
                        pltpu.VMEM((rows, w), F32), pltpu.VMEM((FOX_HEADS, LANES), F32)])
    return pl.pallas_call(
        functools.partial(_fox_sample_kernel, g=g, t=t), grid_spec=grid_spec,
        out_shape=jax.ShapeDtypeStruct((db, t, w), BF16),
        compiler_params=_cparams(("parallel", "arbitrary")), name="fox_sample",
    )(page_table.reshape(-1), q, kb_new, vb_new, lft_new, t8, u,
      *([cache_kt] * g), *([cache_vt] * g), *([cache_lft] * g))


def _gla_tables(c):
    idx = np.arange(c)
    ws = [np.tril(np.ones((c, c), np.float32))]
    ms = [np.eye(c, dtype=np.float32)]
    sz = 2
    while sz <= c:
        seg = idx // sz
        upper = (idx % sz) >= sz // 2
        mid = seg * sz + sz // 2 - 1
        j = idx[None, :]
        r = idx[:, None]
        w_up = (j > mid[:, None]) & (j <= r)
        w_lo = (j > r) & (j <= mid[:, None])
        ws.append(np.where(upper[:, None], w_up, w_lo).astype(np.float32))
        ms.append(((seg[:, None] == seg[None, :]) & upper[:, None] & ~upper[None, :]).astype(np.float32))
        sz *= 2
    return (jnp.asarray(np.concatenate(ws, axis=0), BF16),
            jnp.asarray(np.stack([np.concatenate([m, m], axis=0) for m in ms]), F32))


def _gla_kernel(q_ref, k_ref, v_ref, la_ref, s0_ref, wall_ref, msk_ref, o_ref, sfin_ref, state_ref,
                *, c, n_levels, nb, shared_s0):
    ci = pl.program_id(1)

    @pl.when(ci == 0)
    def _():
        for bi in range(nb):
            state_ref[bi] = s0_ref[0 if shared_s0 else bi]

    lane = lax.broadcasted_iota(I32, (c, LANES), 1)
    r128 = lax.broadcasted_iota(I32, (LANES, LANES), 0)
    half = GLA_KEY_DIM
    in_a = lane < half

    def stack_heads(x):
        return jnp.concatenate([jnp.where(in_a, x, 0.0), jnp.where(in_a, 0.0, x)], axis=0).astype(BF16)

    for bi in range(nb):
        la3 = _split3(la_ref[bi])
        seg = _dot(wall_ref[...], jnp.concatenate(la3, axis=1))
        seg = seg[:, 2 * GLA_K:] + seg[:, GLA_K:2 * GLA_K] + seg[:, :GLA_K]
        b_all = seg[:c]
        e_all = jnp.exp(seg[c:])
        for p in range(GLA_HEADS // 2):
            ksl = slice(p * LANES, (p + 1) * LANES)
            q = q_ref[bi, :, ksl]
            k = k_ref[bi, :, ksl]
            v16 = v_ref[bi, :, 2 * p * LANES:(2 * p + 2) * LANES].astype(BF16)
            sab = msk_ref[0] * _dot_nt(stack_heads(q), k.astype(BF16))
            for lv in range(n_levels):
                e = e_all[lv * c:(lv + 1) * c, ksl]
                sab = sab + msk_ref[lv + 1] * _dot_nt(stack_heads(q * e), (k * e).astype(BF16))
            st = state_ref[bi, p]
            b = b_all[:, ksl]
            inter = _dot(stack_heads(q * jnp.exp(b)), st.astype(BF16))
            s16 = sab.astype(BF16)
            o_ref[bi, :, 2 * p * LANES:(2 * p + 1) * LANES] = inter[:c] + _dot(s16[:c], v16[:, :LANES])
            o_ref[bi, :, (2 * p + 1) * LANES:(2 * p + 2) * LANES] = inter[c:] + _dot(s16[c:], v16[:, LANES:])
            b_end = b[c - 1:c, :]
            kdec = (k * jnp.exp(b_end - b)).astype(BF16)
            upd = _dot_tn(kdec, v16)
            upd = jnp.where(r128 < half, upd[:, :LANES], upd[:, LANES:])
            state_ref[bi, p] = _col_from_row(jnp.exp(b_end), LANES) * st + upd

    @pl.when(ci == pl.num_programs(1) - 1)
    def _():
        sfin_ref[...] = state_ref[...]


def _gla(qg, kg, vg, la, s0, c):
    b, l, _ = qg.shape
    assert l % c == 0
    nb = GLA_SEQS_PER_STEP
    while b % nb:
        nb //= 2
    wall, msk = _gla_tables(c)
    n_levels = msk.shape[0] - 1
    shared_s0 = s0.shape[0] != b
    st_shape = (GLA_HEADS // 2, LANES, GLA_VAL_DIM)
    s0_spec = (pl.BlockSpec((1,) + st_shape, lambda bi, ci: (0, 0, 0, 0)) if shared_s0
               else pl.BlockSpec((nb,) + st_shape, lambda bi, ci: (bi, 0, 0, 0)))
    tok = lambda wd: pl.BlockSpec((nb, c, wd), lambda bi, ci: (bi, ci, 0))
    full = lambda a: pl.BlockSpec(a.shape, lambda bi, ci: (0,) * a.ndim)
    return pl.pallas_call(
        functools.partial(_gla_kernel, c=c, n_levels=n_levels, nb=nb, shared_s0=shared_s0),
        grid=(b // nb, l // c),
        in_specs=[tok(GLA_K), tok(GLA_K), tok(GLA_V), tok(GLA_K), s0_spec, full(wall), full(msk)],
        out_specs=(tok(GLA_V), pl.BlockSpec((nb,) + st_shape, lambda bi, ci: (bi, 0, 0, 0))),
        out_shape=(jax.ShapeDtypeStruct((b, l, GLA_V), F32), jax.ShapeDtypeStruct((b,) + st_shape, F32)),
        scratch_shapes=[pltpu.VMEM((nb,) + st_shape, F32)],
        compiler_params=_cparams(("parallel", "arbitrary")), name="gla",
    )(qg, kg, vg, la, s0, wall, msk)


def _merge_kernel(x_ref, of_ref, og_ref, gmix_ref, wrg_ref, wfo_ref, wgo_ref, wo_ref, gng_ref, gffn_ref,
                  wrt_ref, brt_ref, x2_ref, h2_ref, eid_ref, wsel_ref):
    x = x_ref[...]
    tm, d = x.shape
    h = _rmsnorm(x, gmix_ref[...]).astype(BF16)
    z = _dot(h, wrg_ref[...])
    r = z[:, :GLA_V]
    gate_f = z[:, GLA_V:GLA_V + d]
    gate_g = z[:, GLA_V + d:]
    og = og_ref[...]
    parts = []
    for hh in range(GLA_HEADS):
        blk = og[:, hh * GLA_VAL_DIM:(hh + 1) * GLA_VAL_DIM]
        parts.append(blk * lax.rsqrt(jnp.mean(blk * blk, axis=-1, keepdims=True) + EPS))
    ogn = jnp.concatenate(parts, axis=1)
    ogs = (ogn * gng_ref[...] * (r * _sigmoid(r))).astype(BF16)
    branch_f = _dot(of_ref[...], wfo_ref[...])
    branch_g = _dot(ogs, wgo_ref[...])
    merged = _sigmoid(gate_f) * branch_f + _sigmoid(gate_g) * branch_g
    x2 = x + _dot(merged.astype(BF16), wo_ref[...])
    x2_ref[...] = x2
    h2 = _rmsnorm(x2, gffn_ref[...])
    h2_ref[...] = h2.astype(BF16)

    a = _split3(h2)
    wr = (wrt_ref[0], wrt_ref[1], wrt_ref[2])
    lt = (_dot_nt(wr[1], a[1]) + _dot_nt(wr[0], a[2]) + _dot_nt(wr[2], a[0])
          + _dot_nt(wr[0], a[1]) + _dot_nt(wr[1], a[0]) + _dot_nt(wr[0], a[0])) + brt_ref[...]
    row8 = lax.broadcasted_iota(I32, (SUBLANES, tm), 0)
    gl = jnp.where(row8 < N_GROUPS, lt[:SUBLANES], -jnp.inf)
    gmax = jnp.max(gl, axis=0, keepdims=True)
    g_idx = jnp.min(jnp.where(gl == gmax, row8, SUBLANES), axis=0, keepdims=True)
    g_sel = 1.0 / jnp.sum(jnp.exp(gl - gmax), axis=0, keepdims=True)
    rowe = lax.broadcasted_iota(I32, (N_EXPERTS, tm), 0)
    e1 = jnp.where(rowe // EXPERTS_PER_GROUP == g_idx, lt[SUBLANES:], -jnp.inf)
    v1 = jnp.max(e1, axis=0, keepdims=True)
    i1 = jnp.min(jnp.where(e1 == v1, rowe, N_EXPERTS), axis=0, keepdims=True)
    e2 = jnp.where(rowe == i1, -jnp.inf, e1)
    v2 = jnp.max(e2, axis=0, keepdims=True)
    i2 = jnp.min(jnp.where(e2 == v2, rowe, N_EXPERTS), axis=0, keepdims=True)
    tt = jnp.exp(v2 - v1)
    eid_ref[...] = jnp.concatenate([i1, i2], axis=0)
    wsel_ref[...] = jnp.concatenate([g_sel / (1.0 + tt), g_sel * tt / (1.0 + tt)], axis=0)


def _merge(x, o_fox, o_gla, mw, tm):
    n, d = x.shape
    assert n % tm == 0
    row = lambda wd: pl.BlockSpec((tm, wd), lambda i: (i, 0))
    full = lambda a: pl.BlockSpec(a.shape, lambda i: (0,) * a.ndim, pipeline_mode=pl.Buffered(1))
    ws = (mw["g_mix"], mw["w_rg"], mw["w_fo"], mw["w_go"], mw["w_o"], mw["g_gla"], mw["g_ffn"],
          mw["w_rt"], mw["b_rt"])
    return pl.pallas_call(
        _merge_kernel, grid=(n // tm,),
        in_specs=[row(d), row(FOX_WIDTH), row(GLA_V)] + [full(a) for a in ws],
        out_specs=(row(d), row(d), pl.BlockSpec((2, tm), lambda i: (0, i)), pl.BlockSpec((2, tm), lambda i: (0, i))),
        out_shape=(jax.ShapeDtypeStruct((n, d), F32), jax.ShapeDtypeStruct((n, d), BF16),
                   jax.ShapeDtypeStruct((2, n), I32), jax.ShapeDtypeStruct((2, n), F32)),
        compiler_params=_cparams(("parallel",)), name="merge",
    )(x, o_fox, o_gla, *ws)


def _run_sizes(max_rows):
    sizes = []
    s = RUN_ALIGN
    while s <= max_rows:
        sizes.append(s)
        s *= 2
    return tuple(reversed(sizes))


def _for_each_run_piece(n_rows, a_off, b_off, sizes, fn):
    for size in sizes:
        bit = (n_rows & size) != 0

        @pl.when(bit)
        def _(a_off=a_off, b_off=b_off, size=size):
            fn(pl.multiple_of(a_off, RUN_ALIGN), pl.multiple_of(b_off, RUN_ALIGN), size)

        inc = jnp.where(bit, size, 0)
        a_off = a_off + inc
        b_off = b_off + inc


def _dispatch_kernel(p_ref, loc_ref, dst_ref, toff_ref, tlen_ref, nu_ref,
                     h2_ref, eid_ref, wsel_ref, locf_ref, triu_ref,
                     xs_ref, pos_ref, xloc_ref, zero_ref, sems, *, nt, bt, sl, d, n_tiles):
    i = pl.program_id(0)
    slot = i % 2
    sizes = _run_sizes(bt)
    t = zero_ref.shape[0]

    def runs(tile, buf, act):
        def body(e, carry):
            idx = tile * N_EXPERTS + e
            _for_each_run_piece(
                p_ref[idx], loc_ref[idx], dst_ref[idx], sizes,
                lambda lo, do, size: act(pltpu.make_async_copy(
                    xloc_ref.at[buf, pl.ds(lo, size)], xs_ref.at[pl.ds(do, size)], sems.at[buf])))
            return carry
        lax.fori_loop(0, N_EXPERTS, body, 0)

    @pl.when(i < nt)
    def _():
        eid = eid_ref[...]
        wsel = wsel_ref[...]
        rowe = lax.broadcasted_iota(I32, (N_EXPERTS, bt), 0)
        oh0 = jnp.where(rowe == eid[0:1], 1.0, 0.0)
        oh1 = jnp.where(rowe == eid[1:2], 1.0, 0.0)
        pref0 = _dot(oh0.astype(BF16), triu_ref[...])
        pref1 = _dot(oh1.astype(BF16), triu_ref[...])
        n0 = jnp.sum(oh0, axis=1, keepdims=True)
        loc = locf_ref[0]
        pos0 = jnp.sum(oh0 * (loc + pref0), axis=0, keepdims=True)
        pos1 = jnp.sum(oh1 * (loc + n0 + pref1), axis=0, keepdims=True)
        pos_ref[...] = jnp.concatenate([pos0, pos1], axis=0)
        srow = lax.broadcasted_iota(I32, (sl, bt), 0).astype(F32)
        m0 = srow == pos0
        m1 = srow == pos1
        perm = jnp.where(m0, 1.0, jnp.where(m1, 1.0, 0.0)).astype(BF16)
        wslot = jnp.sum(jnp.where(m0, wsel[0:1], jnp.where(m1, wsel[1:2], 0.0)), axis=1, keepdims=True)
        xloc_ref[slot, :, :d] = _dot(perm, h2_ref[...])
        xloc_ref[slot, :, d:] = jnp.broadcast_to(wslot, (sl, W_COLS))
        runs(i, slot, lambda cp: cp.start())

    @pl.when(i > 0)
    def _():
        runs(i - 1, 1 - slot, lambda cp: cp.wait())

    @pl.when(i == nt)
    def _():
        zero_ref[...] = jnp.zeros_like(zero_ref)
        tsizes = _run_sizes(t // 2)

        def tail_copy(zo, do, size):
            return pltpu.make_async_copy(zero_ref.at[pl.ds(0, size)], xs_ref.at[pl.ds(do, size)], sems.at[slot])

        def tails(act):
            def body(e, carry):
                _for_each_run_piece(tlen_ref[e], jnp.int32(0), toff_ref[e], tsizes,
                                    lambda zo, do, size: act(tail_copy(zo, do, size)))
                return carry
            lax.fori_loop(0, N_EXPERTS, body, 0)

            def unused(j, carry):
                act(tail_copy(0, pl.multiple_of(j * t, t), t))
                return carry
            lax.fori_loop(nu_ref[0], n_tiles, unused, 0)

        tails(lambda cp: cp.start())
        tails(lambda cp: cp.wait())


def _ffn_kernel(te_ref, nu_ref, xs_ref, wg_ref, wu_ref, wd_ref, y_ref, *, d):
    i = pl.program_id(0)

    @pl.when(i < nu_ref[0])
    def _():
        x = xs_ref[:, :d].astype(BF16)
        g = _dot(x, wg_ref[0])
        u = _dot(x, wu_ref[0])
        a = (g * _sigmoid(g) * u).astype(BF16)
        y = _dot(a, wd_ref[0])
        wrow = xs_ref[:, d:]
        y_ref[...] = y * jnp.concatenate([wrow] * (d // W_COLS), axis=1)

    @pl.when(i >= nu_ref[0])
    def _():
        y_ref[...] = jnp.zeros_like(y_ref)


def _combine_kernel(p_ref, loc_ref, dst_ref, pos_ref, x2_ref, gfin_ref, y_hbm, out_ref, yloc_ref, sems,
                    *, nt, bt, sl):
    i = pl.program_id(0)
    slot = i % 2
    sizes = _run_sizes(bt)

    def runs(tile, buf, act):
        def body(e, carry):
            idx = tile * N_EXPERTS + e
            _for_each_run_piece(
                p_ref[idx], loc_ref[idx], dst_ref[idx], sizes,
                lambda lo, do, size: act(pltpu.make_async_copy(
                    y_hbm.at[pl.ds(do, size)], yloc_ref.at[buf, pl.ds(lo, size)], sems.at[buf])))
            return carry
        lax.fori_loop(0, N_EXPERTS, body, 0)

    @pl.when(i == 0)
    def _():
        yloc_ref[...] = jnp.zeros_like(yloc_ref)
        runs(0, 0, lambda cp: cp.start())

    @pl.when(i + 1 < nt)
    def _():
        runs(i + 1, 1 - slot, lambda cp: cp.start())

    runs(i, slot, lambda cp: cp.wait())
    pos = pos_ref[...]
    pos0 = _col_from_row(pos[0:1], bt)
    pos1 = _col_from_row(pos[1:2], bt)
    scol = lax.broadcasted_iota(I32, (bt, sl), 1).astype(F32)
    perm_t = jnp.where(scol == pos0, 1.0, jnp.where(scol == pos1, 1.0, 0.0)).astype(BF16)
    moe = _dot(perm_t, yloc_ref[slot].astype(BF16))
    out_ref[...] = _rmsnorm(x2_ref[...] + moe, gfin_ref[...])


def _moe_prompt(x2, h2, eid, wsel, wg, wu, wd, g_fin):
    n, d = x2.shape
    bt, t = MOE_TOKENS, MOE_ROWS
    assert n % bt == 0
    nt = n // bt
    sl = -(-(2 * bt + N_EXPERTS * (RUN_ALIGN - 1)) // LANES) * LANES
    onehot = eid.reshape(2, nt, bt)[..., None] == jnp.arange(N_EXPERTS, dtype=I32)
    cnt = jnp.sum(onehot, axis=(0, 2), dtype=I32)
    p = (cnt + RUN_ALIGN - 1) // RUN_ALIGN * RUN_ALIGN
    loc = jnp.cumsum(p, axis=1) - p
    tot = jnp.sum(p, axis=0)
    reg = (tot + t - 1) // t * t
    reg_end = jnp.cumsum(reg)
    reg_start = reg_end - reg
    dst = reg_start[None, :] + jnp.cumsum(p, axis=0) - p
    n_tiles = (2 * n + nt * N_EXPERTS * (RUN_ALIGN - 1) + N_EXPERTS * (t - 1)) // t + 1
    s_max = n_tiles * t
    n_used = (reg_end[-1] // t).astype(I32).reshape(1)
    tile_start = jnp.arange(n_tiles, dtype=I32) * t
    tile_expert = jnp.minimum(jnp.sum(reg_end[None, :] <= tile_start[:, None], axis=1, dtype=I32), N_EXPERTS - 1)
    p_f, loc_f, dst_f = p.reshape(-1), loc.reshape(-1), dst.reshape(-1).astype(I32)
    locf = loc.astype(F32).reshape(nt, N_EXPERTS, 1)
    triu = jnp.asarray(np.triu(np.ones((bt, bt), np.float32), 1), BF16)

    last = nt - 1
    xs, pos = pl.pallas_call(
        functools.partial(_dispatch_kernel, nt=nt, bt=bt, sl=sl, d=d, n_tiles=n_tiles),
        grid_spec=pltpu.PrefetchScalarGridSpec(
            num_scalar_prefetch=6, grid=(nt + 1,),
            in_specs=[pl.BlockSpec((bt, d), lambda i, *_: (jnp.minimum(i, last), 0)),
                      pl.BlockSpec((2, bt), lambda i, *_: (0, jnp.minimum(i, last))),
                      pl.BlockSpec((2, bt), lambda i, *_: (0, jnp.minimum(i, last))),
                      pl.BlockSpec((1, N_EXPERTS, 1), lambda i, *_: (jnp.minimum(i, last), 0, 0)),
                      pl.BlockSpec((bt, bt), lambda i, *_: (0, 0))],
            out_specs=(pl.BlockSpec(memory_space=pl.ANY),
                       pl.BlockSpec((2, bt), lambda i, *_: (0, jnp.minimum(i, last)))),
            scratch_shapes=[pltpu.VMEM((2, sl, d + W_COLS), F32), pltpu.VMEM((t, d + W_COLS), F32),
                            pltpu.SemaphoreType.DMA((2,))]),
        out_shape=(jax.ShapeDtypeStruct((s_max, d + W_COLS), F32), jax.ShapeDtypeStruct((2, n), F32)),
        compiler_params=_cparams(("arbitrary",)), name="moe_dispatch",
    )(p_f, loc_f, dst_f, (reg_start + tot).astype(I32), (reg - tot).astype(I32), n_used, h2, eid, wsel, locf, triu)

    de = wg.shape[2]
    used = lambda i, te, nu: jnp.minimum(i, nu[0] - 1)
    y = pl.pallas_call(
        functools.partial(_ffn_kernel, d=d),
        grid_spec=pltpu.PrefetchScalarGridSpec(
            num_scalar_prefetch=2, grid=(n_tiles,),
            in_specs=[pl.BlockSpec((t, d + W_COLS), lambda i, te, nu: (used(i, te, nu), 0)),
                      pl.BlockSpec((1, d, de), lambda i, te, nu: (te[used(i, te, nu)], 0, 0)),
                      pl.BlockSpec((1, d, de), lambda i, te, nu: (te[used(i, te, nu)], 0, 0)),
                      pl.BlockSpec((1, de, d), lambda i, te, nu: (te[used(i, te, nu)], 0, 0))],
            out_specs=pl.BlockSpec((t, d), lambda i, te, nu: (i, 0))),
        out_shape=jax.ShapeDtypeStruct((s_max, d), F32),
        compiler_params=_cparams(("arbitrary",)), name="moe_ffn",
    )(tile_expert, n_used, xs, wg, wu, wd)

    return pl.pallas_call(
        functools.partial(_combine_kernel, nt=nt, bt=bt, sl=sl),
        grid_spec=pltpu.PrefetchScalarGridSpec(
            num_scalar_prefetch=3, grid=(nt,),
            in_specs=[pl.BlockSpec((2, bt), lambda i, *_: (0, i)),
                      pl.BlockSpec((bt, d), lambda i, *_: (i, 0)),
                      pl.BlockSpec((1, d), lambda i, *_: (0, 0)),
                      pl.BlockSpec(memory_space=pl.ANY)],
            out_specs=pl.BlockSpec((bt, d), lambda i, *_: (i, 0)),
            scratch_shapes=[pltpu.VMEM((2, sl, d), F32), pltpu.SemaphoreType.DMA((2,))]),
        out_shape=jax.ShapeDtypeStruct((n, d), F32),
        compiler_params=_cparams(("arbitrary",)), name="moe_combine",
    )(p_f, loc_f, dst_f, pos, x2, g_fin, y)


def _moe_dense_kernel(x2_ref, h2_ref, eid_ref, wsel_ref, gfin_ref, wg_ref, wu_ref, wd_ref, out_ref, acc_ref):
    e = pl.program_id(0)
    n = x2_ref.shape[0]

    @pl.when(e == 0)
    def _():
        acc_ref[...] = jnp.zeros_like(acc_ref)

    eid = eid_ref[...]
    wsel = wsel_ref[...]
    gate_row = jnp.where(eid[0:1] == e, wsel[0:1], 0.0) + jnp.where(eid[1:2] == e, wsel[1:2], 0.0)
    gate = _col_from_row(gate_row, n)
    x = h2_ref[...]
    g = _dot(x, wg_ref[0])
    u = _dot(x, wu_ref[0])
    a = (g * _sigmoid(g) * u).astype(BF16)
    acc_ref[...] += gate * _dot(a, wd_ref[0])

    @pl.when(e == pl.num_programs(0) - 1)
    def _():
        out_ref[...] = _rmsnorm(x2_ref[...] + acc_ref[...], gfin_ref[...])


def _moe_dense(x2, h2, eid, wsel, wg, wu, wd, g_fin):
    n, d = x2.shape
    de = wg.shape[2]
    full = lambda a: pl.BlockSpec(a.shape, lambda e: (0,) * a.ndim)
    return pl.pallas_call(
        _moe_dense_kernel, grid=(wg.shape[0],),
        in_specs=[full(x2), full(h2), full(eid), full(wsel), full(g_fin),
                  pl.BlockSpec((1, d, de), lambda e: (e, 0, 0)),
                  pl.BlockSpec((1, d, de), lambda e: (e, 0, 0)),
                  pl.BlockSpec((1, de, d), lambda e: (e, 0, 0))],
        out_specs=pl.BlockSpec((n, d), lambda e: (0, 0)),
        out_shape=jax.ShapeDtypeStruct((n, d), F32),
        scratch_shapes=[pltpu.VMEM((n, d), F32)],
        compiler_params=_cparams(("arbitrary",)), name="moe_dense",
    )(x2, h2, eid, wsel, g_fin, wg, wu, wd)


def _layer_weights(norm_mix_g, w_in, fox_f_bias, gla_w_a2, gla_b_a, gla_norm_g, w_fox_out, w_gla_out, w_o,
                   norm_ffn_g, w_group_router, b_group_router, w_expert_router, b_expert_router):
    d = w_in.shape[0]
    c = 0
    parts = {}
    for name, width in (("q", FOX_WIDTH), ("k", FOX_WIDTH), ("v", FOX_WIDTH), ("ff", FOX_HEADS),
                        ("qg", GLA_K), ("kg", GLA_K), ("vg", GLA_V), ("lr", GLA_RANK), ("r", GLA_V),
                        ("gf", d), ("gg", d)):
        parts[name] = w_in[:, c:c + width]
        c += width
    assert c == w_in.shape[1]
    g_mix = norm_mix_g.reshape(1, d)
    w_fft = jnp.zeros((2 * SUBLANES, d), F32).at[:FOX_HEADS].set(parts["ff"].T).astype(BF16)
    pw = dict(
        g_mix=g_mix,
        w_qkv=jnp.concatenate([parts["q"], parts["k"], parts["v"]], axis=1).astype(BF16),
        w_fft=w_fft,
        w_gla=jnp.concatenate([parts["qg"], parts["kg"], parts["vg"]], axis=1).astype(BF16),
        w_lr=parts["lr"].astype(BF16),
        w_a2=gla_w_a2.astype(BF16),
        b_a=gla_b_a.reshape(1, GLA_K),
        f_bias=fox_f_bias.reshape(FOX_HEADS, 1),
    )
    w_rt = jnp.zeros((SUBLANES + N_EXPERTS, d), F32)
    w_rt = w_rt.at[:N_GROUPS].set(w_group_router.T).at[SUBLANES:].set(w_expert_router.T)
    b_rt = jnp.zeros((SUBLANES + N_EXPERTS, 1), F32)
    b_rt = b_rt.at[:N_GROUPS, 0].set(b_group_router).at[SUBLANES:, 0].set(b_expert_router)
    mw = dict(
        g_mix=g_mix,
        w_rg=jnp.concatenate([parts["r"], parts["gf"], parts["gg"]], axis=1).astype(BF16),
        w_fo=w_fox_out.astype(BF16), w_go=w_gla_out.astype(BF16), w_o=w_o.astype(BF16),
        g_gla=gla_norm_g.reshape(1, GLA_V), g_ffn=norm_ffn_g.reshape(1, d),
        w_rt=jnp.stack(_split3(w_rt)), b_rt=b_rt,
    )
    return pw, mw


def kernel(x_prompt, x_sample, cache_k, cache_v, cache_log_f, state_gla, page_table, meta_tokens, norm_mix_g,
           w_in, fox_f_bias, gla_w_a2, gla_b_a, gla_norm_g, w_fox_out, w_gla_out, w_o, norm_ffn_g,
           w_group_router, b_group_router, w_expert_router, b_expert_router, w_expert_gate, w_expert_up,
           w_expert_down, norm_final_g):
    depth = w_in.shape[0]
    assert depth == 1, "meta rows are only carried through one layer"
    b, seq, d = x_prompt.shape
    db, t_dec, _ = x_sample.shape
    n_phys = cache_k.shape[1]
    pw, mw = _layer_weights(norm_mix_g[0], w_in[0], fox_f_bias[0], gla_w_a2[0], gla_b_a[0], gla_norm_g[0],
                            w_fox_out[0], w_gla_out[0], w_o[0], norm_ffn_g[0], w_group_router[0],
                            b_group_router[0], w_expert_router[0], b_expert_router[0])
    wg = w_expert_gate[0].astype(BF16)
    wu = w_expert_up[0].astype(BF16)
    wd = w_expert_down[0].astype(BF16)
    g_fin = norm_final_g.reshape(1, d)

    (_, k_m, v_m, kb_m, vb_m, lft_m, qg_m, kg_m, vg_m, la_m) = _project(
        meta_tokens.reshape(1, N_META, d), N_META, pw)
    zero_state = jnp.zeros((1, GLA_HEADS // 2, LANES, GLA_VAL_DIM), F32)
    _, s_meta = _gla(qg_m, kg_m, vg_m, la_m, zero_state, N_META)

    q, k, v, kb, vb, lft, qg, kg, vg, la = _project(x_prompt, min(PROJ_TILE, seq), pw)
    o_fox = _fox_prompt(q, kb, vb, lft, kb_m[0], vb_m[0], lft_m[0], min(ATTN_TILE, seq))
    o_gla, s_gla_p = _gla(qg, kg, vg, la, s_meta, min(GLA_CHUNK, seq))
    n = b * seq
    x2, h2, eid, wsel = _merge(x_prompt.reshape(n, d), o_fox.reshape(n, FOX_WIDTH), o_gla.reshape(n, GLA_V),
                               mw, min(MERGE_TILE, n))
    y_prompt = _moe_prompt(x2, h2, eid, wsel, wg, wu, wd, g_fin).reshape(b, seq, d)

    ns = db * t_dec
    qs, ks, vs, kbs, vbs, lfts, qgs, kgs, vgs, las = _project(x_sample.reshape(1, ns, d), ns, pw)
    tok = lambda a: a.reshape(db, t_dec, a.shape[-1])
    lft_s = lfts.reshape(FOX_HEADS, db, t_dec).transpose(1, 0, 2)
    o_fox_s = _fox_sample(tok(qs), tok(kbs), tok(vbs), lft_s, cache_k[0].transpose(0, 2, 3, 1),
                          cache_v[0].transpose(0, 2, 3, 1), cache_log_f[0].transpose(0, 2, 1), page_table)
    s0 = state_gla[0].reshape(db, GLA_HEADS // 2, LANES, GLA_VAL_DIM)
    o_gla_s, s_gla_s = _gla(tok(qgs), tok(kgs), tok(vgs), tok(las), s0, t_dec)
    x2s, h2s, eids, wsels = _merge(x_sample.reshape(ns, d), o_fox_s.reshape(ns, FOX_WIDTH),
                                   o_gla_s.reshape(ns, GLA_V), mw, ns)
    y_sample = _moe_dense(x2s, h2s, eids, wsels, wg, wu, wd, g_fin).reshape(db, t_dec, d)

    heads = lambda a: a.reshape(a.shape[:-1] + (FOX_HEADS, FOX_HEAD_DIM))
    with_meta = lambda m, r: jnp.concatenate([jnp.broadcast_to(m, (b,) + m.shape[1:]), r], axis=1)
    new_k_prompt = heads(with_meta(k_m, k))[None]
    new_v_prompt = heads(with_meta(v_m, v))[None]
    new_log_f_prompt = with_meta(lft_m.transpose(0, 2, 1), lft.transpose(0, 2, 1))[None]
    new_gla_prompt = s_gla_p.reshape(1, b, GLA_HEADS, GLA_KEY_DIM, GLA_VAL_DIM)
    new_k_sample = heads(tok(ks))[None]
    new_v_sample = heads(tok(vs))[None]
    new_log_f_sample = lfts[0].T.reshape(1, db, t_dec, FOX_HEADS)
    new_gla_sample = s_gla_s.reshape(1, db, GLA_HEADS, GLA_KEY_DIM, GLA_VAL_DIM)
    return (y_prompt, y_sample, new_k_prompt, new_v_prompt, new_log_f_prompt, new_gla_prompt,
            new_k_sample, new_v_sample, new_log_f_sample, new_gla_sample)
```

```python
import functools

import numpy as np
import jax
import jax.numpy as jnp
from jax import lax
from jax.experimental import pallas as pl
from jax.experimental.pallas import tpu as pltpu

F32 = jnp.float32
BF16 = jnp.bfloat16
I32 = jnp.int32
EPS = 1e-6
LOG2E = 1.4426950408889634
N_META = 16
FOX_HEADS = 8
FOX_HEAD_DIM = 64
FOX_WIDTH = FOX_HEADS * FOX_HEAD_DIM
GLA_HEADS = 4
GLA_KEY_DIM = 64
GLA_VAL_DIM = 128
GLA_K = GLA_HEADS * GLA_KEY_DIM
GLA_V = GLA_HEADS * GLA_VAL_DIM
GLA_RANK = 16
GLA_TAU = 16.0
GLA_CHUNK = 128
N_GROUPS = 4
EXPERTS_PER_GROUP = 8
N_EXPERTS = N_GROUPS * EXPERTS_PER_GROUP
PAGE_SIZE = 128

LANES = 128
SUBLANES = 8
VMEM_LIMIT = 56 * 1024 * 1024

PROJ_TILE = 512
ATTN_TILE = 512
MERGE_TILE = 512
MOE_TOKENS = 512
MOE_ROWS = 512
RUN_ALIGN = SUBLANES
PAGES_PER_STEP = 16
GLA_SEQS_PER_STEP = 4
W_COLS = LANES


def _cparams(sem):
    return pltpu.CompilerParams(dimension_semantics=sem, vmem_limit_bytes=VMEM_LIMIT)


def _dot(a, b):
    return jnp.dot(a, b, preferred_element_type=F32)


def _dot_nt(a, b):
    return lax.dot_general(a, b, (((1,), (1,)), ((), ())), preferred_element_type=F32)


def _dot_tn(a, b):
    return lax.dot_general(a, b, (((0,), (0,)), ((), ())), preferred_element_type=F32)


def _split3(a):
    hi = a.astype(BF16)
    r1 = a - hi.astype(F32)
    mid = r1.astype(BF16)
    lo = (r1 - mid.astype(F32)).astype(BF16)
    return hi, mid, lo


def _dot_sel_l(m, a):
    hi, mid, lo = _split3(a)
    return _dot(m, lo) + _dot(m, mid) + _dot(m, hi)


def _dot_sel_r(a, m):
    hi, mid, lo = _split3(a)
    return _dot(lo, m) + _dot(mid, m) + _dot(hi, m)


def _rmsnorm(x, g):
    return x * lax.rsqrt(jnp.mean(x * x, axis=-1, keepdims=True) + EPS) * g


def _log_sigmoid(x):
    return jnp.minimum(x, 0.0) - jnp.log1p(jnp.exp(-jnp.abs(x)))


def _sigmoid(x):
    return 1.0 / (1.0 + jnp.exp(-x))


def _col_from_row(row, n):
    r = lax.broadcasted_iota(I32, (n, n), 0)
    c = lax.broadcasted_iota(I32, (n, n), 1)
    return jnp.sum(jnp.where(r == c, jnp.broadcast_to(row, (n, n)), 0.0), axis=1, keepdims=True)


def _proj_kernel(x_ref, g_ref, wqkv_ref, wfft_ref, wgla_ref, wlr_ref, wa2_ref, ba_ref, fb_ref,
                 wvt_ref, wff_ref, fbr_ref,
                 q_ref, k_ref, v_ref, kb_ref, vb_ref, lft_ref, qg_ref, kg_ref, vg_ref, la_ref, vt_ref, lf_ref):
    h = _rmsnorm(x_ref[0], g_ref[...]).astype(BF16)
    z = _dot(h, wqkv_ref[...])
    q_ref[0] = (z[:, :FOX_WIDTH] * (FOX_HEAD_DIM ** -0.5 * LOG2E)).astype(BF16)
    k = z[:, FOX_WIDTH:2 * FOX_WIDTH]
    v = z[:, 2 * FOX_WIDTH:]
    k_ref[0] = k
    v_ref[0] = v
    kb_ref[0] = k.astype(BF16)
    vb_ref[0] = v.astype(BF16)
    vt_ref[0, 0] = _dot_nt(wvt_ref[...], h).astype(BF16)
    fft = _dot_nt(wfft_ref[...], h)[:FOX_HEADS]
    lft_ref[0] = _log_sigmoid(fft + fb_ref[...])
    lf_ref[0] = _log_sigmoid(_dot(h, wff_ref[...]) + fbr_ref[...])
    zg = _dot(h, wgla_ref[...])
    qg_ref[0] = zg[:, :GLA_K] * GLA_KEY_DIM ** -0.5
    kg_ref[0] = zg[:, GLA_K:2 * GLA_K]
    vg_ref[0] = zg[:, 2 * GLA_K:]
    lr = _dot(h, wlr_ref[...])
    xa = _dot(lr.astype(BF16), wa2_ref[...]) + ba_ref[...]
    la_ref[0] = _log_sigmoid(xa) / GLA_TAU


def _project(x3, tm, pw):
    bx, lx, d = x3.shape
    assert lx % tm == 0
    grid = (bx, lx // tm)
    row = lambda w: pl.BlockSpec((1, tm, w), lambda b, i: (b, i, 0))
    full = lambda a: pl.BlockSpec(a.shape, lambda b, i: (0,) * a.ndim)
    ws = (pw["g_mix"], pw["w_qkv"], pw["w_fft"], pw["w_gla"], pw["w_lr"], pw["w_a2"], pw["b_a"], pw["f_bias"],
          pw["w_vt"], pw["w_ff"], pw["f_bias_row"])
    out_shape = (
        jax.ShapeDtypeStruct((bx, lx, FOX_WIDTH), BF16),
        jax.ShapeDtypeStruct((bx, lx, FOX_WIDTH), F32),
        jax.ShapeDtypeStruct((bx, lx, FOX_WIDTH), F32),
        jax.ShapeDtypeStruct((bx, lx, FOX_WIDTH), BF16),
        jax.ShapeDtypeStruct((bx, lx, FOX_WIDTH), BF16),
        jax.ShapeDtypeStruct((bx, FOX_HEADS, lx), F32),
        jax.ShapeDtypeStruct((bx, lx, GLA_K), F32),
        jax.ShapeDtypeStruct((bx, lx, GLA_K), F32),
        jax.ShapeDtypeStruct((bx, lx, GLA_V), F32),
        jax.ShapeDtypeStruct((bx, lx, GLA_K), F32),
        jax.ShapeDtypeStruct((bx, lx // tm, FOX_WIDTH, tm), BF16),
        jax.ShapeDtypeStruct((bx, lx, FOX_HEADS), F32),
    )
    out_specs = (row(FOX_WIDTH), row(FOX_WIDTH), row(FOX_WIDTH), row(FOX_WIDTH), row(FOX_WIDTH),
                 pl.BlockSpec((1, FOX_HEADS, tm), lambda b, i: (b, 0, i)),
                 row(GLA_K), row(GLA_K), row(GLA_V), row(GLA_K),
                 pl.BlockSpec((1, 1, FOX_WIDTH, tm), lambda b, i: (b, i, 0, 0)), row(FOX_HEADS))
    return pl.pallas_call(
        _proj_kernel, grid=grid,
        in_specs=[row(d)] + [full(a) for a in ws],
        out_specs=out_specs, out_shape=out_shape,
        compiler_params=_cparams(("parallel", "parallel")), name="proj",
    )(x3, *ws)


def _online_update_t(s, vt, state):
    m, l, acc = state
    m_new = jnp.maximum(m, jnp.max(s, axis=0, keepdims=True))
    alpha = jnp.exp2(m - m_new)
    p = jnp.exp2(s - m_new)
    l = alpha * l + jnp.sum(p, axis=0, keepdims=True)
    acc = alpha * acc + _dot(vt, p.astype(BF16))
    return m_new, l, acc


def _fox_prompt_kernel(q_ref, kb_ref, vt_ref, lf_ref, km_ref, vtm_ref, lfm_ref, tri_ref, um_ref,
                       o_ref, c_ref, s00_ref, s01_ref, s10_ref, s11_ref, *, tq):
    qi = pl.program_id(1)
    n_blk = kb_ref.shape[1] // tq
    dh = FOX_HEAD_DIM
    s_refs = ((s00_ref, s01_ref), (s10_ref, s11_ref))

    @pl.when(qi == 0)
    def _():
        carry = jnp.zeros((1, FOX_HEADS), F32)
        for jb in range(n_blk):
            for js in range(tq // LANES):
                off = jb * tq + js * LANES
                cs = _dot_sel_l(tri_ref[...], lf_ref[0, off:off + LANES, :]) + carry
                c_ref[jb, js * LANES:(js + 1) * LANES, :] = cs * LOG2E
                carry = cs[LANES - 1:LANES, :]

    bias_meta = _dot_sel_l(um_ref[...], lfm_ref[...]) * LOG2E
    lane = lax.broadcasted_iota(I32, (tq, LANES), 1)
    key = lax.broadcasted_iota(I32, (tq, tq), 0)
    qry = lax.broadcasted_iota(I32, (tq, tq), 1)
    in_a = lane < dh
    for hp in range(FOX_HEADS // 2):
        sl = slice(hp * LANES, (hp + 1) * LANES)
        ha, hb = 2 * hp, 2 * hp + 1
        qp = q_ref[0, :, sl].astype(F32)
        qa = jnp.where(in_a, qp, 0.0).astype(BF16)
        qb = jnp.where(in_a, 0.0, qp).astype(BF16)
        km = km_ref[:, sl]

        def first(qh, h, km=km):
            s = _dot_nt(km, qh) + bias_meta[:, h:h + 1]
            m = jnp.max(s, axis=0, keepdims=True)
            p = jnp.exp2(s - m)
            return m, jnp.sum(p, axis=0, keepdims=True), _dot(vtm_ref[h * dh:(h + 1) * dh, :], p.astype(BF16))

        def score(j, buf, qa=qa, qb=qb, ha=ha, hb=hb, sl=sl):
            start = pl.multiple_of(j * tq, tq)
            kc = kb_ref[0, pl.ds(start, tq), sl]
            cj = c_ref[j]
            s_refs[buf][0][...] = _dot_nt(kc, qa) - cj[:, ha:ha + 1]
            s_refs[buf][1][...] = _dot_nt(kc, qb) - cj[:, hb:hb + 1]

        def advance(j, buf, states, masked, ha=ha, hb=hb):
            out = []
            for sub, h in ((0, ha), (1, hb)):
                s = s_refs[buf][sub][...]
                if masked:
                    s = jnp.where(key <= qry, s, -jnp.inf)
                out.append(_online_update_t(s, vt_ref[0, j, h * dh:(h + 1) * dh, :], states[sub]))
            return tuple(out)

        def two_blocks(i, states):
            j = 2 * i
            score(j + 1, 1)
            states = advance(j, 0, states, False)
            score(j + 2, 0)
            return advance(j + 1, 1, states, False)

        def last_from_even(states):
            return advance(qi, 0, states, True)

        def last_from_odd(states):
            score(qi, 1)
            return advance(qi, 1, advance(qi - 1, 0, states, False), True)

        score(0, 0)
        states = lax.fori_loop(0, qi // 2, two_blocks, (first(qa, ha), first(qb, hb)))
        (_, la, acc_a), (_, lb, acc_b) = lax.cond(qi % 2 == 0, last_from_even, last_from_odd, states)
        o_t = jnp.concatenate([acc_a / la, acc_b / lb], axis=0)
        o_ref[0, :, sl] = o_t.T.astype(BF16)


def _fox_prompt(q, kb, vt, lf, kb_meta, vt_meta, lf_meta, tq):
    b, l, w = q.shape
    assert l % tq == 0 and vt.shape == (b, l // tq, w, tq)
    n_blk = l // tq
    tri = jnp.asarray(np.tril(np.ones((LANES, LANES), np.float32)), BF16)
    um = jnp.asarray(np.triu(np.ones((N_META, N_META), np.float32), 1), BF16)
    full = lambda a: pl.BlockSpec(a.shape, lambda bi, qi: (0,) * a.ndim)
    return pl.pallas_call(
        functools.partial(_fox_prompt_kernel, tq=tq), grid=(b, n_blk),
        in_specs=[pl.BlockSpec((1, tq, w), lambda bi, qi: (bi, qi, 0)),
                  pl.BlockSpec((1, l, w), lambda bi, qi: (bi, 0, 0)),
                  pl.BlockSpec((1, n_blk, w, tq), lambda bi, qi: (bi, 0, 0, 0)),
                  pl.BlockSpec((1, l, FOX_HEADS), lambda bi, qi: (bi, 0, 0)),
                  full(kb_meta), full(vt_meta), full(lf_meta), full(tri), full(um)],
        out_specs=pl.BlockSpec((1, tq, w), lambda bi, qi: (bi, qi, 0)),
        out_shape=jax.ShapeDtypeStruct((b, l, w), BF16),
        scratch_shapes=[pltpu.VMEM((n_blk, tq, FOX_HEADS), F32)] + [pltpu.VMEM((tq, tq), F32)] * 4,
        compiler_params=_cparams(("parallel", "arbitrary")), name="fox_prompt",
    )(q, kb, vt, lf, kb_meta, vt_meta, lf_meta, tri, um)


def _fox_sample_kernel(pt_ref, q_ref, kn_ref, vn_ref, lfn_ref, t8_ref, u_ref, *rest, g, t):
    ck = rest[:g]
    cv = rest[g:2 * g]
    clf = rest[2 * g:3 * g]
    o_ref, qbd_ref, m_ref, l_ref, acc_ref, r_ref = rest[3 * g:]
    gi = pl.program_id(1)
    rows = t * FOX_HEADS
    own = (lax.broadcasted_iota(I32, (rows, FOX_WIDTH), 0) % FOX_HEADS
           == lax.broadcasted_iota(I32, (rows, FOX_WIDTH), 1) // FOX_HEAD_DIM)

    @pl.when(gi == 0)
    def _():
        q = q_ref[0].astype(F32)
        qt = jnp.concatenate([jnp.broadcast_to(q[i:i + 1], (FOX_HEADS, FOX_WIDTH)) for i in range(t)], axis=0)
        qbd = jnp.where(own, qt, 0.0).astype(BF16)
        qbd_ref[...] = qbd
        a_t = _dot_sel_r(lfn_ref[0], t8_ref[...])
        s = _dot_nt(qbd, kn_ref[0]) - jnp.concatenate([a_t * LOG2E] * t, axis=0)
        rt = lax.broadcasted_iota(I32, (rows, t), 0) // FOX_HEADS
        ct = lax.broadcasted_iota(I32, (rows, t), 1)
        s = jnp.where(ct <= rt, s, -jnp.inf)
        m = jnp.max(s, axis=1, keepdims=True)
        p = jnp.exp2(s - m)
        m_ref[...] = m
        l_ref[...] = jnp.sum(p, axis=1, keepdims=True)
        acc_ref[...] = _dot(p.astype(BF16), vn_ref[0])
        r_ref[...] = jnp.zeros_like(r_ref)

    lf_all = jnp.concatenate([clf[jj][0] for jj in range(g)], axis=0)
    after = _dot_sel_r(lf_all, u_ref[...])
    r_run = r_ref[:, 0:1]
    qbd = qbd_ref[...]
    scores = [None] * g
    for jj in reversed(range(g)):
        bias = (after[jj * FOX_HEADS:(jj + 1) * FOX_HEADS] + r_run) * LOG2E
        r_run = r_run + jnp.sum(clf[jj][0], axis=1, keepdims=True)
        kt = ck[jj][0].reshape(FOX_WIDTH, PAGE_SIZE).astype(BF16)
        scores[jj] = _dot(qbd, kt) + jnp.concatenate([bias] * t, axis=0)
    r_ref[...] = jnp.broadcast_to(r_run, r_ref.shape)
    m = m_ref[...]
    m_new = m
    for s in scores:
        m_new = jnp.maximum(m_new, jnp.max(s, axis=1, keepdims=True))
    alpha = jnp.exp2(m - m_new)
    l = alpha * l_ref[...]
    acc = alpha * acc_ref[...]
    for jj, s in enumerate(scores):
        p = jnp.exp2(s - m_new)
        l = l + jnp.sum(p, axis=1, keepdims=True)
        acc = acc + _dot_nt(p.astype(BF16), cv[jj][0].reshape(FOX_WIDTH, PAGE_SIZE).astype(BF16))
    m_ref[...] = m_new
    l_ref[...] = l
    acc_ref[...] = acc

    @pl.when(gi == pl.num_programs(1) - 1)
    def _():
        o = jnp.where(own, acc / l, 0.0)
        o_ref[0] = jnp.concatenate(
            [jnp.sum(o[i * FOX_HEADS:(i + 1) * FOX_HEADS], axis=0, keepdims=True) for i in range(t)],
            axis=0).astype(BF16)


def _fox_sample(q, kb_new, vb_new, lft_new, cache_kt, cache_vt, cache_lft, page_table):
    db, t, w = q.shape
    n_pages = page_table.shape[1]
    g = min(PAGES_PER_STEP, n_pages)
    assert n_pages % g == 0
    ng = n_pages // g
    rows = FOX_HEADS * t
    t8 = jnp.asarray(np.triu(np.ones((t, t), np.float32)), BF16)
    u = jnp.asarray(np.tril(np.ones((PAGE_SIZE, PAGE_SIZE), np.float32), -1), BF16)
    tok = lambda wd: pl.BlockSpec((1, t, wd), lambda b, gi, pt: (b, 0, 0))
    full = lambda a: pl.BlockSpec(a.shape, lambda b, gi, pt: (0,) * a.ndim)

    def page(shape, jj):
        zeros = (0,) * (len(shape) - 1)
        return pl.BlockSpec(shape, lambda b, gi, pt: (pt[b * n_pages + (ng - 1 - gi) * g + jj],) + zeros)

    kv_page = (1, FOX_HEADS, FOX_HEAD_DIM, PAGE_SIZE)
    in_specs = ([tok(w), tok(w), tok(w), pl.BlockSpec((1, FOX_HEADS, t), lambda b, gi, pt: (b, 0, 0)),
                 full(t8), full(u)]
                + [page(kv_page, jj) for jj in range(g)]
                + [page(kv_page, jj) for jj in range(g)]
                + [page((1, FOX_HEADS, PAGE_SIZE), jj) for jj in range(g)])
    grid_spec = pltpu.PrefetchScalarGridSpec(
        num_scalar_prefetch=1, grid=(db, ng), in_specs=in_specs,
        out_specs=pl.BlockSpec((1, t, w), lambda b, gi, pt: (b, 0, 0)),
        scratch_shapes=[pltpu.VMEM((rows, w), BF16), pltpu.VMEM((rows, 1), F32), pltpu.VMEM((rows, 1), F32),
                        pltpu.VMEM((rows, w), F32), pltpu.VMEM((FOX_HEADS, LANES), F32)])
    return pl.pallas_call(
        functools.partial(_fox_sample_kernel, g=g, t=t), grid_spec=grid_spec,
        out_shape=jax.ShapeDtypeStruct((db, t, w), BF16),
        compiler_params=_cparams(("parallel", "arbitrary")), name="fox_sample",
    )(page_table.reshape(-1), q, kb_new, vb_new, lft_new, t8, u,
      *([cache_kt] * g), *([cache_vt] * g), *([cache_lft] * g))


def _gla_tables(c):
    idx = np.arange(c)
    ws = [np.tril(np.ones((c, c), np.float32))]
    ms = [np.eye(c, dtype=np.float32)]
    sz = 2
    while sz <= c:
        seg = idx // sz
        upper = (idx % sz) >= sz // 2
        mid = seg * sz + sz // 2 - 1
        j = idx[None, :]
        r = idx[:, None]
        w_up = (j > mid[:, None]) & (j <= r)
        w_lo = (j > r) & (j <= mid[:, None])
        ws.append(np.where(upper[:, None], w_up, w_lo).astype(np.float32))
        ms.append(((seg[:, None] == seg[None, :]) & upper[:, None] & ~upper[None, :]).astype(np.float32))
        sz *= 2
    return (jnp.asarray(np.concatenate(ws, axis=0), BF16),
            jnp.asarray(np.stack([np.concatenate([m, m], axis=0) for m in ms]), F32))


def _gla_kernel(q_ref, k_ref, v_ref, la_ref, s0_ref, wall_ref, msk_ref, o_ref, sfin_ref, state_ref,
                *, c, n_levels, nb, shared_s0):
    ci = pl.program_id(1)

    @pl.when(ci == 0)
    def _():
        for bi in range(nb):
            state_ref[bi] = s0_ref[0 if shared_s0 else bi]

    lane = lax.broadcasted_iota(I32, (c, LANES), 1)
    r128 = lax.broadcasted_iota(I32, (LANES, LANES), 0)
    half = GLA_KEY_DIM
    in_a = lane < half

    def stack_heads(x):
        return jnp.concatenate([jnp.where(in_a, x, 0.0), jnp.where(in_a, 0.0, x)], axis=0).astype(BF16)

    for bi in range(nb):
        la3 = _split3(la_ref[bi])
        seg = _dot(wall_ref[...], jnp.concatenate(la3, axis=1))
        seg = seg[:, 2 * GLA_K:] + seg[:, GLA_K:2 * GLA_K] + seg[:, :GLA_K]
        b_all = seg[:c]
        e_all = jnp.exp(seg[c:])
        for p in range(GLA_HEADS // 2):
            ksl = slice(p * LANES, (p + 1) * LANES)
            q = q_ref[bi, :, ksl]
            k = k_ref[bi, :, ksl]
            v16 = v_ref[bi, :, 2 * p * LANES:(2 * p + 2) * LANES].astype(BF16)
            sab = msk_ref[0] * _dot_nt(stack_heads(q), k.astype(BF16))
            for lv in range(n_levels):
                e = e_all[lv * c:(lv + 1) * c, ksl]
                sab = sab + msk_ref[lv + 1] * _dot_nt(stack_heads(q * e), (k * e).astype(BF16))
            st = state_ref[bi, p]
            b = b_all[:, ksl]
            inter = _dot(stack_heads(q * jnp.exp(b)), st.astype(BF16))
            s16 = sab.astype(BF16)
            o_ref[bi, :, 2 * p * LANES:(2 * p + 1) * LANES] = inter[:c] + _dot(s16[:c], v16[:, :LANES])
            o_ref[bi, :, (2 * p + 1) * LANES:(2 * p + 2) * LANES] = inter[c:] + _dot(s16[c:], v16[:, LANES:])
            b_end = b[c - 1:c, :]
            kdec = (k * jnp.exp(b_end - b)).astype(BF16)
            upd = _dot_tn(kdec, v16)
            upd = jnp.where(r128 < half, upd[:, :LANES], upd[:, LANES:])
            state_ref[bi, p] = _col_from_row(jnp.exp(b_end), LANES) * st + upd

    @pl.when(ci == pl.num_programs(1) - 1)
    def _():
        sfin_ref[...] = state_ref[...]


def _gla(qg, kg, vg, la, s0, c):
    b, l, _ = qg.shape
    assert l % c == 0
    nb = GLA_SEQS_PER_STEP
    while b % nb:
        nb //= 2
    wall, msk = _gla_tables(c)
    n_levels = msk.shape[0] - 1
    shared_s0 = s0.shape[0] != b
    st_shape = (GLA_HEADS // 2, LANES, GLA_VAL_DIM)
    s0_spec = (pl.BlockSpec((1,) + st_shape, lambda bi, ci: (0, 0, 0, 0)) if shared_s0
               else pl.BlockSpec((nb,) + st_shape, lambda bi, ci: (bi, 0, 0, 0)))
    tok = lambda wd: pl.BlockSpec((nb, c, wd), lambda bi, ci: (bi, ci, 0))
    full = lambda a: pl.BlockSpec(a.shape, lambda bi, ci: (0,) * a.ndim)
    return pl.pallas_call(
        functools.partial(_gla_kernel, c=c, n_levels=n_levels, nb=nb, shared_s0=shared_s0),
        grid=(b // nb, l // c),
        in_specs=[tok(GLA_K), tok(GLA_K), tok(GLA_V), tok(GLA_K), s0_spec, full(wall), full(msk)],
        out_specs=(tok(GLA_V), pl.BlockSpec((nb,) + st_shape, lambda bi, ci: (bi, 0, 0, 0))),
        out_shape=(jax.ShapeDtypeStruct((b, l, GLA_V), F32), jax.ShapeDtypeStruct((b,) + st_shape, F32)),
        scratch_shapes=[pltpu.VMEM((nb,) + st_shape, F32)],
        compiler_params=_cparams(("parallel", "arbitrary")), name="gla",
    )(qg, kg, vg, la, s0, wall, msk)


def _merge_kernel(x_ref, of_ref, og_ref, gmix_ref, wrg_ref, wfo_ref, wgo_ref, wo_ref, gng_ref, gffn_ref,
                  wrt_ref, brt_ref, x2_ref, h2_ref, eid_ref, wsel_ref):
    x = x_ref[...]
    tm, d = x.shape
    h = _rmsnorm(x, gmix_ref[...]).astype(BF16)
    z = _dot(h, wrg_ref[...])
    r = z[:, :GLA_V]
    gate_f = z[:, GLA_V:GLA_V + d]
    gate_g = z[:, GLA_V + d:]
    og = og_ref[...]
    parts = []
    for hh in range(GLA_HEADS):
        blk = og[:, hh * GLA_VAL_DIM:(hh + 1) * GLA_VAL_DIM]
        parts.append(blk * lax.rsqrt(jnp.mean(blk * blk, axis=-1, keepdims=True) + EPS))
    ogn = jnp.concatenate(parts, axis=1)
    ogs = (ogn * gng_ref[...] * (r * _sigmoid(r))).astype(BF16)
    branch_f = _dot(of_ref[...], wfo_ref[...])
    branch_g = _dot(ogs, wgo_ref[...])
    merged = _sigmoid(gate_f) * branch_f + _sigmoid(gate_g) * branch_g
    x2 = x + _dot(merged.astype(BF16), wo_ref[...])
    x2_ref[...] = x2
    h2 = _rmsnorm(x2, gffn_ref[...])
    h2_ref[...] = h2.astype(BF16)

    a = _split3(h2)
    wr = (wrt_ref[0], wrt_ref[1], wrt_ref[2])
    lt = (_dot_nt(wr[1], a[1]) + _dot_nt(wr[0], a[2]) + _dot_nt(wr[2], a[0])
          + _dot_nt(wr[0], a[1]) + _dot_nt(wr[1], a[0]) + _dot_nt(wr[0], a[0])) + brt_ref[...]
    row8 = lax.broadcasted_iota(I32, (SUBLANES, tm), 0)
    gl = jnp.where(row8 < N_GROUPS, lt[:SUBLANES], -jnp.inf)
    gmax = jnp.max(gl, axis=0, keepdims=True)
    g_idx = jnp.min(jnp.where(gl == gmax, row8, SUBLANES), axis=0, keepdims=True)
    g_sel = 1.0 / jnp.sum(jnp.exp(gl - gmax), axis=0, keepdims=True)
    rowe = lax.broadcasted_iota(I32, (N_EXPERTS, tm), 0)
    e1 = jnp.where(rowe // EXPERTS_PER_GROUP == g_idx, lt[SUBLANES:], -jnp.inf)
    v1 = jnp.max(e1, axis=0, keepdims=True)
    i1 = jnp.min(jnp.where(e1 == v1, rowe, N_EXPERTS), axis=0, keepdims=True)
    e2 = jnp.where(rowe == i1, -jnp.inf, e1)
    v2 = jnp.max(e2, axis=0, keepdims=True)
    i2 = jnp.min(jnp.where(e2 == v2, rowe, N_EXPERTS), axis=0, keepdims=True)
    tt = jnp.exp(v2 - v1)
    eid_ref[...] = jnp.concatenate([i1, i2], axis=0)
    wsel_ref[...] = jnp.concatenate([g_sel / (1.0 + tt), g_sel * tt / (1.0 + tt)], axis=0)


def _merge(x, o_fox, o_gla, mw, tm):
    n, d = x.shape
    assert n % tm == 0
    row = lambda wd: pl.BlockSpec((tm, wd), lambda i: (i, 0))
    full = lambda a: pl.BlockSpec(a.shape, lambda i: (0,) * a.ndim, pipeline_mode=pl.Buffered(1))
    ws = (mw["g_mix"], mw["w_rg"], mw["w_fo"], mw["w_go"], mw["w_o"], mw["g_gla"], mw["g_ffn"],
          mw["w_rt"], mw["b_rt"])
    return pl.pallas_call(
        _merge_kernel, grid=(n // tm,),
        in_specs=[row(d), row(FOX_WIDTH), row(GLA_V)] + [full(a) for a in ws],
        out_specs=(row(d), row(d), pl.BlockSpec((2, tm), lambda i: (0, i)), pl.BlockSpec((2, tm), lambda i: (0, i))),
        out_shape=(jax.ShapeDtypeStruct((n, d), F32), jax.ShapeDtypeStruct((n, d), BF16),
                   jax.ShapeDtypeStruct((2, n), I32), jax.ShapeDtypeStruct((2, n), F32)),
        compiler_params=_cparams(("parallel",)), name="merge",
    )(x, o_fox, o_gla, *ws)


def _run_sizes(max_rows):
    sizes = []
    s = RUN_ALIGN
    while s <= max_rows:
        sizes.append(s)
        s *= 2
    return tuple(reversed(sizes))


def _for_each_run_piece(n_rows, a_off, b_off, sizes, fn):
    for size in sizes:
        bit = (n_rows & size) != 0

        @pl.when(bit)
        def _(a_off=a_off, b_off=b_off, size=size):
            fn(pl.multiple_of(a_off, RUN_ALIGN), pl.multiple_of(b_off, RUN_ALIGN), size)

        inc = jnp.where(bit, size, 0)
        a_off = a_off + inc
        b_off = b_off + inc


def _dispatch_kernel(p_ref, loc_ref, dst_ref, toff_ref, tlen_ref, nu_ref,
                     h2_ref, eid_ref, wsel_ref, locf_ref, triu_ref,
                     xs_ref, pos_ref, xloc_ref, zero_ref, sems, *, nt, bt, sl, d, n_tiles):
    i = pl.program_id(0)
    slot = i % 2
    sizes = _run_sizes(bt)
    t = zero_ref.shape[0]

    def runs(tile, buf, act):
        def body(e, carry):
            idx = tile * N_EXPERTS + e
            _for_each_run_piece(
                p_ref[idx], loc_ref[idx], dst_ref[idx], sizes,
                lambda lo, do, size: act(pltpu.make_async_copy(
                    xloc_ref.at[buf, pl.ds(lo, size)], xs_ref.at[pl.ds(do, size)], sems.at[buf])))
            return carry
        lax.fori_loop(0, N_EXPERTS, body, 0)

    @pl.when(i < nt)
    def _():
        eid = eid_ref[...]
        wsel = wsel_ref[...]
        rowe = lax.broadcasted_iota(I32, (N_EXPERTS, bt), 0)
        oh0 = jnp.where(rowe == eid[0:1], 1.0, 0.0)
        oh1 = jnp.where(rowe == eid[1:2], 1.0, 0.0)
        pref0 = _dot(oh0.astype(BF16), triu_ref[...])
        pref1 = _dot(oh1.astype(BF16), triu_ref[...])
        n0 = jnp.sum(oh0, axis=1, keepdims=True)
        loc = locf_ref[0]
        pos0 = jnp.sum(oh0 * (loc + pref0), axis=0, keepdims=True)
        pos1 = jnp.sum(oh1 * (loc + n0 + pref1), axis=0, keepdims=True)
        pos_ref[...] = jnp.concatenate([pos0, pos1], axis=0)
        srow = lax.broadcasted_iota(I32, (sl, bt), 0).astype(F32)
        m0 = srow == pos0
        m1 = srow == pos1
        perm = jnp.where(m0, 1.0, jnp.where(m1, 1.0, 0.0)).astype(BF16)
        wslot = jnp.sum(jnp.where(m0, wsel[0:1], jnp.where(m1, wsel[1:2], 0.0)), axis=1, keepdims=True)
        xloc_ref[slot, :, :d] = _dot(perm, h2_ref[...])
        xloc_ref[slot, :, d:] = jnp.broadcast_to(wslot, (sl, W_COLS))
        runs(i, slot, lambda cp: cp.start())

    @pl.when(i > 0)
    def _():
        runs(i - 1, 1 - slot, lambda cp: cp.wait())

    @pl.when(i == nt)
    def _():
        zero_ref[...] = jnp.zeros_like(zero_ref)
        tsizes = _run_sizes(t // 2)

        def tail_copy(zo, do, size):
            return pltpu.make_async_copy(zero_ref.at[pl.ds(0, size)], xs_ref.at[pl.ds(do, size)], sems.at[slot])

        def tails(act):
            def body(e, carry):
                _for_each_run_piece(tlen_ref[e], jnp.int32(0), toff_ref[e], tsizes,
                                    lambda zo, do, size: act(tail_copy(zo, do, size)))
                return carry
            lax.fori_loop(0, N_EXPERTS, body, 0)

            def unused(j, carry):
                act(tail_copy(0, pl.multiple_of(j * t, t), t))
                return carry
            lax.fori_loop(nu_ref[0], n_tiles, unused, 0)

        tails(lambda cp: cp.start())
        tails(lambda cp: cp.wait())


def _ffn_kernel(te_ref, nu_ref, xs_ref, wg_ref, wu_ref, wd_ref, y_ref, *, d):
    i = pl.program_id(0)

    @pl.when(i < nu_ref[0])
    def _():
        x = xs_ref[:, :d].astype(BF16)
        g = _dot(x, wg_ref[0])
        u = _dot(x, wu_ref[0])
        a = (g * _sigmoid(g) * u).astype(BF16)
        y = _dot(a, wd_ref[0])
        wrow = xs_ref[:, d:]
        y_ref[...] = y * jnp.concatenate([wrow] * (d // W_COLS), axis=1)

    @pl.when(i >= nu_ref[0])
    def _():
        y_ref[...] = jnp.zeros_like(y_ref)


def _combine_kernel(p_ref, loc_ref, dst_ref, pos_ref, x2_ref, gfin_ref, y_hbm, out_ref, yloc_ref, sems,
                    *, nt, bt, sl):
    i = pl.program_id(0)
    slot = i % 2
    sizes = _run_sizes(bt)

    def runs(tile, buf, act):
        def body(e, carry):
            idx = tile * N_EXPERTS + e
            _for_each_run_piece(
                p_ref[idx], loc_ref[idx], dst_ref[idx], sizes,
                lambda lo, do, size: act(pltpu.make_async_copy(
                    y_hbm.at[pl.ds(do, size)], yloc_ref.at[buf, pl.ds(lo, size)], sems.at[buf])))
            return carry
        lax.fori_loop(0, N_EXPERTS, body, 0)

    @pl.when(i == 0)
    def _():
        yloc_ref[...] = jnp.zeros_like(yloc_ref)
        runs(0, 0, lambda cp: cp.start())

    @pl.when(i + 1 < nt)
    def _():
        runs(i + 1, 1 - slot, lambda cp: cp.start())

    runs(i, slot, lambda cp: cp.wait())
    pos = pos_ref[...]
    pos0 = _col_from_row(pos[0:1], bt)
    pos1 = _col_from_row(pos[1:2], bt)
    scol = lax.broadcasted_iota(I32, (bt, sl), 1).astype(F32)
    perm_t = jnp.where(scol == pos0, 1.0, jnp.where(scol == pos1, 1.0, 0.0)).astype(BF16)
    moe = _dot(perm_t, yloc_ref[slot].astype(BF16))
    out_ref[...] = _rmsnorm(x2_ref[...] + moe, gfin_ref[...])


def _moe_prompt(x2, h2, eid, wsel, wg, wu, wd, g_fin):
    n, d = x2.shape
    bt, t = MOE_TOKENS, MOE_ROWS
    assert n % bt == 0
    nt = n // bt
    sl = -(-(2 * bt + N_EXPERTS * (RUN_ALIGN - 1)) // LANES) * LANES
    onehot = eid.reshape(2, nt, bt)[..., None] == jnp.arange(N_EXPERTS, dtype=I32)
    cnt = jnp.sum(onehot, axis=(0, 2), dtype=I32)
    p = (cnt + RUN_ALIGN - 1) // RUN_ALIGN * RUN_ALIGN
    loc = jnp.cumsum(p, axis=1) - p
    tot = jnp.sum(p, axis=0)
    reg = (tot + t - 1) // t * t
    reg_end = jnp.cumsum(reg)
    reg_start = reg_end - reg
    dst = reg_start[None, :] + jnp.cumsum(p, axis=0) - p
    n_tiles = (2 * n + nt * N_EXPERTS * (RUN_ALIGN - 1) + N_EXPERTS * (t - 1)) // t + 1
    s_max = n_tiles * t
    n_used = (reg_end[-1] // t).astype(I32).reshape(1)
    tile_start = jnp.arange(n_tiles, dtype=I32) * t
    tile_expert = jnp.minimum(jnp.sum(reg_end[None, :] <= tile_start[:, None], axis=1, dtype=I32), N_EXPERTS - 1)
    p_f, loc_f, dst_f = p.reshape(-1), loc.reshape(-1), dst.reshape(-1).astype(I32)
    locf = loc.astype(F32).reshape(nt, N_EXPERTS, 1)
    triu = jnp.asarray(np.triu(np.ones((bt, bt), np.float32), 1), BF16)

    last = nt - 1
    xs, pos = pl.pallas_call(
        functools.partial(_dispatch_kernel, nt=nt, bt=bt, sl=sl, d=d, n_tiles=n_tiles),
        grid_spec=pltpu.PrefetchScalarGridSpec(
            num_scalar_prefetch=6, grid=(nt + 1,),
            in_specs=[pl.BlockSpec((bt, d), lambda i, *_: (jnp.minimum(i, last), 0)),
                      pl.BlockSpec((2, bt), lambda i, *_: (0, jnp.minimum(i, last))),
                      pl.BlockSpec((2, bt), lambda i, *_: (0, jnp.minimum(i, last))),
                      pl.BlockSpec((1, N_EXPERTS, 1), lambda i, *_: (jnp.minimum(i, last), 0, 0)),
                      pl.BlockSpec((bt, bt), lambda i, *_: (0, 0))],
            out_specs=(pl.BlockSpec(memory_space=pl.ANY),
                       pl.BlockSpec((2, bt), lambda i, *_: (0, jnp.minimum(i, last)))),
            scratch_shapes=[pltpu.VMEM((2, sl, d + W_COLS), F32), pltpu.VMEM((t, d + W_COLS), F32),
                            pltpu.SemaphoreType.DMA((2,))]),
        out_shape=(jax.ShapeDtypeStruct((s_max, d + W_COLS), F32), jax.ShapeDtypeStruct((2, n), F32)),
        compiler_params=_cparams(("arbitrary",)), name="moe_dispatch",
    )(p_f, loc_f, dst_f, (reg_start + tot).astype(I32), (reg - tot).astype(I32), n_used, h2, eid, wsel, locf, triu)

    de = wg.shape[2]
    used = lambda i, te, nu: jnp.minimum(i, nu[0] - 1)
    y = pl.pallas_call(
        functools.partial(_ffn_kernel, d=d),
        grid_spec=pltpu.PrefetchScalarGridSpec(
            num_scalar_prefetch=2, grid=(n_tiles,),
            in_specs=[pl.BlockSpec((t, d + W_COLS), lambda i, te, nu: (used(i, te, nu), 0)),
                      pl.BlockSpec((1, d, de), lambda i, te, nu: (te[used(i, te, nu)], 0, 0)),
                      pl.BlockSpec((1, d, de), lambda i, te, nu: (te[used(i, te, nu)], 0, 0)),
                      pl.BlockSpec((1, de, d), lambda i, te, nu: (te[used(i, te, nu)], 0, 0))],
            out_specs=pl.BlockSpec((t, d), lambda i, te, nu: (i, 0))),
        out_shape=jax.ShapeDtypeStruct((s_max, d), F32),
        compiler_params=_cparams(("arbitrary",)), name="moe_ffn",
    )(tile_expert, n_used, xs, wg, wu, wd)

    return pl.pallas_call(
        functools.partial(_combine_kernel, nt=nt, bt=bt, sl=sl),
        grid_spec=pltpu.PrefetchScalarGridSpec(
            num_scalar_prefetch=3, grid=(nt,),
            in_specs=[pl.BlockSpec((2, bt), lambda i, *_: (0, i)),
                      pl.BlockSpec((bt, d), lambda i, *_: (i, 0)),
                      pl.BlockSpec((1, d), lambda i, *_: (0, 0)),
                      pl.BlockSpec(memory_space=pl.ANY)],
            out_specs=pl.BlockSpec((bt, d), lambda i, *_: (i, 0)),
            scratch_shapes=[pltpu.VMEM((2, sl, d), F32), pltpu.SemaphoreType.DMA((2,))]),
        out_shape=jax.ShapeDtypeStruct((n, d), F32),
        compiler_params=_cparams(("arbitrary",)), name="moe_combine",
    )(p_f, loc_f, dst_f, pos, x2, g_fin, y)


def _moe_dense_kernel(x2_ref, h2_ref, eid_ref, wsel_ref, gfin_ref, wg_ref, wu_ref, wd_ref, out_ref, acc_ref):
    e = pl.program_id(0)
    n = x2_ref.shape[0]

    @pl.when(e == 0)
    def _():
        acc_ref[...] = jnp.zeros_like(acc_ref)

    eid = eid_ref[...]
    wsel = wsel_ref[...]
    gate_row = jnp.where(eid[0:1] == e, wsel[0:1], 0.0) + jnp.where(eid[1:2] == e, wsel[1:2], 0.0)
    gate = _col_from_row(gate_row, n)
    x = h2_ref[...]
    g = _dot(x, wg_ref[0])
    u = _dot(x, wu_ref[0])
    a = (g * _sigmoid(g) * u).astype(BF16)
    acc_ref[...] += gate * _dot(a, wd_ref[0])

    @pl.when(e == pl.num_programs(0) - 1)
    def _():
        out_ref[...] = _rmsnorm(x2_ref[...] + acc_ref[...], gfin_ref[...])


def _moe_dense(x2, h2, eid, wsel, wg, wu, wd, g_fin):
    n, d = x2.shape
    de = wg.shape[2]
    full = lambda a: pl.BlockSpec(a.shape, lambda e: (0,) * a.ndim)
    return pl.pallas_call(
        _moe_dense_kernel, grid=(wg.shape[0],),
        in_specs=[full(x2), full(h2), full(eid), full(wsel), full(g_fin),
                  pl.BlockSpec((1, d, de), lambda e: (e, 0, 0)),
                  pl.BlockSpec((1, d, de), lambda e: (e, 0, 0)),
                  pl.BlockSpec((1, de, d), lambda e: (e, 0, 0))],
        out_specs=pl.BlockSpec((n, d), lambda e: (0, 0)),
        out_shape=jax.ShapeDtypeStruct((n, d), F32),
        scratch_shapes=[pltpu.VMEM((n, d), F32)],
        compiler_params=_cparams(("arbitrary",)), name="moe_dense",
    )(x2, h2, eid, wsel, g_fin, wg, wu, wd)


def _layer_weights(norm_mix_g, w_in, fox_f_bias, gla_w_a2, gla_b_a, gla_norm_g, w_fox_out, w_gla_out, w_o,
                   norm_ffn_g, w_group_router, b_group_router, w_expert_router, b_expert_router):
    d = w_in.shape[0]
    c = 0
    parts = {}
    for name, width in (("q", FOX_WIDTH), ("k", FOX_WIDTH), ("v", FOX_WIDTH), ("ff", FOX_HEADS),
                        ("qg", GLA_K), ("kg", GLA_K), ("vg", GLA_V), ("lr", GLA_RANK), ("r", GLA_V),
                        ("gf", d), ("gg", d)):
        parts[name] = w_in[:, c:c + width]
        c += width
    assert c == w_in.shape[1]
    g_mix = norm_mix_g.reshape(1, d)
    w_fft = jnp.zeros((2 * SUBLANES, d), F32).at[:FOX_HEADS].set(parts["ff"].T).astype(BF16)
    pw = dict(
        g_mix=g_mix,
        w_qkv=jnp.concatenate([parts["q"], parts["k"], parts["v"]], axis=1).astype(BF16),
        w_fft=w_fft,
        w_gla=jnp.concatenate([parts["qg"], parts["kg"], parts["vg"]], axis=1).astype(BF16),
        w_lr=parts["lr"].astype(BF16),
        w_a2=gla_w_a2.astype(BF16),
        b_a=gla_b_a.reshape(1, GLA_K),
        f_bias=fox_f_bias.reshape(FOX_HEADS, 1),
        w_vt=parts["v"].T.astype(BF16),
        w_ff=parts["ff"].astype(BF16),
        f_bias_row=fox_f_bias.reshape(1, FOX_HEADS),
    )
    w_rt = jnp.zeros((SUBLANES + N_EXPERTS, d), F32)
    w_rt = w_rt.at[:N_GROUPS].set(w_group_router.T).at[SUBLANES:].set(w_expert_router.T)
    b_rt = jnp.zeros((SUBLANES + N_EXPERTS, 1), F32)
    b_rt = b_rt.at[:N_GROUPS, 0].set(b_group_router).at[SUBLANES:, 0].set(b_expert_router)
    mw = dict(
        g_mix=g_mix,
        w_rg=jnp.concatenate([parts["r"], parts["gf"], parts["gg"]], axis=1).astype(BF16),
        w_fo=w_fox_out.astype(BF16), w_go=w_gla_out.astype(BF16), w_o=w_o.astype(BF16),
        g_gla=gla_norm_g.reshape(1, GLA_V), g_ffn=norm_ffn_g.reshape(1, d),
        w_rt=jnp.stack(_split3(w_rt)), b_rt=b_rt,
    )
    return pw, mw


def kernel(x_prompt, x_sample, cache_k, cache_v, cache_log_f, state_gla, page_table, meta_tokens, norm_mix_g,
           w_in, fox_f_bias, gla_w_a2, gla_b_a, gla_norm_g, w_fox_out, w_gla_out, w_o, norm_ffn_g,
           w_group_router, b_group_router, w_expert_router, b_expert_router, w_expert_gate, w_expert_up,
           w_expert_down, norm_final_g):
    depth = w_in.shape[0]
    assert depth == 1, "meta rows are only carried through one layer"
    b, seq, d = x_prompt.shape
    db, t_dec, _ = x_sample.shape
    n_phys = cache_k.shape[1]
    pw, mw = _layer_weights(norm_mix_g[0], w_in[0], fox_f_bias[0], gla_w_a2[0], gla_b_a[0], gla_norm_g[0],
                            w_fox_out[0], w_gla_out[0], w_o[0], norm_ffn_g[0], w_group_router[0],
                            b_group_router[0], w_expert_router[0], b_expert_router[0])
    wg = w_expert_gate[0].astype(BF16)
    wu = w_expert_up[0].astype(BF16)
    wd = w_expert_down[0].astype(BF16)
    g_fin = norm_final_g.reshape(1, d)

    (_, k_m, v_m, kb_m, _, _, qg_m, kg_m, vg_m, la_m, vt_m, lf_m) = _project(
        meta_tokens.reshape(1, N_META, d), N_META, pw)
    zero_state = jnp.zeros((1, GLA_HEADS // 2, LANES, GLA_VAL_DIM), F32)
    _, s_meta = _gla(qg_m, kg_m, vg_m, la_m, zero_state, N_META)

    tile = min(ATTN_TILE, seq)
    q, k, v, kb, _, _, qg, kg, vg, la, vt, lf = _project(x_prompt, tile, pw)
    o_fox = _fox_prompt(q, kb, vt, lf, kb_m[0], vt_m[0, 0], lf_m[0], tile)
    o_gla, s_gla_p = _gla(qg, kg, vg, la, s_meta, min(GLA_CHUNK, seq))
    n = b * seq
    x2, h2, eid, wsel = _merge(x_prompt.reshape(n, d), o_fox.reshape(n, FOX_WIDTH), o_gla.reshape(n, GLA_V),
                               mw, min(MERGE_TILE, n))
    y_prompt = _moe_prompt(x2, h2, eid, wsel, wg, wu, wd, g_fin).reshape(b, seq, d)

    ns = db * t_dec
    qs, ks, vs, kbs, vbs, lfts, qgs, kgs, vgs, las, _, lfs = _project(x_sample.reshape(1, ns, d), ns, pw)
    tok = lambda a: a.reshape(db, t_dec, a.shape[-1])
    lft_s = lfts.reshape(FOX_HEADS, db, t_dec).transpose(1, 0, 2)
    o_fox_s = _fox_sample(tok(qs), tok(kbs), tok(vbs), lft_s, cache_k[0].transpose(0, 2, 3, 1),
                          cache_v[0].transpose(0, 2, 3, 1), cache_log_f[0].transpose(0, 2, 1), page_table)
    s0 = state_gla[0].reshape(db, GLA_HEADS // 2, LANES, GLA_VAL_DIM)
    o_gla_s, s_gla_s = _gla(tok(qgs), tok(kgs), tok(vgs), tok(las), s0, t_dec)
    x2s, h2s, eids, wsels = _merge(x_sample.reshape(ns, d), o_fox_s.reshape(ns, FOX_WIDTH),
                                   o_gla_s.reshape(ns, GLA_V), mw, ns)
    y_sample = _moe_dense(x2s, h2s, eids, wsels, wg, wu, wd, g_fin).reshape(db, t_dec, d)

    heads = lambda a: a.reshape(a.shape[:-1] + (FOX_HEADS, FOX_HEAD_DIM))
    with_meta = lambda m, r: jnp.concatenate([jnp.broadcast_to(m, (b,) + m.shape[1:]), r], axis=1)
    new_k_prompt = heads(with_meta(k_m, k))[None]
    new_v_prompt = heads(with_meta(v_m, v))[None]
    new_log_f_prompt = with_meta(lf_m, lf)[None]
    new_gla_prompt = s_gla_p.reshape(1, b, GLA_HEADS, GLA_KEY_DIM, GLA_VAL_DIM)
    new_k_sample = heads(tok(ks))[None]
    new_v_sample = heads(tok(vs))[None]
    new_log_f_sample = lfs.reshape(1, db, t_dec, FOX_HEADS)
    new_gla_sample = s_gla_s.reshape(1, db, GLA_HEADS, GLA_KEY_DIM, GLA_VAL_DIM)
    return (y_prompt, y_sample, new_k_prompt, new_v_prompt, new_log_f_prompt, new_gla_prompt,
            new_k_sample, new_v_sample, new_log_f_sample, new_gla_sample)
```

```python
import functools

import numpy as np
import jax
import jax.numpy as jnp
from jax import lax
from jax.experimental import pallas as pl
from jax.experimental.pallas import tpu as pltpu

F32 = jnp.float32
BF16 = jnp.bfloat16
I32 = jnp.int32
EPS = 1e-6
LOG2E = 1.4426950408889634
N_META = 16
FOX_HEADS = 8
FOX_HEAD_DIM = 64
FOX_WIDTH = FOX_HEADS * FOX_HEAD_DIM
GLA_HEADS = 4
GLA_KEY_DIM = 64
GLA_VAL_DIM = 128
GLA_K = GLA_HEADS * GLA_KEY_DIM
GLA_V = GLA_HEADS * GLA_VAL_DIM
GLA_RANK = 16
GLA_TAU = 16.0
GLA_CHUNK = 128
N_GROUPS = 4
EXPERTS_PER_GROUP = 8
N_EXPERTS = N_GROUPS * EXPERTS_PER_GROUP
PAGE_SIZE = 128

LANES = 128
SUBLANES = 8
VMEM_LIMIT = 56 * 1024 * 1024

PROJ_TILE = 512
ATTN_TILE = 512
MERGE_TILE = 512
MOE_TOKENS = 512
MOE_ROWS = 512
RUN_ALIGN = SUBLANES
RUN_LARGE = 128
PAGES_PER_STEP = 32
GLA_SEQS_PER_STEP = 4
W_COLS = LANES


def _cparams(sem):
    return pltpu.CompilerParams(dimension_semantics=sem, vmem_limit_bytes=VMEM_LIMIT)


def _dot(a, b):
    return jnp.dot(a, b, preferred_element_type=F32)


def _dot_nt(a, b):
    return lax.dot_general(a, b, (((1,), (1,)), ((), ())), preferred_element_type=F32)


def _dot_tn(a, b):
    return lax.dot_general(a, b, (((0,), (0,)), ((), ())), preferred_element_type=F32)


def _split3(a):
    hi = a.astype(BF16)
    r1 = a - hi.astype(F32)
    mid = r1.astype(BF16)
    lo = (r1 - mid.astype(F32)).astype(BF16)
    return hi, mid, lo


def _dot_sel_l(m, a):
    hi, mid, lo = _split3(a)
    return _dot(m, lo) + _dot(m, mid) + _dot(m, hi)


def _dot_sel_r(a, m):
    hi, mid, lo = _split3(a)
    return _dot(lo, m) + _dot(mid, m) + _dot(hi, m)


def _rmsnorm(x, g):
    return x * lax.rsqrt(jnp.mean(x * x, axis=-1, keepdims=True) + EPS) * g


def _log_sigmoid(x):
    return jnp.minimum(x, 0.0) - jnp.log1p(jnp.exp(-jnp.abs(x)))


def _sigmoid(x):
    return 1.0 / (1.0 + jnp.exp(-x))


def _col_from_row(row, n):
    r = lax.broadcasted_iota(I32, (n, n), 0)
    c = lax.broadcasted_iota(I32, (n, n), 1)
    return jnp.sum(jnp.where(r == c, jnp.broadcast_to(row, (n, n)), 0.0), axis=1, keepdims=True)


def _proj_kernel(x_ref, g_ref, wqkv_ref, wfft_ref, wgla_ref, wlr_ref, wa2_ref, ba_ref, fb_ref,
                 wvt_ref, wff_ref, fbr_ref,
                 q_ref, k_ref, v_ref, kb_ref, vb_ref, lft_ref, qg_ref, kg_ref, vg_ref, la_ref, vt_ref, lf_ref):
    h = _rmsnorm(x_ref[0], g_ref[...]).astype(BF16)
    z = _dot(h, wqkv_ref[...])
    q_ref[0] = (z[:, :FOX_WIDTH] * (FOX_HEAD_DIM ** -0.5 * LOG2E)).astype(BF16)
    k = z[:, FOX_WIDTH:2 * FOX_WIDTH]
    v = z[:, 2 * FOX_WIDTH:]
    k_ref[0] = k
    v_ref[0] = v
    kb_ref[0] = k.astype(BF16)
    vb_ref[0] = v.astype(BF16)
    vt_ref[0, 0] = _dot_nt(wvt_ref[...], h).astype(BF16)
    fft = _dot_nt(wfft_ref[...], h)[:FOX_HEADS]
    lft_ref[0] = _log_sigmoid(fft + fb_ref[...])
    lf_ref[0] = _log_sigmoid(_dot(h, wff_ref[...]) + fbr_ref[...])
    zg = _dot(h, wgla_ref[...])
    qg_ref[0] = zg[:, :GLA_K] * GLA_KEY_DIM ** -0.5
    kg_ref[0] = zg[:, GLA_K:2 * GLA_K]
    vg_ref[0] = zg[:, 2 * GLA_K:]
    lr = _dot(h, wlr_ref[...])
    xa = _dot(lr.astype(BF16), wa2_ref[...]) + ba_ref[...]
    la_ref[0] = _log_sigmoid(xa) / GLA_TAU


def _project(x3, tm, pw):
    bx, lx, d = x3.shape
    assert lx % tm == 0
    grid = (bx, lx // tm)
    row = lambda w: pl.BlockSpec((1, tm, w), lambda b, i: (b, i, 0))
    full = lambda a: pl.BlockSpec(a.shape, lambda b, i: (0,) * a.ndim)
    ws = (pw["g_mix"], pw["w_qkv"], pw["w_fft"], pw["w_gla"], pw["w_lr"], pw["w_a2"], pw["b_a"], pw["f_bias"],
          pw["w_vt"], pw["w_ff"], pw["f_bias_row"])
    out_shape = (
        jax.ShapeDtypeStruct((bx, lx, FOX_WIDTH), BF16),
        jax.ShapeDtypeStruct((bx, lx, FOX_WIDTH), F32),
        jax.ShapeDtypeStruct((bx, lx, FOX_WIDTH), F32),
        jax.ShapeDtypeStruct((bx, lx, FOX_WIDTH), BF16),
        jax.ShapeDtypeStruct((bx, lx, FOX_WIDTH), BF16),
        jax.ShapeDtypeStruct((bx, FOX_HEADS, lx), F32),
        jax.ShapeDtypeStruct((bx, lx, GLA_K), F32),
        jax.ShapeDtypeStruct((bx, lx, GLA_K), F32),
        jax.ShapeDtypeStruct((bx, lx, GLA_V), F32),
        jax.ShapeDtypeStruct((bx, lx, GLA_K), F32),
        jax.ShapeDtypeStruct((bx, lx // tm, FOX_WIDTH, tm), BF16),
        jax.ShapeDtypeStruct((bx, lx, FOX_HEADS), F32),
    )
    out_specs = (row(FOX_WIDTH), row(FOX_WIDTH), row(FOX_WIDTH), row(FOX_WIDTH), row(FOX_WIDTH),
                 pl.BlockSpec((1, FOX_HEADS, tm), lambda b, i: (b, 0, i)),
                 row(GLA_K), row(GLA_K), row(GLA_V), row(GLA_K),
                 pl.BlockSpec((1, 1, FOX_WIDTH, tm), lambda b, i: (b, i, 0, 0)), row(FOX_HEADS))
    return pl.pallas_call(
        _proj_kernel, grid=grid,
        in_specs=[row(d)] + [full(a) for a in ws],
        out_specs=out_specs, out_shape=out_shape,
        compiler_params=_cparams(("parallel", "parallel")), name="proj",
    )(x3, *ws)


def _online_update_t(s, vt, state):
    m, l, acc = state
    m_new = jnp.maximum(m, jnp.max(s, axis=0, keepdims=True))
    alpha = jnp.exp2(m - m_new)
    p = jnp.exp2(s - m_new)
    l = alpha * l + jnp.sum(p, axis=0, keepdims=True)
    acc = alpha * acc + _dot(vt, p.astype(BF16))
    return m_new, l, acc


def _fox_prompt_kernel(q_ref, kb_ref, vt_ref, lf_ref, km_ref, vtm_ref, lfm_ref, tri_ref, um_ref,
                       o_ref, c_ref, s00_ref, s01_ref, s10_ref, s11_ref, *, tq):
    qi = pl.program_id(1)
    n_blk = kb_ref.shape[1] // tq
    dh = FOX_HEAD_DIM
    s_refs = ((s00_ref, s01_ref), (s10_ref, s11_ref))

    @pl.when(qi == 0)
    def _():
        carry = jnp.zeros((1, FOX_HEADS), F32)
        for jb in range(n_blk):
            for js in range(tq // LANES):
                off = jb * tq + js * LANES
                cs = _dot_sel_l(tri_ref[...], lf_ref[0, off:off + LANES, :]) + carry
                c_ref[jb, js * LANES:(js + 1) * LANES, :] = cs * LOG2E
                carry = cs[LANES - 1:LANES, :]

    bias_meta = _dot_sel_l(um_ref[...], lfm_ref[...]) * LOG2E
    lane = lax.broadcasted_iota(I32, (tq, LANES), 1)
    key = lax.broadcasted_iota(I32, (tq, tq), 0)
    qry = lax.broadcasted_iota(I32, (tq, tq), 1)
    in_a = lane < dh
    for hp in range(FOX_HEADS // 2):
        sl = slice(hp * LANES, (hp + 1) * LANES)
        ha, hb = 2 * hp, 2 * hp + 1
        qp = q_ref[0, :, sl].astype(F32)
        qa = jnp.where(in_a, qp, 0.0).astype(BF16)
        qb = jnp.where(in_a, 0.0, qp).astype(BF16)
        km = km_ref[:, sl]

        def first(qh, h, km=km):
            s = _dot_nt(km, qh) + bias_meta[:, h:h + 1]
            m = jnp.max(s, axis=0, keepdims=True)
            p = jnp.exp2(s - m)
            return m, jnp.sum(p, axis=0, keepdims=True), _dot(vtm_ref[h * dh:(h + 1) * dh, :], p.astype(BF16))

        def score(j, buf, qa=qa, qb=qb, ha=ha, hb=hb, sl=sl):
            start = pl.multiple_of(j * tq, tq)
            kc = kb_ref[0, pl.ds(start, tq), sl]
            cj = c_ref[j]
            s_refs[buf][0][...] = _dot_nt(kc, qa) - cj[:, ha:ha + 1]
            s_refs[buf][1][...] = _dot_nt(kc, qb) - cj[:, hb:hb + 1]

        def advance(j, buf, states, masked, ha=ha, hb=hb):
            out = []
            for sub, h in ((0, ha), (1, hb)):
                s = s_refs[buf][sub][...]
                if masked:
                    s = jnp.where(key <= qry, s, -jnp.inf)
                out.append(_online_update_t(s, vt_ref[0, j, h * dh:(h + 1) * dh, :], states[sub]))
            return tuple(out)

        def two_blocks(i, states):
            j = 2 * i
            score(j + 1, 1)
            states = advance(j, 0, states, False)
            score(j + 2, 0)
            return advance(j + 1, 1, states, False)

        def last_from_even(states):
            return advance(qi, 0, states, True)

        def last_from_odd(states):
            score(qi, 1)
            return advance(qi, 1, advance(qi - 1, 0, states, False), True)

        score(0, 0)
        states = lax.fori_loop(0, qi // 2, two_blocks, (first(qa, ha), first(qb, hb)))
        (_, la, acc_a), (_, lb, acc_b) = lax.cond(qi % 2 == 0, last_from_even, last_from_odd, states)
        o_t = jnp.concatenate([acc_a / la, acc_b / lb], axis=0)
        o_ref[0, :, sl] = o_t.T.astype(BF16)


def _fox_prompt(q, kb, vt, lf, kb_meta, vt_meta, lf_meta, tq):
    b, l, w = q.shape
    assert l % tq == 0 and vt.shape == (b, l // tq, w, tq)
    n_blk = l // tq
    tri = jnp.asarray(np.tril(np.ones((LANES, LANES), np.float32)), BF16)
    um = jnp.asarray(np.triu(np.ones((N_META, N_META), np.float32), 1), BF16)
    full = lambda a: pl.BlockSpec(a.shape, lambda bi, qi: (0,) * a.ndim)
    return pl.pallas_call(
        functools.partial(_fox_prompt_kernel, tq=tq), grid=(b, n_blk),
        in_specs=[pl.BlockSpec((1, tq, w), lambda bi, qi: (bi, qi, 0)),
                  pl.BlockSpec((1, l, w), lambda bi, qi: (bi, 0, 0)),
                  pl.BlockSpec((1, n_blk, w, tq), lambda bi, qi: (bi, 0, 0, 0)),
                  pl.BlockSpec((1, l, FOX_HEADS), lambda bi, qi: (bi, 0, 0)),
                  full(kb_meta), full(vt_meta), full(lf_meta), full(tri), full(um)],
        out_specs=pl.BlockSpec((1, tq, w), lambda bi, qi: (bi, qi, 0)),
        out_shape=jax.ShapeDtypeStruct((b, l, w), BF16),
        scratch_shapes=[pltpu.VMEM((n_blk, tq, FOX_HEADS), F32)] + [pltpu.VMEM((tq, tq), F32)] * 4,
        compiler_params=_cparams(("parallel", "arbitrary")), name="fox_prompt",
    )(q, kb, vt, lf, kb_meta, vt_meta, lf_meta, tri, um)


def _fox_sample_kernel(pt_ref, q_ref, kn_ref, vn_ref, lfn_ref, t8_ref, u_ref, *rest, g, t):
    ck = rest[:g]
    cv = rest[g:2 * g]
    clf = rest[2 * g:3 * g]
    o_ref, qbd_ref, m_ref, l_ref, acc_ref, r_ref = rest[3 * g:]
    gi = pl.program_id(1)
    rows = t * FOX_HEADS
    own = (lax.broadcasted_iota(I32, (rows, FOX_WIDTH), 0) % FOX_HEADS
           == lax.broadcasted_iota(I32, (rows, FOX_WIDTH), 1) // FOX_HEAD_DIM)

    @pl.when(gi == 0)
    def _():
        q = q_ref[0].astype(F32)
        qt = jnp.concatenate([jnp.broadcast_to(q[i:i + 1], (FOX_HEADS, FOX_WIDTH)) for i in range(t)], axis=0)
        qbd = jnp.where(own, qt, 0.0).astype(BF16)
        qbd_ref[...] = qbd
        a_t = _dot_sel_r(lfn_ref[0], t8_ref[...])
        s = _dot_nt(qbd, kn_ref[0]) - jnp.concatenate([a_t * LOG2E] * t, axis=0)
        rt = lax.broadcasted_iota(I32, (rows, t), 0) // FOX_HEADS
        ct = lax.broadcasted_iota(I32, (rows, t), 1)
        s = jnp.where(ct <= rt, s, -jnp.inf)
        m = jnp.max(s, axis=1, keepdims=True)
        p = jnp.exp2(s - m)
        m_ref[...] = m
        l_ref[...] = jnp.sum(p, axis=1, keepdims=True)
        acc_ref[...] = _dot(p.astype(BF16), vn_ref[0])
        r_ref[...] = jnp.zeros_like(r_ref)

    lf_all = jnp.concatenate([clf[jj][0] for jj in range(g)], axis=0)
    after = _dot_sel_r(lf_all, u_ref[...])
    r_run = r_ref[:, 0:1]
    qbd = qbd_ref[...]
    scores = [None] * g
    for jj in reversed(range(g)):
        bias = (after[jj * FOX_HEADS:(jj + 1) * FOX_HEADS] + r_run) * LOG2E
        r_run = r_run + jnp.sum(clf[jj][0], axis=1, keepdims=True)
        kt = ck[jj][0].reshape(FOX_WIDTH, PAGE_SIZE).astype(BF16)
        scores[jj] = _dot(qbd, kt) + jnp.concatenate([bias] * t, axis=0)
    r_ref[...] = jnp.broadcast_to(r_run, r_ref.shape)
    m = m_ref[...]
    m_new = m
    for s in scores:
        m_new = jnp.maximum(m_new, jnp.max(s, axis=1, keepdims=True))
    alpha = jnp.exp2(m - m_new)
    l = alpha * l_ref[...]
    acc = alpha * acc_ref[...]
    for jj, s in enumerate(scores):
        p = jnp.exp2(s - m_new)
        l = l + jnp.sum(p, axis=1, keepdims=True)
        acc = acc + _dot_nt(p.astype(BF16), cv[jj][0].reshape(FOX_WIDTH, PAGE_SIZE).astype(BF16))
    m_ref[...] = m_new
    l_ref[...] = l
    acc_ref[...] = acc

    @pl.when(gi == pl.num_programs(1) - 1)
    def _():
        o = jnp.where(own, acc / l, 0.0)
        o_ref[0] = jnp.concatenate(
            [jnp.sum(o[i * FOX_HEADS:(i + 1) * FOX_HEADS], axis=0, keepdims=True) for i in range(t)],
            axis=0).astype(BF16)


def _fox_sample(q, kb_new, vb_new, lft_new, cache_kt, cache_vt, cache_lft, page_table):
    db, t, w = q.shape
    n_pages = page_table.shape[1]
    g = min(PAGES_PER_STEP, n_pages)
    assert n_pages % g == 0
    ng = n_pages // g
    rows = FOX_HEADS * t
    t8 = jnp.asarray(np.triu(np.ones((t, t), np.float32)), BF16)
    u = jnp.asarray(np.tril(np.ones((PAGE_SIZE, PAGE_SIZE), np.float32), -1), BF16)
    tok = lambda wd: pl.BlockSpec((1, t, wd), lambda b, gi, pt: (b, 0, 0))
    full = lambda a: pl.BlockSpec(a.shape, lambda b, gi, pt: (0,) * a.ndim)

    def page(shape, jj):
        zeros = (0,) * (len(shape) - 1)
        return pl.BlockSpec(shape, lambda b, gi, pt: (pt[b * n_pages + (ng - 1 - gi) * g + jj],) + zeros)

    kv_page = (1, FOX_HEADS, FOX_HEAD_DIM, PAGE_SIZE)
    in_specs = ([tok(w), tok(w), tok(w), pl.BlockSpec((1, FOX_HEADS, t), lambda b, gi, pt: (b, 0, 0)),
                 full(t8), full(u)]
                + [page(kv_page, jj) for jj in range(g)]
                + [page(kv_page, jj) for jj in range(g)]
                + [page((1, FOX_HEADS, PAGE_SIZE), jj) for jj in range(g)])
    grid_spec = pltpu.PrefetchScalarGridSpec(
        num_scalar_prefetch=1, grid=(db, ng), in_specs=in_specs,
        out_specs=pl.BlockSpec((1, t, w), lambda b, gi, pt: (b, 0, 0)),
        scratch_shapes=[pltpu.VMEM((rows, w), BF16), pltpu.VMEM((rows, 1), F32), pltpu.VMEM((rows, 1), F32),
                        pltpu.VMEM((rows, w), F32), pltpu.VMEM((FOX_HEADS, LANES), F32)])
    return pl.pallas_call(
        functools.partial(_fox_sample_kernel, g=g, t=t), grid_spec=grid_spec,
        out_shape=jax.ShapeDtypeStruct((db, t, w), BF16),
        compiler_params=_cparams(("parallel", "arbitrary")), name="fox_sample",
    )(page_table.reshape(-1), q, kb_new, vb_new, lft_new, t8, u,
      *([cache_kt] * g), *([cache_vt] * g), *([cache_lft] * g))


def _gla_tables(c):
    idx = np.arange(c)
    ws = [np.tril(np.ones((c, c), np.float32))]
    ms = [np.eye(c, dtype=np.float32)]
    sz = 2
    while sz <= c:
        seg = idx // sz
        upper = (idx % sz) >= sz // 2
        mid = seg * sz + sz // 2 - 1
        j = idx[None, :]
        r = idx[:, None]
        w_up = (j > mid[:, None]) & (j <= r)
        w_lo = (j > r) & (j <= mid[:, None])
        ws.append(np.where(upper[:, None], w_up, w_lo).astype(np.float32))
        ms.append(((seg[:, None] == seg[None, :]) & upper[:, None] & ~upper[None, :]).astype(np.float32))
        sz *= 2
    return (jnp.asarray(np.concatenate(ws, axis=0), BF16),
            jnp.asarray(np.stack([np.concatenate([m, m], axis=0) for m in ms]), F32))


def _gla_kernel(q_ref, k_ref, v_ref, la_ref, s0_ref, wall_ref, msk_ref, o_ref, sfin_ref, state_ref,
                *, c, n_levels, nb, shared_s0):
    ci = pl.program_id(1)

    @pl.when(ci == 0)
    def _():
        for bi in range(nb):
            state_ref[bi] = s0_ref[0 if shared_s0 else bi]

    lane = lax.broadcasted_iota(I32, (c, LANES), 1)
    r128 = lax.broadcasted_iota(I32, (LANES, LANES), 0)
    half = GLA_KEY_DIM
    in_a = lane < half

    def stack_heads(x):
        return jnp.concatenate([jnp.where(in_a, x, 0.0), jnp.where(in_a, 0.0, x)], axis=0).astype(BF16)

    for bi in range(nb):
        la_hi, la_mid, _ = _split3(la_ref[bi])
        seg = _dot(wall_ref[...], jnp.concatenate([la_hi, la_mid], axis=1))
        seg = seg[:, GLA_K:] + seg[:, :GLA_K]
        b_all = seg[:c]
        e_all = jnp.exp(seg[c:])
        for p in range(GLA_HEADS // 2):
            ksl = slice(p * LANES, (p + 1) * LANES)
            q = q_ref[bi, :, ksl]
            k = k_ref[bi, :, ksl]
            v16 = v_ref[bi, :, 2 * p * LANES:(2 * p + 2) * LANES].astype(BF16)
            sab = msk_ref[0] * _dot_nt(stack_heads(q), k.astype(BF16))
            for lv in range(n_levels):
                e = e_all[lv * c:(lv + 1) * c, ksl]
                sab = sab + msk_ref[lv + 1] * _dot_nt(stack_heads(q * e), (k * e).astype(BF16))
            st = state_ref[bi, p]
            b = b_all[:, ksl]
            inter = _dot(stack_heads(q * jnp.exp(b)), st.astype(BF16))
            s16 = sab.astype(BF16)
            o_ref[bi, :, 2 * p * LANES:(2 * p + 1) * LANES] = inter[:c] + _dot(s16[:c], v16[:, :LANES])
            o_ref[bi, :, (2 * p + 1) * LANES:(2 * p + 2) * LANES] = inter[c:] + _dot(s16[c:], v16[:, LANES:])
            b_end = b[c - 1:c, :]
            kdec = (k * jnp.exp(b_end - b)).astype(BF16)
            upd = _dot_tn(kdec, v16)
            upd = jnp.where(r128 < half, upd[:, :LANES], upd[:, LANES:])
            state_ref[bi, p] = _col_from_row(jnp.exp(b_end), LANES) * st + upd

    @pl.when(ci == pl.num_programs(1) - 1)
    def _():
        sfin_ref[...] = state_ref[...]


def _gla(qg, kg, vg, la, s0, c):
    b, l, _ = qg.shape
    assert l % c == 0
    nb = GLA_SEQS_PER_STEP
    while b % nb:
        nb //= 2
    wall, msk = _gla_tables(c)
    n_levels = msk.shape[0] - 1
    shared_s0 = s0.shape[0] != b
    st_shape = (GLA_HEADS // 2, LANES, GLA_VAL_DIM)
    s0_spec = (pl.BlockSpec((1,) + st_shape, lambda bi, ci: (0, 0, 0, 0)) if shared_s0
               else pl.BlockSpec((nb,) + st_shape, lambda bi, ci: (bi, 0, 0, 0)))
    tok = lambda wd: pl.BlockSpec((nb, c, wd), lambda bi, ci: (bi, ci, 0))
    full = lambda a: pl.BlockSpec(a.shape, lambda bi, ci: (0,) * a.ndim)
    return pl.pallas_call(
        functools.partial(_gla_kernel, c=c, n_levels=n_levels, nb=nb, shared_s0=shared_s0),
        grid=(b // nb, l // c),
        in_specs=[tok(GLA_K), tok(GLA_K), tok(GLA_V), tok(GLA_K), s0_spec, full(wall), full(msk)],
        out_specs=(tok(GLA_V), pl.BlockSpec((nb,) + st_shape, lambda bi, ci: (bi, 0, 0, 0))),
        out_shape=(jax.ShapeDtypeStruct((b, l, GLA_V), F32), jax.ShapeDtypeStruct((b,) + st_shape, F32)),
        scratch_shapes=[pltpu.VMEM((nb,) + st_shape, F32)],
        compiler_params=_cparams(("parallel", "arbitrary")), name="gla",
    )(qg, kg, vg, la, s0, wall, msk)


def _merge_kernel(x_ref, of_ref, og_ref, gmix_ref, wrg_ref, wfo_ref, wgo_ref, wo_ref, gng_ref, gffn_ref,
                  wrt_ref, brt_ref, x2_ref, h2_ref, eid_ref, wsel_ref):
    x = x_ref[...]
    tm, d = x.shape
    h = _rmsnorm(x, gmix_ref[...]).astype(BF16)
    z = _dot(h, wrg_ref[...])
    r = z[:, :GLA_V]
    gate_f = z[:, GLA_V:GLA_V + d]
    gate_g = z[:, GLA_V + d:]
    og = og_ref[...]
    parts = []
    for hh in range(GLA_HEADS):
        blk = og[:, hh * GLA_VAL_DIM:(hh + 1) * GLA_VAL_DIM]
        parts.append(blk * lax.rsqrt(jnp.mean(blk * blk, axis=-1, keepdims=True) + EPS))
    ogn = jnp.concatenate(parts, axis=1)
    ogs = (ogn * gng_ref[...] * (r * _sigmoid(r))).astype(BF16)
    branch_f = _dot(of_ref[...], wfo_ref[...])
    branch_g = _dot(ogs, wgo_ref[...])
    merged = _sigmoid(gate_f) * branch_f + _sigmoid(gate_g) * branch_g
    x2 = x + _dot(merged.astype(BF16), wo_ref[...])
    x2_ref[...] = x2
    h2 = _rmsnorm(x2, gffn_ref[...])
    h2_ref[...] = h2.astype(BF16)

    a = _split3(h2)
    wr = (wrt_ref[0], wrt_ref[1], wrt_ref[2])
    lt = (_dot_nt(wr[1], a[1]) + _dot_nt(wr[0], a[2]) + _dot_nt(wr[2], a[0])
          + _dot_nt(wr[0], a[1]) + _dot_nt(wr[1], a[0]) + _dot_nt(wr[0], a[0])) + brt_ref[...]
    row8 = lax.broadcasted_iota(I32, (SUBLANES, tm), 0)
    gl = jnp.where(row8 < N_GROUPS, lt[:SUBLANES], -jnp.inf)
    gmax = jnp.max(gl, axis=0, keepdims=True)
    g_idx = jnp.min(jnp.where(gl == gmax, row8, SUBLANES), axis=0, keepdims=True)
    g_sel = 1.0 / jnp.sum(jnp.exp(gl - gmax), axis=0, keepdims=True)
    rowe = lax.broadcasted_iota(I32, (N_EXPERTS, tm), 0)
    e1 = jnp.where(rowe // EXPERTS_PER_GROUP == g_idx, lt[SUBLANES:], -jnp.inf)
    v1 = jnp.max(e1, axis=0, keepdims=True)
    i1 = jnp.min(jnp.where(e1 == v1, rowe, N_EXPERTS), axis=0, keepdims=True)
    e2 = jnp.where(rowe == i1, -jnp.inf, e1)
    v2 = jnp.max(e2, axis=0, keepdims=True)
    i2 = jnp.min(jnp.where(e2 == v2, rowe, N_EXPERTS), axis=0, keepdims=True)
    tt = jnp.exp(v2 - v1)
    eid_ref[...] = jnp.concatenate([i1, i2], axis=0)
    wsel_ref[...] = jnp.concatenate([g_sel / (1.0 + tt), g_sel * tt / (1.0 + tt)], axis=0)


def _merge(x, o_fox, o_gla, mw, tm):
    n, d = x.shape
    assert n % tm == 0
    row = lambda wd: pl.BlockSpec((tm, wd), lambda i: (i, 0))
    full = lambda a: pl.BlockSpec(a.shape, lambda i: (0,) * a.ndim, pipeline_mode=pl.Buffered(1))
    ws = (mw["g_mix"], mw["w_rg"], mw["w_fo"], mw["w_go"], mw["w_o"], mw["g_gla"], mw["g_ffn"],
          mw["w_rt"], mw["b_rt"])
    return pl.pallas_call(
        _merge_kernel, grid=(n // tm,),
        in_specs=[row(d), row(FOX_WIDTH), row(GLA_V)] + [full(a) for a in ws],
        out_specs=(row(d), row(d), pl.BlockSpec((2, tm), lambda i: (0, i)), pl.BlockSpec((2, tm), lambda i: (0, i))),
        out_shape=(jax.ShapeDtypeStruct((n, d), F32), jax.ShapeDtypeStruct((n, d), BF16),
                   jax.ShapeDtypeStruct((2, n), I32), jax.ShapeDtypeStruct((2, n), F32)),
        compiler_params=_cparams(("parallel",)), name="merge",
    )(x, o_fox, o_gla, *ws)


def _run_sizes(max_rows):
    sizes = []
    s = RUN_ALIGN
    while s <= max_rows:
        sizes.append(s)
        s *= 2
    return tuple(reversed(sizes))


def _for_each_run_piece(n_rows, a_off, b_off, sizes, fn):
    def pieces(n, a, b, szs):
        for size in szs:
            bit = (n & size) != 0

            @pl.when(bit)
            def _(a=a, b=b, size=size):
                fn(pl.multiple_of(a, RUN_ALIGN), pl.multiple_of(b, RUN_ALIGN), size)

            inc = jnp.where(bit, size, 0)
            a = a + inc
            b = b + inc

    large = [s for s in sizes if s >= RUN_LARGE]
    small = [s for s in sizes if s < RUN_LARGE]
    n_large = n_rows - (n_rows & (RUN_LARGE - 1))
    if large:
        @pl.when(n_large != 0)
        def _():
            pieces(n_large, a_off, b_off, large)
    pieces(n_rows, a_off + n_large, b_off + n_large, small)


def _dispatch_kernel(p_ref, loc_ref, dst_ref, toff_ref, tlen_ref, nu_ref,
                     h2_ref, eid_ref, wsel_ref, locf_ref, triu_ref,
                     xs_ref, pos_ref, xloc_ref, zero_ref, sems, *, nt, bt, sl, d, n_tiles):
    i = pl.program_id(0)
    slot = i % 2
    sizes = _run_sizes(bt)
    t = zero_ref.shape[0]

    def runs(tile, buf, act):
        def body(e, carry):
            idx = tile * N_EXPERTS + e
            _for_each_run_piece(
                p_ref[idx], loc_ref[idx], dst_ref[idx], sizes,
                lambda lo, do, size: act(pltpu.make_async_copy(
                    xloc_ref.at[buf, pl.ds(lo, size)], xs_ref.at[pl.ds(do, size)], sems.at[buf])))
            return carry
        lax.fori_loop(0, N_EXPERTS, body, 0)

    @pl.when(i < nt)
    def _():
        eid = eid_ref[...]
        wsel = wsel_ref[...]
        rowe = lax.broadcasted_iota(I32, (N_EXPERTS, bt), 0)
        oh0 = jnp.where(rowe == eid[0:1], 1.0, 0.0)
        oh1 = jnp.where(rowe == eid[1:2], 1.0, 0.0)
        pref0 = _dot(oh0.astype(BF16), triu_ref[...])
        pref1 = _dot(oh1.astype(BF16), triu_ref[...])
        n0 = jnp.sum(oh0, axis=1, keepdims=True)
        loc = locf_ref[0]
        pos0 = jnp.sum(oh0 * (loc + pref0), axis=0, keepdims=True)
        pos1 = jnp.sum(oh1 * (loc + n0 + pref1), axis=0, keepdims=True)
        pos_ref[...] = jnp.concatenate([pos0, pos1], axis=0)
        srow = lax.broadcasted_iota(I32, (sl, bt), 0).astype(F32)
        m0 = srow == pos0
        m1 = srow == pos1
        perm = jnp.where(m0, 1.0, jnp.where(m1, 1.0, 0.0)).astype(BF16)
        wslot = jnp.sum(jnp.where(m0, wsel[0:1], jnp.where(m1, wsel[1:2], 0.0)), axis=1, keepdims=True)
        xloc_ref[slot, :, :d] = _dot(perm, h2_ref[...])
        xloc_ref[slot, :, d:] = jnp.broadcast_to(wslot, (sl, W_COLS))
        runs(i, slot, lambda cp: cp.start())

    @pl.when(i > 0)
    def _():
        runs(i - 1, 1 - slot, lambda cp: cp.wait())

    @pl.when(i == nt)
    def _():
        zero_ref[...] = jnp.zeros_like(zero_ref)
        tsizes = _run_sizes(t // 2)

        def tail_copy(zo, do, size):
            return pltpu.make_async_copy(zero_ref.at[pl.ds(0, size)], xs_ref.at[pl.ds(do, size)], sems.at[slot])

        def tails(act):
            def body(e, carry):
                _for_each_run_piece(tlen_ref[e], jnp.int32(0), toff_ref[e], tsizes,
                                    lambda zo, do, size: act(tail_copy(zo, do, size)))
                return carry
            lax.fori_loop(0, N_EXPERTS, body, 0)

            def unused(j, carry):
                act(tail_copy(0, pl.multiple_of(j * t, t), t))
                return carry
            lax.fori_loop(nu_ref[0], n_tiles, unused, 0)

        tails(lambda cp: cp.start())
        tails(lambda cp: cp.wait())


def _ffn_kernel(te_ref, nu_ref, xs_ref, wg_ref, wu_ref, wd_ref, y_ref, *, d):
    i = pl.program_id(0)

    @pl.when(i < nu_ref[0])
    def _():
        x = xs_ref[:, :d].astype(BF16)
        g = _dot(x, wg_ref[0])
        u = _dot(x, wu_ref[0])
        a = (g * _sigmoid(g) * u).astype(BF16)
        y = _dot(a, wd_ref[0])
        wrow = xs_ref[:, d:]
        y_ref[...] = y * jnp.concatenate([wrow] * (d // W_COLS), axis=1)

    @pl.when(i >= nu_ref[0])
    def _():
        y_ref[...] = jnp.zeros_like(y_ref)


def _combine_kernel(p_ref, loc_ref, dst_ref, pos_ref, x2_ref, gfin_ref, y_hbm, out_ref, yloc_ref, sems,
                    *, nt, bt, sl):
    i = pl.program_id(0)
    slot = i % 2
    sizes = _run_sizes(bt)

    def runs(tile, buf, act):
        def body(e, carry):
            idx = tile * N_EXPERTS + e
            _for_each_run_piece(
                p_ref[idx], loc_ref[idx], dst_ref[idx], sizes,
                lambda lo, do, size: act(pltpu.make_async_copy(
                    y_hbm.at[pl.ds(do, size)], yloc_ref.at[buf, pl.ds(lo, size)], sems.at[buf])))
            return carry
        lax.fori_loop(0, N_EXPERTS, body, 0)

    @pl.when(i == 0)
    def _():
        yloc_ref[...] = jnp.zeros_like(yloc_ref)
        runs(0, 0, lambda cp: cp.start())

    @pl.when(i + 1 < nt)
    def _():
        runs(i + 1, 1 - slot, lambda cp: cp.start())

    runs(i, slot, lambda cp: cp.wait())
    pos = pos_ref[...]
    pos0 = _col_from_row(pos[0:1], bt)
    pos1 = _col_from_row(pos[1:2], bt)
    scol = lax.broadcasted_iota(I32, (bt, sl), 1).astype(F32)
    perm_t = jnp.where(scol == pos0, 1.0, jnp.where(scol == pos1, 1.0, 0.0)).astype(BF16)
    moe = _dot(perm_t, yloc_ref[slot].astype(BF16))
    out_ref[...] = _rmsnorm(x2_ref[...] + moe, gfin_ref[...])


def _moe_prompt(x2, h2, eid, wsel, wg, wu, wd, g_fin):
    n, d = x2.shape
    bt, t = MOE_TOKENS, MOE_ROWS
    assert n % bt == 0
    nt = n // bt
    sl = -(-(2 * bt + N_EXPERTS * (RUN_ALIGN - 1)) // LANES) * LANES
    onehot = eid.reshape(2, nt, bt)[..., None] == jnp.arange(N_EXPERTS, dtype=I32)
    cnt = jnp.sum(onehot, axis=(0, 2), dtype=I32)
    p = (cnt + RUN_ALIGN - 1) // RUN_ALIGN * RUN_ALIGN
    loc = jnp.cumsum(p, axis=1) - p
    tot = jnp.sum(p, axis=0)
    reg = (tot + t - 1) // t * t
    reg_end = jnp.cumsum(reg)
    reg_start = reg_end - reg
    dst = reg_start[None, :] + jnp.cumsum(p, axis=0) - p
    n_tiles = (2 * n + nt * N_EXPERTS * (RUN_ALIGN - 1) + N_EXPERTS * (t - 1)) // t + 1
    s_max = n_tiles * t
    n_used = (reg_end[-1] // t).astype(I32).reshape(1)
    tile_start = jnp.arange(n_tiles, dtype=I32) * t
    tile_expert = jnp.minimum(jnp.sum(reg_end[None, :] <= tile_start[:, None], axis=1, dtype=I32), N_EXPERTS - 1)
    p_f, loc_f, dst_f = p.reshape(-1), loc.reshape(-1), dst.reshape(-1).astype(I32)
    locf = loc.astype(F32).reshape(nt, N_EXPERTS, 1)
    triu = jnp.asarray(np.triu(np.ones((bt, bt), np.float32), 1), BF16)

    last = nt - 1
    xs, pos = pl.pallas_call(
        functools.partial(_dispatch_kernel, nt=nt, bt=bt, sl=sl, d=d, n_tiles=n_tiles),
        grid_spec=pltpu.PrefetchScalarGridSpec(
            num_scalar_prefetch=6, grid=(nt + 1,),
            in_specs=[pl.BlockSpec((bt, d), lambda i, *_: (jnp.minimum(i, last), 0)),
                      pl.BlockSpec((2, bt), lambda i, *_: (0, jnp.minimum(i, last))),
                      pl.BlockSpec((2, bt), lambda i, *_: (0, jnp.minimum(i, last))),
                      pl.BlockSpec((1, N_EXPERTS, 1), lambda i, *_: (jnp.minimum(i, last), 0, 0)),
                      pl.BlockSpec((bt, bt), lambda i, *_: (0, 0))],
            out_specs=(pl.BlockSpec(memory_space=pl.ANY),
                       pl.BlockSpec((2, bt), lambda i, *_: (0, jnp.minimum(i, last)))),
            scratch_shapes=[pltpu.VMEM((2, sl, d + W_COLS), F32), pltpu.VMEM((t, d + W_COLS), F32),
                            pltpu.SemaphoreType.DMA((2,))]),
        out_shape=(jax.ShapeDtypeStruct((s_max, d + W_COLS), F32), jax.ShapeDtypeStruct((2, n), F32)),
        compiler_params=_cparams(("arbitrary",)), name="moe_dispatch",
    )(p_f, loc_f, dst_f, (reg_start + tot).astype(I32), (reg - tot).astype(I32), n_used, h2, eid, wsel, locf, triu)

    de = wg.shape[2]
    used = lambda i, te, nu: jnp.minimum(i, nu[0] - 1)
    y = pl.pallas_call(
        functools.partial(_ffn_kernel, d=d),
        grid_spec=pltpu.PrefetchScalarGridSpec(
            num_scalar_prefetch=2, grid=(n_tiles,),
            in_specs=[pl.BlockSpec((t, d + W_COLS), lambda i, te, nu: (used(i, te, nu), 0)),
                      pl.BlockSpec((1, d, de), lambda i, te, nu: (te[used(i, te, nu)], 0, 0)),
                      pl.BlockSpec((1, d, de), lambda i, te, nu: (te[used(i, te, nu)], 0, 0)),
                      pl.BlockSpec((1, de, d), lambda i, te, nu: (te[used(i, te, nu)], 0, 0))],
            out_specs=pl.BlockSpec((t, d), lambda i, te, nu: (i, 0))),
        out_shape=jax.ShapeDtypeStruct((s_max, d), F32),
        compiler_params=_cparams(("arbitrary",)), name="moe_ffn",
    )(tile_expert, n_used, xs, wg, wu, wd)

    return pl.pallas_call(
        functools.partial(_combine_kernel, nt=nt, bt=bt, sl=sl),
        grid_spec=pltpu.PrefetchScalarGridSpec(
            num_scalar_prefetch=3, grid=(nt,),
            in_specs=[pl.BlockSpec((2, bt), lambda i, *_: (0, i)),
                      pl.BlockSpec((bt, d), lambda i, *_: (i, 0)),
                      pl.BlockSpec((1, d), lambda i, *_: (0, 0)),
                      pl.BlockSpec(memory_space=pl.ANY)],
            out_specs=pl.BlockSpec((bt, d), lambda i, *_: (i, 0)),
            scratch_shapes=[pltpu.VMEM((2, sl, d), F32), pltpu.SemaphoreType.DMA((2,))]),
        out_shape=jax.ShapeDtypeStruct((n, d), F32),
        compiler_params=_cparams(("arbitrary",)), name="moe_combine",
    )(p_f, loc_f, dst_f, pos, x2, g_fin, y)


def _moe_dense_kernel(x2_ref, h2_ref, eid_ref, wsel_ref, gfin_ref, wg_ref, wu_ref, wd_ref, out_ref, acc_ref):
    e = pl.program_id(0)
    n = x2_ref.shape[0]

    @pl.when(e == 0)
    def _():
        acc_ref[...] = jnp.zeros_like(acc_ref)

    eid = eid_ref[...]
    wsel = wsel_ref[...]
    gate_row = jnp.where(eid[0:1] == e, wsel[0:1], 0.0) + jnp.where(eid[1:2] == e, wsel[1:2], 0.0)
    gate = _col_from_row(gate_row, n)
    x = h2_ref[...]
    g = _dot(x, wg_ref[0])
    u = _dot(x, wu_ref[0])
    a = (g * _sigmoid(g) * u).astype(BF16)
    acc_ref[...] += gate * _dot(a, wd_ref[0])

    @pl.when(e == pl.num_programs(0) - 1)
    def _():
        out_ref[...] = _rmsnorm(x2_ref[...] + acc_ref[...], gfin_ref[...])


def _moe_dense(x2, h2, eid, wsel, wg, wu, wd, g_fin):
    n, d = x2.shape
    de = wg.shape[2]
    full = lambda a: pl.BlockSpec(a.shape, lambda e: (0,) * a.ndim)
    return pl.pallas_call(
        _moe_dense_kernel, grid=(wg.shape[0],),
        in_specs=[full(x2), full(h2), full(eid), full(wsel), full(g_fin),
                  pl.BlockSpec((1, d, de), lambda e: (e, 0, 0)),
                  pl.BlockSpec((1, d, de), lambda e: (e, 0, 0)),
                  pl.BlockSpec((1, de, d), lambda e: (e, 0, 0))],
        out_specs=pl.BlockSpec((n, d), lambda e: (0, 0)),
        out_shape=jax.ShapeDtypeStruct((n, d), F32),
        scratch_shapes=[pltpu.VMEM((n, d), F32)],
        compiler_params=_cparams(("arbitrary",)), name="moe_dense",
    )(x2, h2, eid, wsel, g_fin, wg, wu, wd)


def _layer_weights(norm_mix_g, w_in, fox_f_bias, gla_w_a2, gla_b_a, gla_norm_g, w_fox_out, w_gla_out, w_o,
                   norm_ffn_g, w_group_router, b_group_router, w_expert_router, b_expert_router):
    d = w_in.shape[0]
    c = 0
    parts = {}
    for name, width in (("q", FOX_WIDTH), ("k", FOX_WIDTH), ("v", FOX_WIDTH), ("ff", FOX_HEADS),
                        ("qg", GLA_K), ("kg", GLA_K), ("vg", GLA_V), ("lr", GLA_RANK), ("r", GLA_V),
                        ("gf", d), ("gg", d)):
        parts[name] = w_in[:, c:c + width]
        c += width
    assert c == w_in.shape[1]
    g_mix = norm_mix_g.reshape(1, d)
    w_fft = jnp.zeros((2 * SUBLANES, d), F32).at[:FOX_HEADS].set(parts["ff"].T).astype(BF16)
    pw = dict(
        g_mix=g_mix,
        w_qkv=jnp.concatenate([parts["q"], parts["k"], parts["v"]], axis=1).astype(BF16),
        w_fft=w_fft,
        w_gla=jnp.concatenate([parts["qg"], parts["kg"], parts["vg"]], axis=1).astype(BF16),
        w_lr=parts["lr"].astype(BF16),
        w_a2=gla_w_a2.astype(BF16),
        b_a=gla_b_a.reshape(1, GLA_K),
        f_bias=fox_f_bias.reshape(FOX_HEADS, 1),
        w_vt=parts["v"].T.astype(BF16),
        w_ff=parts["ff"].astype(BF16),
        f_bias_row=fox_f_bias.reshape(1, FOX_HEADS),
    )
    w_rt = jnp.zeros((SUBLANES + N_EXPERTS, d), F32)
    w_rt = w_rt.at[:N_GROUPS].set(w_group_router.T).at[SUBLANES:].set(w_expert_router.T)
    b_rt = jnp.zeros((SUBLANES + N_EXPERTS, 1), F32)
    b_rt = b_rt.at[:N_GROUPS, 0].set(b_group_router).at[SUBLANES:, 0].set(b_expert_router)
    mw = dict(
        g_mix=g_mix,
        w_rg=jnp.concatenate([parts["r"], parts["gf"], parts["gg"]], axis=1).astype(BF16),
        w_fo=w_fox_out.astype(BF16), w_go=w_gla_out.astype(BF16), w_o=w_o.astype(BF16),
        g_gla=gla_norm_g.reshape(1, GLA_V), g_ffn=norm_ffn_g.reshape(1, d),
        w_rt=jnp.stack(_split3(w_rt)), b_rt=b_rt,
    )
    return pw, mw


def kernel(x_prompt, x_sample, cache_k, cache_v, cache_log_f, state_gla, page_table, meta_tokens, norm_mix_g,
           w_in, fox_f_bias, gla_w_a2, gla_b_a, gla_norm_g, w_fox_out, w_gla_out, w_o, norm_ffn_g,
           w_group_router, b_group_router, w_expert_router, b_expert_router, w_expert_gate, w_expert_up,
           w_expert_down, norm_final_g):
    depth = w_in.shape[0]
    assert depth == 1, "meta rows are only carried through one layer"
    b, seq, d = x_prompt.shape
    db, t_dec, _ = x_sample.shape
    n_phys = cache_k.shape[1]
    pw, mw = _layer_weights(norm_mix_g[0], w_in[0], fox_f_bias[0], gla_w_a2[0], gla_b_a[0], gla_norm_g[0],
                            w_fox_out[0], w_gla_out[0], w_o[0], norm_ffn_g[0], w_group_router[0],
                            b_group_router[0], w_expert_router[0], b_expert_router[0])
    wg = w_expert_gate[0].astype(BF16)
    wu = w_expert_up[0].astype(BF16)
    wd = w_expert_down[0].astype(BF16)
    g_fin = norm_final_g.reshape(1, d)

    (_, k_m, v_m, kb_m, _, _, qg_m, kg_m, vg_m, la_m, vt_m, lf_m) = _project(
        meta_tokens.reshape(1, N_META, d), N_META, pw)
    zero_state = jnp.zeros((1, GLA_HEADS // 2, LANES, GLA_VAL_DIM), F32)
    _, s_meta = _gla(qg_m, kg_m, vg_m, la_m, zero_state, N_META)

    tile = min(ATTN_TILE, seq)
    q, k, v, kb, _, _, qg, kg, vg, la, vt, lf = _project(x_prompt, tile, pw)
    o_fox = _fox_prompt(q, kb, vt, lf, kb_m[0], vt_m[0, 0], lf_m[0], tile)
    o_gla, s_gla_p = _gla(qg, kg, vg, la, s_meta, min(GLA_CHUNK, seq))
    n = b * seq
    x2, h2, eid, wsel = _merge(x_prompt.reshape(n, d), o_fox.reshape(n, FOX_WIDTH), o_gla.reshape(n, GLA_V),
                               mw, min(MERGE_TILE, n))
    y_prompt = _moe_prompt(x2, h2, eid, wsel, wg, wu, wd, g_fin).reshape(b, seq, d)

    ns = db * t_dec
    qs, ks, vs, kbs, vbs, lfts, qgs, kgs, vgs, las, _, lfs = _project(x_sample.reshape(1, ns, d), ns, pw)
    tok = lambda a: a.reshape(db, t_dec, a.shape[-1])
    lft_s = lfts.reshape(FOX_HEADS, db, t_dec).transpose(1, 0, 2)
    o_fox_s = _fox_sample(tok(qs), tok(kbs), tok(vbs), lft_s, cache_k[0].transpose(0, 2, 3, 1),
                          cache_v[0].transpose(0, 2, 3, 1), cache_log_f[0].transpose(0, 2, 1), page_table)
    s0 = state_gla[0].reshape(db, GLA_HEADS // 2, LANES, GLA_VAL_DIM)
    o_gla_s, s_gla_s = _gla(tok(qgs), tok(kgs), tok(vgs), tok(las), s0, t_dec)
    x2s, h2s, eids, wsels = _merge(x_sample.reshape(ns, d), o_fox_s.reshape(ns, FOX_WIDTH),
                                   o_gla_s.reshape(ns, GLA_V), mw, ns)
    y_sample = _moe_dense(x2s, h2s, eids, wsels, wg, wu, wd, g_fin).reshape(db, t_dec, d)

    heads = lambda a: a.reshape(a.shape[:-1] + (FOX_HEADS, FOX_HEAD_DIM))
    with_meta = lambda m, r: jnp.concatenate([jnp.broadcast_to(m, (b,) + m.shape[1:]), r], axis=1)
    new_k_prompt = heads(with_meta(k_m, k))[None]
    new_v_prompt = heads(with_meta(v_m, v))[None]
    new_log_f_prompt = with_meta(lf_m, lf)[None]
    new_gla_prompt = s_gla_p.reshape(1, b, GLA_HEADS, GLA_KEY_DIM, GLA_VAL_DIM)
    new_k_sample = heads(tok(ks))[None]
    new_v_sample = heads(tok(vs))[None]
    new_log_f_sample = lfs.reshape(1, db, t_dec, FOX_HEADS)
    new_gla_sample = s_gla_s.reshape(1, db, GLA_HEADS, GLA_KEY_DIM, GLA_VAL_DIM)
    return (y_prompt, y_sample, new_k_prompt, new_v_prompt, new_log_f_prompt, new_gla_prompt,
            new_k_sample, new_v_sample, new_log_f_sample, new_gla_sample)
```

```python
import functools

import numpy as np
import jax
import jax.numpy as jnp
from jax import lax
from jax.experimental import pallas as pl
from jax.experimental.pallas import tpu as pltpu

F32 = jnp.float32
BF16 = jnp.bfloat16
I32 = jnp.int32
EPS = 1e-6
LOG2E = 1.4426950408889634
N_META = 16
FOX_HEADS = 8
FOX_HEAD_DIM = 64
FOX_WIDTH = FOX_HEADS * FOX_HEAD_DIM
GLA_HEADS = 4
GLA_KEY_DIM = 64
GLA_VAL_DIM = 128
GLA_K = GLA_HEADS * GLA_KEY_DIM
GLA_V = GLA_HEADS * GLA_VAL_DIM
GLA_RANK = 16
GLA_TAU = 16.0
GLA_CHUNK = 128
N_GROUPS = 4
EXPERTS_PER_GROUP = 8
N_EXPERTS = N_GROUPS * EXPERTS_PER_GROUP
PAGE_SIZE = 128

LANES = 128
SUBLANES = 8
VMEM_LIMIT = 56 * 1024 * 1024

PROJ_TILE = 512
ATTN_TILE = 512
MERGE_TILE = 512
MOE_TOKENS = 512
MOE_ROWS = 512
RUN_ALIGN = SUBLANES
RUN_LARGE = 64
PAGES_PER_STEP = 32
GLA_SEQS_PER_STEP = 4
W_COLS = LANES


def _cparams(sem):
    return pltpu.CompilerParams(dimension_semantics=sem, vmem_limit_bytes=VMEM_LIMIT)


def _dot(a, b):
    return jnp.dot(a, b, preferred_element_type=F32)


def _dot_nt(a, b):
    return lax.dot_general(a, b, (((1,), (1,)), ((), ())), preferred_element_type=F32)


def _dot_tn(a, b):
    return lax.dot_general(a, b, (((0,), (0,)), ((), ())), preferred_element_type=F32)


def _split3(a):
    hi = a.astype(BF16)
    r1 = a - hi.astype(F32)
    mid = r1.astype(BF16)
    lo = (r1 - mid.astype(F32)).astype(BF16)
    return hi, mid, lo


def _dot_sel_l(m, a):
    hi, mid, lo = _split3(a)
    return _dot(m, lo) + _dot(m, mid) + _dot(m, hi)


def _dot_sel_r(a, m):
    hi, mid, lo = _split3(a)
    return _dot(lo, m) + _dot(mid, m) + _dot(hi, m)


def _rmsnorm(x, g):
    return x * lax.rsqrt(jnp.mean(x * x, axis=-1, keepdims=True) + EPS) * g


def _log_sigmoid(x):
    return jnp.minimum(x, 0.0) - jnp.log1p(jnp.exp(-jnp.abs(x)))


def _sigmoid(x):
    return 1.0 / (1.0 + jnp.exp(-x))


def _col_from_row(row, n):
    r = lax.broadcasted_iota(I32, (n, n), 0)
    c = lax.broadcasted_iota(I32, (n, n), 1)
    return jnp.sum(jnp.where(r == c, jnp.broadcast_to(row, (n, n)), 0.0), axis=1, keepdims=True)


def _proj_kernel(x_ref, g_ref, wqkv_ref, wfft_ref, wgla_ref, wlr_ref, wa2_ref, ba_ref, fb_ref,
                 wvt_ref, wff_ref, fbr_ref,
                 q_ref, k_ref, v_ref, kb_ref, vb_ref, lft_ref, qg_ref, kg_ref, vg_ref, la_ref, vt_ref, lf_ref):
    h = _rmsnorm(x_ref[0], g_ref[...]).astype(BF16)
    z = _dot(h, wqkv_ref[...])
    q_ref[0] = (z[:, :FOX_WIDTH] * (FOX_HEAD_DIM ** -0.5 * LOG2E)).astype(BF16)
    k = z[:, FOX_WIDTH:2 * FOX_WIDTH]
    v = z[:, 2 * FOX_WIDTH:]
    k_ref[0] = k
    v_ref[0] = v
    kb_ref[0] = k.astype(BF16)
    vb_ref[0] = v.astype(BF16)
    vt_ref[0, 0] = _dot_nt(wvt_ref[...], h).astype(BF16)
    fft = _dot_nt(wfft_ref[...], h)[:FOX_HEADS]
    lft_ref[0] = _log_sigmoid(fft + fb_ref[...])
    lf_ref[0] = _log_sigmoid(_dot(h, wff_ref[...]) + fbr_ref[...])
    zg = _dot(h, wgla_ref[...])
    qg_ref[0] = zg[:, :GLA_K] * GLA_KEY_DIM ** -0.5
    kg_ref[0] = zg[:, GLA_K:2 * GLA_K]
    vg_ref[0] = zg[:, 2 * GLA_K:]
    lr = _dot(h, wlr_ref[...])
    xa = _dot(lr.astype(BF16), wa2_ref[...]) + ba_ref[...]
    la_ref[0] = _log_sigmoid(xa) / GLA_TAU


def _project(x3, tm, pw):
    bx, lx, d = x3.shape
    assert lx % tm == 0
    grid = (bx, lx // tm)
    row = lambda w: pl.BlockSpec((1, tm, w), lambda b, i: (b, i, 0))
    full = lambda a: pl.BlockSpec(a.shape, lambda b, i: (0,) * a.ndim)
    ws = (pw["g_mix"], pw["w_qkv"], pw["w_fft"], pw["w_gla"], pw["w_lr"], pw["w_a2"], pw["b_a"], pw["f_bias"],
          pw["w_vt"], pw["w_ff"], pw["f_bias_row"])
    out_shape = (
        jax.ShapeDtypeStruct((bx, lx, FOX_WIDTH), BF16),
        jax.ShapeDtypeStruct((bx, lx, FOX_WIDTH), F32),
        jax.ShapeDtypeStruct((bx, lx, FOX_WIDTH), F32),
        jax.ShapeDtypeStruct((bx, lx, FOX_WIDTH), BF16),
        jax.ShapeDtypeStruct((bx, lx, FOX_WIDTH), BF16),
        jax.ShapeDtypeStruct((bx, FOX_HEADS, lx), F32),
        jax.ShapeDtypeStruct((bx, lx, GLA_K), F32),
        jax.ShapeDtypeStruct((bx, lx, GLA_K), F32),
        jax.ShapeDtypeStruct((bx, lx, GLA_V), F32),
        jax.ShapeDtypeStruct((bx, lx, GLA_K), F32),
        jax.ShapeDtypeStruct((bx, lx // tm, FOX_WIDTH, tm), BF16),
        jax.ShapeDtypeStruct((bx, lx, FOX_HEADS), F32),
    )
    out_specs = (row(FOX_WIDTH), row(FOX_WIDTH), row(FOX_WIDTH), row(FOX_WIDTH), row(FOX_WIDTH),
                 pl.BlockSpec((1, FOX_HEADS, tm), lambda b, i: (b, 0, i)),
                 row(GLA_K), row(GLA_K), row(GLA_V), row(GLA_K),
                 pl.BlockSpec((1, 1, FOX_WIDTH, tm), lambda b, i: (b, i, 0, 0)), row(FOX_HEADS))
    return pl.pallas_call(
        _proj_kernel, grid=grid,
        in_specs=[row(d)] + [full(a) for a in ws],
        out_specs=out_specs, out_shape=out_shape,
        compiler_params=_cparams(("parallel", "parallel")), name="proj",
    )(x3, *ws)


def _online_update_t(s, s_max, vt, state):
    m, acc = state
    m_new = jnp.maximum(m, s_max)
    p = jnp.exp2(s - m_new)
    return m_new, jnp.exp2(m - m_new) * acc + _dot(vt, p.astype(BF16))


def _with_ones_rows(vt):
    return jnp.concatenate([vt, jnp.ones((2 * SUBLANES, vt.shape[1]), BF16)], axis=0)


def _fox_prompt_kernel(q_ref, kb_ref, vt_ref, lf_ref, km_ref, vtm_ref, lfm_ref, tri_ref, um_ref,
                       o_ref, c_ref, s00_ref, s01_ref, s10_ref, s11_ref, cm00_ref, cm01_ref, cm10_ref, cm11_ref,
                       *, tq):
    qi = pl.program_id(1)
    n_blk = kb_ref.shape[1] // tq
    dh = FOX_HEAD_DIM
    s_refs = ((s00_ref, s01_ref), (s10_ref, s11_ref))
    cm_refs = ((cm00_ref, cm01_ref), (cm10_ref, cm11_ref))

    @pl.when(qi == 0)
    def _():
        carry = jnp.zeros((1, FOX_HEADS), F32)
        for jb in range(n_blk):
            for js in range(tq // LANES):
                off = jb * tq + js * LANES
                cs = _dot_sel_l(tri_ref[...], lf_ref[0, off:off + LANES, :]) + carry
                c_ref[jb, js * LANES:(js + 1) * LANES, :] = cs * LOG2E
                carry = cs[LANES - 1:LANES, :]

    bias_meta = _dot_sel_l(um_ref[...], lfm_ref[...]) * LOG2E
    lane = lax.broadcasted_iota(I32, (tq, LANES), 1)
    key = lax.broadcasted_iota(I32, (tq, tq), 0)
    qry = lax.broadcasted_iota(I32, (tq, tq), 1)
    in_a = lane < dh
    for hp in range(FOX_HEADS // 2):
        sl = slice(hp * LANES, (hp + 1) * LANES)
        ha, hb = 2 * hp, 2 * hp + 1
        qp = q_ref[0, :, sl].astype(F32)
        qa = jnp.where(in_a, qp, 0.0).astype(BF16)
        qb = jnp.where(in_a, 0.0, qp).astype(BF16)
        km = km_ref[:, sl]

        def first(qh, h, km=km):
            s = _dot_nt(km, qh) + bias_meta[:, h:h + 1]
            m = jnp.max(s, axis=0, keepdims=True)
            p = jnp.exp2(s - m)
            return m, _dot(_with_ones_rows(vtm_ref[h * dh:(h + 1) * dh, :]), p.astype(BF16))

        def raw_scores(j, qa=qa, qb=qb, ha=ha, hb=hb, sl=sl):
            start = pl.multiple_of(j * tq, tq)
            kc = kb_ref[0, pl.ds(start, tq), sl]
            cj = c_ref[j]
            return _dot_nt(kc, qa) - cj[:, ha:ha + 1], _dot_nt(kc, qb) - cj[:, hb:hb + 1]

        def values(j, sub, ha=ha, hb=hb):
            h = (ha, hb)[sub]
            return _with_ones_rows(vt_ref[0, j, h * dh:(h + 1) * dh, :])

        def score(j, buf):
            for sub, s in enumerate(raw_scores(j)):
                s_refs[buf][sub][...] = s
                cm_refs[buf][sub][...] = jnp.max(s, axis=0, keepdims=True)

        def advance(j, buf, states):
            return tuple(_online_update_t(s_refs[buf][sub][...], cm_refs[buf][sub][...], values(j, sub), states[sub])
                         for sub in range(2))

        def two_blocks(i, states):
            j = 2 * i
            score(j + 1, 1)
            states = advance(j, 0, states)
            score(jnp.minimum(j + 2, qi - 1), 0)
            return advance(j + 1, 1, states)

        score(0, 0)
        states = []
        for sub, (s, st) in enumerate(zip(raw_scores(qi), (first(qa, ha), first(qb, hb)))):
            s = jnp.where(key <= qry, s, -jnp.inf)
            states.append(_online_update_t(s, jnp.max(s, axis=0, keepdims=True), values(qi, sub), st))
        states = lax.fori_loop(0, qi // 2, two_blocks, tuple(states))
        (_, acc_a), (_, acc_b) = lax.cond(qi % 2 == 1, lambda st: advance(qi - 1, 0, st), lambda st: st, states)
        o_t = jnp.concatenate([acc_a[:dh] / acc_a[dh:dh + 1], acc_b[:dh] / acc_b[dh:dh + 1]], axis=0)
        o_ref[0, :, sl] = o_t.T.astype(BF16)


def _fox_prompt(q, kb, vt, lf, kb_meta, vt_meta, lf_meta, tq):
    b, l, w = q.shape
    assert l % tq == 0 and vt.shape == (b, l // tq, w, tq)
    n_blk = l // tq
    tri = jnp.asarray(np.tril(np.ones((LANES, LANES), np.float32)), BF16)
    um = jnp.asarray(np.triu(np.ones((N_META, N_META), np.float32), 1), BF16)
    full = lambda a: pl.BlockSpec(a.shape, lambda bi, qi: (0,) * a.ndim)
    return pl.pallas_call(
        functools.partial(_fox_prompt_kernel, tq=tq), grid=(b, n_blk),
        in_specs=[pl.BlockSpec((1, tq, w), lambda bi, qi: (bi, qi, 0)),
                  pl.BlockSpec((1, l, w), lambda bi, qi: (bi, 0, 0)),
                  pl.BlockSpec((1, n_blk, w, tq), lambda bi, qi: (bi, 0, 0, 0)),
                  pl.BlockSpec((1, l, FOX_HEADS), lambda bi, qi: (bi, 0, 0)),
                  full(kb_meta), full(vt_meta), full(lf_meta), full(tri), full(um)],
        out_specs=pl.BlockSpec((1, tq, w), lambda bi, qi: (bi, qi, 0)),
        out_shape=jax.ShapeDtypeStruct((b, l, w), BF16),
        scratch_shapes=([pltpu.VMEM((n_blk, tq, FOX_HEADS), F32)] + [pltpu.VMEM((tq, tq), F32)] * 4
                        + [pltpu.VMEM((1, tq), F32)] * 4),
        compiler_params=_cparams(("parallel", "arbitrary")), name="fox_prompt",
    )(q, kb, vt, lf, kb_meta, vt_meta, lf_meta, tri, um)


def _fox_sample_kernel(pt_ref, q_ref, kn_ref, vn_ref, lfn_ref, t8_ref, u_ref, *rest, g, t):
    ck = rest[:g]
    cv = rest[g:2 * g]
    clf = rest[2 * g:3 * g]
    o_ref, qbd_ref, m_ref, l_ref, acc_ref, r_ref = rest[3 * g:]
    gi = pl.program_id(1)
    rows = t * FOX_HEADS
    own = (lax.broadcasted_iota(I32, (rows, FOX_WIDTH), 0) % FOX_HEADS
           == lax.broadcasted_iota(I32, (rows, FOX_WIDTH), 1) // FOX_HEAD_DIM)

    @pl.when(gi == 0)
    def _():
        q = q_ref[0].astype(F32)
        qt = jnp.concatenate([jnp.broadcast_to(q[i:i + 1], (FOX_HEADS, FOX_WIDTH)) for i in range(t)], axis=0)
        qbd = jnp.where(own, qt, 0.0).astype(BF16)
        qbd_ref[...] = qbd
        a_t = _dot_sel_r(lfn_ref[0], t8_ref[...])
        s = _dot_nt(qbd, kn_ref[0]) - jnp.concatenate([a_t * LOG2E] * t, axis=0)
        rt = lax.broadcasted_iota(I32, (rows, t), 0) // FOX_HEADS
        ct = lax.broadcasted_iota(I32, (rows, t), 1)
        s = jnp.where(ct <= rt, s, -jnp.inf)
        m = jnp.max(s, axis=1, keepdims=True)
        p = jnp.exp2(s - m)
        m_ref[...] = m
        l_ref[...] = jnp.sum(p, axis=1, keepdims=True)
        acc_ref[...] = _dot(p.astype(BF16), vn_ref[0])
        r_ref[...] = jnp.zeros_like(r_ref)

    lf_all = jnp.concatenate([clf[jj][0] for jj in range(g)], axis=0)
    after = _dot_sel_r(lf_all, u_ref[...])
    r_run = r_ref[:, 0:1]
    qbd = qbd_ref[...]
    scores = [None] * g
    for jj in reversed(range(g)):
        bias = (after[jj * FOX_HEADS:(jj + 1) * FOX_HEADS] + r_run) * LOG2E
        r_run = r_run + jnp.sum(clf[jj][0], axis=1, keepdims=True)
        kt = ck[jj][0].reshape(FOX_WIDTH, PAGE_SIZE).astype(BF16)
        scores[jj] = _dot(qbd, kt) + jnp.concatenate([bias] * t, axis=0)
    r_ref[...] = jnp.broadcast_to(r_run, r_ref.shape)
    m = m_ref[...]
    m_new = m
    for s in scores:
        m_new = jnp.maximum(m_new, jnp.max(s, axis=1, keepdims=True))
    alpha = jnp.exp2(m - m_new)
    l = alpha * l_ref[...]
    acc = alpha * acc_ref[...]
    for jj, s in enumerate(scores):
        p = jnp.exp2(s - m_new)
        l = l + jnp.sum(p, axis=1, keepdims=True)
        acc = acc + _dot_nt(p.astype(BF16), cv[jj][0].reshape(FOX_WIDTH, PAGE_SIZE).astype(BF16))
    m_ref[...] = m_new
    l_ref[...] = l
    acc_ref[...] = acc

    @pl.when(gi == pl.num_programs(1) - 1)
    def _():
        o = jnp.where(own, acc / l, 0.0)
        o_ref[0] = jnp.concatenate(
            [jnp.sum(o[i * FOX_HEADS:(i + 1) * FOX_HEADS], axis=0, keepdims=True) for i in range(t)],
            axis=0).astype(BF16)


def _fox_sample(q, kb_new, vb_new, lft_new, cache_kt, cache_vt, cache_lft, page_table):
    db, t, w = q.shape
    n_pages = page_table.shape[1]
    g = min(PAGES_PER_STEP, n_pages)
    assert n_pages % g == 0
    ng = n_pages // g
    rows = FOX_HEADS * t
    t8 = jnp.asarray(np.triu(np.ones((t, t), np.float32)), BF16)
    u = jnp.asarray(np.tril(np.ones((PAGE_SIZE, PAGE_SIZE), np.float32), -1), BF16)
    tok = lambda wd: pl.BlockSpec((1, t, wd), lambda b, gi, pt: (b, 0, 0))
    full = lambda a: pl.BlockSpec(a.shape, lambda b, gi, pt: (0,) * a.ndim)

    def page(shape, jj):
        zeros = (0,) * (len(shape) - 1)
        return pl.BlockSpec(shape, lambda b, gi, pt: (pt[b * n_pages + (ng - 1 - gi) * g + jj],) + zeros)

    kv_page = (1, FOX_HEADS, FOX_HEAD_DIM, PAGE_SIZE)
    in_specs = ([tok(w), tok(w), tok(w), pl.BlockSpec((1, FOX_HEADS, t), lambda b, gi, pt: (b, 0, 0)),
                 full(t8), full(u)]
                + [page(kv_page, jj) for jj in range(g)]
                + [page(kv_page, jj) for jj in range(g)]
                + [page((1, FOX_HEADS, PAGE_SIZE), jj) for jj in range(g)])
    grid_spec = pltpu.PrefetchScalarGridSpec(
        num_scalar_prefetch=1, grid=(db, ng), in_specs=in_specs,
        out_specs=pl.BlockSpec((1, t, w), lambda b, gi, pt: (b, 0, 0)),
        scratch_shapes=[pltpu.VMEM((rows, w), BF16), pltpu.VMEM((rows, 1), F32), pltpu.VMEM((rows, 1), F32),
                        pltpu.VMEM((rows, w), F32), pltpu.VMEM((FOX_HEADS, LANES), F32)])
    return pl.pallas_call(
        functools.partial(_fox_sample_kernel, g=g, t=t), grid_spec=grid_spec,
        out_shape=jax.ShapeDtypeStruct((db, t, w), BF16),
        compiler_params=_cparams(("parallel", "arbitrary")), name="fox_sample",
    )(page_table.reshape(-1), q, kb_new, vb_new, lft_new, t8, u,
      *([cache_kt] * g), *([cache_vt] * g), *([cache_lft] * g))


def _gla_tables(c):
    idx = np.arange(c)
    ws = [np.tril(np.ones((c, c), np.float32))]
    ms = [np.eye(c, dtype=np.float32)]
    sz = 2
    while sz <= c:
        seg = idx // sz
        upper = (idx % sz) >= sz // 2
        mid = seg * sz + sz // 2 - 1
        j = idx[None, :]
        r = idx[:, None]
        w_up = (j > mid[:, None]) & (j <= r)
        w_lo = (j > r) & (j <= mid[:, None])
        ws.append(np.where(upper[:, None], w_up, w_lo).astype(np.float32))
        ms.append(((seg[:, None] == seg[None, :]) & upper[:, None] & ~upper[None, :]).astype(np.float32))
        sz *= 2
    return (jnp.asarray(np.concatenate(ws, axis=0), BF16),
            jnp.asarray(np.stack([np.concatenate([m, m], axis=0) for m in ms]), F32))


def _gla_kernel(q_ref, k_ref, v_ref, la_ref, s0_ref, wall_ref, msk_ref, o_ref, sfin_ref, state_ref,
                *, c, n_levels, nb, shared_s0):
    ci = pl.program_id(1)

    @pl.when(ci == 0)
    def _():
        for bi in range(nb):
            state_ref[bi] = s0_ref[0 if shared_s0 else bi]

    lane = lax.broadcasted_iota(I32, (c, LANES), 1)
    r128 = lax.broadcasted_iota(I32, (LANES, LANES), 0)
    half = GLA_KEY_DIM
    in_a = lane < half

    def stack_heads(x):
        return jnp.concatenate([jnp.where(in_a, x, 0.0), jnp.where(in_a, 0.0, x)], axis=0).astype(BF16)

    for bi in range(nb):
        la_hi, la_mid, _ = _split3(la_ref[bi])
        seg = _dot(wall_ref[...], jnp.concatenate([la_hi, la_mid], axis=1))
        seg = seg[:, GLA_K:] + seg[:, :GLA_K]
        b_all = seg[:c]
        e_all = jnp.exp(seg[c:])
        for p in range(GLA_HEADS // 2):
            ksl = slice(p * LANES, (p + 1) * LANES)
            q = q_ref[bi, :, ksl]
            k = k_ref[bi, :, ksl]
            v16 = v_ref[bi, :, 2 * p * LANES:(2 * p + 2) * LANES].astype(BF16)
            sab = msk_ref[0] * _dot_nt(stack_heads(q), k.astype(BF16))
            for lv in range(n_levels):
                e = e_all[lv * c:(lv + 1) * c, ksl]
                sab = sab + msk_ref[lv + 1] * _dot_nt(stack_heads(q * e), (k * e).astype(BF16))
            st = state_ref[bi, p]
            b = b_all[:, ksl]
            inter = _dot(stack_heads(q * jnp.exp(b)), st.astype(BF16))
            s16 = sab.astype(BF16)
            o_ref[bi, :, 2 * p * LANES:(2 * p + 1) * LANES] = inter[:c] + _dot(s16[:c], v16[:, :LANES])
            o_ref[bi, :, (2 * p + 1) * LANES:(2 * p + 2) * LANES] = inter[c:] + _dot(s16[c:], v16[:, LANES:])
            b_end = b[c - 1:c, :]
            kdec = (k * jnp.exp(b_end - b)).astype(BF16)
            upd = _dot_tn(kdec, v16)
            upd = jnp.where(r128 < half, upd[:, :LANES], upd[:, LANES:])
            state_ref[bi, p] = _col_from_row(jnp.exp(b_end), LANES) * st + upd

    @pl.when(ci == pl.num_programs(1) - 1)
    def _():
        sfin_ref[...] = state_ref[...]


def _gla(qg, kg, vg, la, s0, c):
    b, l, _ = qg.shape
    assert l % c == 0
    nb = GLA_SEQS_PER_STEP
    while b % nb:
        nb //= 2
    wall, msk = _gla_tables(c)
    n_levels = msk.shape[0] - 1
    shared_s0 = s0.shape[0] != b
    st_shape = (GLA_HEADS // 2, LANES, GLA_VAL_DIM)
    s0_spec = (pl.BlockSpec((1,) + st_shape, lambda bi, ci: (0, 0, 0, 0)) if shared_s0
               else pl.BlockSpec((nb,) + st_shape, lambda bi, ci: (bi, 0, 0, 0)))
    tok = lambda wd: pl.BlockSpec((nb, c, wd), lambda bi, ci: (bi, ci, 0))
    full = lambda a: pl.BlockSpec(a.shape, lambda bi, ci: (0,) * a.ndim)
    return pl.pallas_call(
        functools.partial(_gla_kernel, c=c, n_levels=n_levels, nb=nb, shared_s0=shared_s0),
        grid=(b // nb, l // c),
        in_specs=[tok(GLA_K), tok(GLA_K), tok(GLA_V), tok(GLA_K), s0_spec, full(wall), full(msk)],
        out_specs=(tok(GLA_V), pl.BlockSpec((nb,) + st_shape, lambda bi, ci: (bi, 0, 0, 0))),
        out_shape=(jax.ShapeDtypeStruct((b, l, GLA_V), F32), jax.ShapeDtypeStruct((b,) + st_shape, F32)),
        scratch_shapes=[pltpu.VMEM((nb,) + st_shape, F32)],
        compiler_params=_cparams(("parallel", "arbitrary")), name="gla",
    )(qg, kg, vg, la, s0, wall, msk)


def _merge_kernel(x_ref, of_ref, og_ref, gmix_ref, wrg_ref, wfo_ref, wgo_ref, wo_ref, gng_ref, gffn_ref,
                  wrt_ref, brt_ref, x2_ref, h2_ref, eid_ref, wsel_ref):
    x = x_ref[...]
    tm, d = x.shape
    h = _rmsnorm(x, gmix_ref[...]).astype(BF16)
    z = _dot(h, wrg_ref[...])
    r = z[:, :GLA_V]
    gate_f = z[:, GLA_V:GLA_V + d]
    gate_g = z[:, GLA_V + d:]
    og = og_ref[...]
    parts = []
    for hh in range(GLA_HEADS):
        blk = og[:, hh * GLA_VAL_DIM:(hh + 1) * GLA_VAL_DIM]
        parts.append(blk * lax.rsqrt(jnp.mean(blk * blk, axis=-1, keepdims=True) + EPS))
    ogn = jnp.concatenate(parts, axis=1)
    ogs = (ogn * gng_ref[...] * (r * _sigmoid(r))).astype(BF16)
    branch_f = _dot(of_ref[...], wfo_ref[...])
    branch_g = _dot(ogs, wgo_ref[...])
    merged = _sigmoid(gate_f) * branch_f + _sigmoid(gate_g) * branch_g
    x2 = x + _dot(merged.astype(BF16), wo_ref[...])
    x2_ref[...] = x2
    h2 = _rmsnorm(x2, gffn_ref[...])
    h2_ref[...] = h2.astype(BF16)

    a = _split3(h2)
    wr = (wrt_ref[0], wrt_ref[1], wrt_ref[2])
    lt = (_dot_nt(wr[1], a[1]) + _dot_nt(wr[0], a[2]) + _dot_nt(wr[2], a[0])
          + _dot_nt(wr[0], a[1]) + _dot_nt(wr[1], a[0]) + _dot_nt(wr[0], a[0])) + brt_ref[...]
    row8 = lax.broadcasted_iota(I32, (SUBLANES, tm), 0)
    gl = jnp.where(row8 < N_GROUPS, lt[:SUBLANES], -jnp.inf)
    gmax = jnp.max(gl, axis=0, keepdims=True)
    g_idx = jnp.min(jnp.where(gl == gmax, row8, SUBLANES), axis=0, keepdims=True)
    g_sel = 1.0 / jnp.sum(jnp.exp(gl - gmax), axis=0, keepdims=True)
    rowe = lax.broadcasted_iota(I32, (N_EXPERTS, tm), 0)
    e1 = jnp.where(rowe // EXPERTS_PER_GROUP == g_idx, lt[SUBLANES:], -jnp.inf)
    v1 = jnp.max(e1, axis=0, keepdims=True)
    i1 = jnp.min(jnp.where(e1 == v1, rowe, N_EXPERTS), axis=0, keepdims=True)
    e2 = jnp.where(rowe == i1, -jnp.inf, e1)
    v2 = jnp.max(e2, axis=0, keepdims=True)
    i2 = jnp.min(jnp.where(e2 == v2, rowe, N_EXPERTS), axis=0, keepdims=True)
    tt = jnp.exp(v2 - v1)
    eid_ref[...] = jnp.concatenate([i1, i2], axis=0)
    wsel_ref[...] = jnp.concatenate([g_sel / (1.0 + tt), g_sel * tt / (1.0 + tt)], axis=0)


def _merge(x, o_fox, o_gla, mw, tm):
    n, d = x.shape
    assert n % tm == 0
    row = lambda wd: pl.BlockSpec((tm, wd), lambda i: (i, 0))
    full = lambda a: pl.BlockSpec(a.shape, lambda i: (0,) * a.ndim, pipeline_mode=pl.Buffered(1))
    ws = (mw["g_mix"], mw["w_rg"], mw["w_fo"], mw["w_go"], mw["w_o"], mw["g_gla"], mw["g_ffn"],
          mw["w_rt"], mw["b_rt"])
    return pl.pallas_call(
        _merge_kernel, grid=(n // tm,),
        in_specs=[row(d), row(FOX_WIDTH), row(GLA_V)] + [full(a) for a in ws],
        out_specs=(row(d), row(d), pl.BlockSpec((2, tm), lambda i: (0, i)), pl.BlockSpec((2, tm), lambda i: (0, i))),
        out_shape=(jax.ShapeDtypeStruct((n, d), F32), jax.ShapeDtypeStruct((n, d), BF16),
                   jax.ShapeDtypeStruct((2, n), I32), jax.ShapeDtypeStruct((2, n), F32)),
        compiler_params=_cparams(("parallel",)), name="merge",
    )(x, o_fox, o_gla, *ws)


def _run_sizes(max_rows):
    sizes = []
    s = RUN_ALIGN
    while s <= max_rows:
        sizes.append(s)
        s *= 2
    return tuple(reversed(sizes))


def _for_each_run_piece(n_rows, a_off, b_off, sizes, fn):
    def pieces(n, a, b, szs):
        for size in szs:
            bit = (n & size) != 0

            @pl.when(bit)
            def _(a=a, b=b, size=size):
                fn(pl.multiple_of(a, RUN_ALIGN), pl.multiple_of(b, RUN_ALIGN), size)

            inc = jnp.where(bit, size, 0)
            a = a + inc
            b = b + inc

    large = [s for s in sizes if s >= RUN_LARGE]
    small = [s for s in sizes if s < RUN_LARGE]
    n_large = n_rows - (n_rows & (RUN_LARGE - 1))
    if large:
        @pl.when(n_large != 0)
        def _():
            pieces(n_large, a_off, b_off, large)
    pieces(n_rows, a_off + n_large, b_off + n_large, small)


def _dispatch_kernel(p_ref, loc_ref, dst_ref, toff_ref, tlen_ref, nu_ref,
                     h2_ref, eid_ref, wsel_ref, locf_ref, triu_ref,
                     xs_ref, pos_ref, xloc_ref, zero_ref, sems, *, nt, bt, sl, d, n_tiles):
    i = pl.program_id(0)
    slot = i % 2
    sizes = _run_sizes(bt)
    t = zero_ref.shape[0]

    def runs(tile, buf, act):
        def body(e, carry):
            idx = tile * N_EXPERTS + e
            _for_each_run_piece(
                p_ref[idx], loc_ref[idx], dst_ref[idx], sizes,
                lambda lo, do, size: act(pltpu.make_async_copy(
                    xloc_ref.at[buf, pl.ds(lo, size)], xs_ref.at[pl.ds(do, size)], sems.at[buf])))
            return carry
        lax.fori_loop(0, N_EXPERTS, body, 0)

    @pl.when(i < nt)
    def _():
        eid = eid_ref[...]
        wsel = wsel_ref[...]
        rowe = lax.broadcasted_iota(I32, (N_EXPERTS, bt), 0)
        oh0 = jnp.where(rowe == eid[0:1], 1.0, 0.0)
        oh1 = jnp.where(rowe == eid[1:2], 1.0, 0.0)
        pref0 = _dot(oh0.astype(BF16), triu_ref[...])
        pref1 = _dot(oh1.astype(BF16), triu_ref[...])
        n0 = jnp.sum(oh0, axis=1, keepdims=True)
        loc = locf_ref[0]
        pos0 = jnp.sum(oh0 * (loc + pref0), axis=0, keepdims=True)
        pos1 = jnp.sum(oh1 * (loc + n0 + pref1), axis=0, keepdims=True)
        pos_ref[...] = jnp.concatenate([pos0, pos1], axis=0)
        srow = lax.broadcasted_iota(I32, (sl, bt), 0).astype(F32)
        m0 = srow == pos0
        m1 = srow == pos1
        perm = jnp.where(m0, 1.0, jnp.where(m1, 1.0, 0.0)).astype(BF16)
        wslot = jnp.sum(jnp.where(m0, wsel[0:1], jnp.where(m1, wsel[1:2], 0.0)), axis=1, keepdims=True)
        xloc_ref[slot, :, :d] = _dot(perm, h2_ref[...])
        xloc_ref[slot, :, d:] = jnp.broadcast_to(wslot, (sl, W_COLS))
        runs(i, slot, lambda cp: cp.start())

    @pl.when(i > 0)
    def _():
        runs(i - 1, 1 - slot, lambda cp: cp.wait())

    @pl.when(i == nt)
    def _():
        zero_ref[...] = jnp.zeros_like(zero_ref)
        tsizes = _run_sizes(t // 2)

        def tail_copy(zo, do, size):
            return pltpu.make_async_copy(zero_ref.at[pl.ds(0, size)], xs_ref.at[pl.ds(do, size)], sems.at[slot])

        def tails(act):
            def body(e, carry):
                _for_each_run_piece(tlen_ref[e], jnp.int32(0), toff_ref[e], tsizes,
                                    lambda zo, do, size: act(tail_copy(zo, do, size)))
                return carry
            lax.fori_loop(0, N_EXPERTS, body, 0)

            def unused(j, carry):
                act(tail_copy(0, pl.multiple_of(j * t, t), t))
                return carry
            lax.fori_loop(nu_ref[0], n_tiles, unused, 0)

        tails(lambda cp: cp.start())
        tails(lambda cp: cp.wait())


def _ffn_kernel(te_ref, nu_ref, xs_ref, wg_ref, wu_ref, wd_ref, y_ref, *, d):
    i = pl.program_id(0)

    @pl.when(i < nu_ref[0])
    def _():
        x = xs_ref[:, :d].astype(BF16)
        g = _dot(x, wg_ref[0])
        u = _dot(x, wu_ref[0])
        a = (g * _sigmoid(g) * u).astype(BF16)
        y = _dot(a, wd_ref[0])
        wrow = xs_ref[:, d:]
        y_ref[...] = y * jnp.concatenate([wrow] * (d // W_COLS), axis=1)

    @pl.when(i >= nu_ref[0])
    def _():
        y_ref[...] = jnp.zeros_like(y_ref)


def _combine_kernel(p_ref, loc_ref, dst_ref, pos_ref, x2_ref, gfin_ref, y_hbm, out_ref, yloc_ref, sems,
                    *, nt, bt, sl):
    i = pl.program_id(0)
    slot = i % 2
    sizes = _run_sizes(bt)

    def runs(tile, buf, act):
        def body(e, carry):
            idx = tile * N_EXPERTS + e
            _for_each_run_piece(
                p_ref[idx], loc_ref[idx], dst_ref[idx], sizes,
                lambda lo, do, size: act(pltpu.make_async_copy(
                    y_hbm.at[pl.ds(do, size)], yloc_ref.at[buf, pl.ds(lo, size)], sems.at[buf])))
            return carry
        lax.fori_loop(0, N_EXPERTS, body, 0)

    @pl.when(i == 0)
    def _():
        yloc_ref[...] = jnp.zeros_like(yloc_ref)
        runs(0, 0, lambda cp: cp.start())

    @pl.when(i + 1 < nt)
    def _():
        runs(i + 1, 1 - slot, lambda cp: cp.start())

    runs(i, slot, lambda cp: cp.wait())
    pos = pos_ref[...]
    pos0 = _col_from_row(pos[0:1], bt)
    pos1 = _col_from_row(pos[1:2], bt)
    scol = lax.broadcasted_iota(I32, (bt, sl), 1).astype(F32)
    perm_t = jnp.where(scol == pos0, 1.0, jnp.where(scol == pos1, 1.0, 0.0)).astype(BF16)
    moe = _dot(perm_t, yloc_ref[slot].astype(BF16))
    out_ref[...] = _rmsnorm(x2_ref[...] + moe, gfin_ref[...])


def _moe_prompt(x2, h2, eid, wsel, wg, wu, wd, g_fin):
    n, d = x2.shape
    bt, t = MOE_TOKENS, MOE_ROWS
    assert n % bt == 0
    nt = n // bt
    sl = -(-(2 * bt + N_EXPERTS * (RUN_ALIGN - 1)) // LANES) * LANES
    onehot = eid.reshape(2, nt, bt)[..., None] == jnp.arange(N_EXPERTS, dtype=I32)
    cnt = jnp.sum(onehot, axis=(0, 2), dtype=I32)
    p = (cnt + RUN_ALIGN - 1) // RUN_ALIGN * RUN_ALIGN
    loc = jnp.cumsum(p, axis=1) - p
    tot = jnp.sum(p, axis=0)
    reg = (tot + t - 1) // t * t
    reg_end = jnp.cumsum(reg)
    reg_start = reg_end - reg
    dst = reg_start[None, :] + jnp.cumsum(p, axis=0) - p
    n_tiles = (2 * n + nt * N_EXPERTS * (RUN_ALIGN - 1) + N_EXPERTS * (t - 1)) // t + 1
    s_max = n_tiles * t
    n_used = (reg_end[-1] // t).astype(I32).reshape(1)
    tile_start = jnp.arange(n_tiles, dtype=I32) * t
    tile_expert = jnp.minimum(jnp.sum(reg_end[None, :] <= tile_start[:, None], axis=1, dtype=I32), N_EXPERTS - 1)
    p_f, loc_f, dst_f = p.reshape(-1), loc.reshape(-1), dst.reshape(-1).astype(I32)
    locf = loc.astype(F32).reshape(nt, N_EXPERTS, 1)
    triu = jnp.asarray(np.triu(np.ones((bt, bt), np.float32), 1), BF16)

    last = nt - 1
    xs, pos = pl.pallas_call(
        functools.partial(_dispatch_kernel, nt=nt, bt=bt, sl=sl, d=d, n_tiles=n_tiles),
        grid_spec=pltpu.PrefetchScalarGridSpec(
            num_scalar_prefetch=6, grid=(nt + 1,),
            in_specs=[pl.BlockSpec((bt, d), lambda i, *_: (jnp.minimum(i, last), 0)),
                      pl.BlockSpec((2, bt), lambda i, *_: (0, jnp.minimum(i, last))),
                      pl.BlockSpec((2, bt), lambda i, *_: (0, jnp.minimum(i, last))),
                      pl.BlockSpec((1, N_EXPERTS, 1), lambda i, *_: (jnp.minimum(i, last), 0, 0)),
                      pl.BlockSpec((bt, bt), lambda i, *_: (0, 0))],
            out_specs=(pl.BlockSpec(memory_space=pl.ANY),
                       pl.BlockSpec((2, bt), lambda i, *_: (0, jnp.minimum(i, last)))),
            scratch_shapes=[pltpu.VMEM((2, sl, d + W_COLS), F32), pltpu.VMEM((t, d + W_COLS), F32),
                            pltpu.SemaphoreType.DMA((2,))]),
        out_shape=(jax.ShapeDtypeStruct((s_max, d + W_COLS), F32), jax.ShapeDtypeStruct((2, n), F32)),
        compiler_params=_cparams(("arbitrary",)), name="moe_dispatch",
    )(p_f, loc_f, dst_f, (reg_start + tot).astype(I32), (reg - tot).astype(I32), n_used, h2, eid, wsel, locf, triu)

    de = wg.shape[2]
    used = lambda i, te, nu: jnp.minimum(i, nu[0] - 1)
    y = pl.pallas_call(
        functools.partial(_ffn_kernel, d=d),
        grid_spec=pltpu.PrefetchScalarGridSpec(
            num_scalar_prefetch=2, grid=(n_tiles,),
            in_specs=[pl.BlockSpec((t, d + W_COLS), lambda i, te, nu: (used(i, te, nu), 0)),
                      pl.BlockSpec((1, d, de), lambda i, te, nu: (te[used(i, te, nu)], 0, 0)),
                      pl.BlockSpec((1, d, de), lambda i, te, nu: (te[used(i, te, nu)], 0, 0)),
                      pl.BlockSpec((1, de, d), lambda i, te, nu: (te[used(i, te, nu)], 0, 0))],
            out_specs=pl.BlockSpec((t, d), lambda i, te, nu: (i, 0))),
        out_shape=jax.ShapeDtypeStruct((s_max, d), F32),
        compiler_params=_cparams(("arbitrary",)), name="moe_ffn",
    )(tile_expert, n_used, xs, wg, wu, wd)

    return pl.pallas_call(
        functools.partial(_combine_kernel, nt=nt, bt=bt, sl=sl),
        grid_spec=pltpu.PrefetchScalarGridSpec(
            num_scalar_prefetch=3, grid=(nt,),
            in_specs=[pl.BlockSpec((2, bt), lambda i, *_: (0, i)),
                      pl.BlockSpec((bt, d), lambda i, *_: (i, 0)),
                      pl.BlockSpec((1, d), lambda i, *_: (0, 0)),
                      pl.BlockSpec(memory_space=pl.ANY)],
            out_specs=pl.BlockSpec((bt, d), lambda i, *_: (i, 0)),
            scratch_shapes=[pltpu.VMEM((2, sl, d), F32), pltpu.SemaphoreType.DMA((2,))]),
        out_shape=jax.ShapeDtypeStruct((n, d), F32),
        compiler_params=_cparams(("arbitrary",)), name="moe_combine",
    )(p_f, loc_f, dst_f, pos, x2, g_fin, y)


def _moe_dense_kernel(x2_ref, h2_ref, eid_ref, wsel_ref, gfin_ref, wg_ref, wu_ref, wd_ref, out_ref, acc_ref):
    e = pl.program_id(0)
    n = x2_ref.shape[0]

    @pl.when(e == 0)
    def _():
        acc_ref[...] = jnp.zeros_like(acc_ref)

    eid = eid_ref[...]
    wsel = wsel_ref[...]
    gate_row = jnp.where(eid[0:1] == e, wsel[0:1], 0.0) + jnp.where(eid[1:2] == e, wsel[1:2], 0.0)
    gate = _col_from_row(gate_row, n)
    x = h2_ref[...]
    g = _dot(x, wg_ref[0])
    u = _dot(x, wu_ref[0])
    a = (g * _sigmoid(g) * u).astype(BF16)
    acc_ref[...] += gate * _dot(a, wd_ref[0])

    @pl.when(e == pl.num_programs(0) - 1)
    def _():
        out_ref[...] = _rmsnorm(x2_ref[...] + acc_ref[...], gfin_ref[...])


def _moe_dense(x2, h2, eid, wsel, wg, wu, wd, g_fin):
    n, d = x2.shape
    de = wg.shape[2]
    full = lambda a: pl.BlockSpec(a.shape, lambda e: (0,) * a.ndim)
    return pl.pallas_call(
        _moe_dense_kernel, grid=(wg.shape[0],),
        in_specs=[full(x2), full(h2), full(eid), full(wsel), full(g_fin),
                  pl.BlockSpec((1, d, de), lambda e: (e, 0, 0)),
                  pl.BlockSpec((1, d, de), lambda e: (e, 0, 0)),
                  pl.BlockSpec((1, de, d), lambda e: (e, 0, 0))],
        out_specs=pl.BlockSpec((n, d), lambda e: (0, 0)),
        out_shape=jax.ShapeDtypeStruct((n, d), F32),
        scratch_shapes=[pltpu.VMEM((n, d), F32)],
        compiler_params=_cparams(("arbitrary",)), name="moe_dense",
    )(x2, h2, eid, wsel, g_fin, wg, wu, wd)


def _layer_weights(norm_mix_g, w_in, fox_f_bias, gla_w_a2, gla_b_a, gla_norm_g, w_fox_out, w_gla_out, w_o,
                   norm_ffn_g, w_group_router, b_group_router, w_expert_router, b_expert_router):
    d = w_in.shape[0]
    c = 0
    parts = {}
    for name, width in (("q", FOX_WIDTH), ("k", FOX_WIDTH), ("v", FOX_WIDTH), ("ff", FOX_HEADS),
                        ("qg", GLA_K), ("kg", GLA_K), ("vg", GLA_V), ("lr", GLA_RANK), ("r", GLA_V),
                        ("gf", d), ("gg", d)):
        parts[name] = w_in[:, c:c + width]
        c += width
    assert c == w_in.shape[1]
    g_mix = norm_mix_g.reshape(1, d)
    w_fft = jnp.zeros((2 * SUBLANES, d), F32).at[:FOX_HEADS].set(parts["ff"].T).astype(BF16)
    pw = dict(
        g_mix=g_mix,
        w_qkv=jnp.concatenate([parts["q"], parts["k"], parts["v"]], axis=1).astype(BF16),
        w_fft=w_fft,
        w_gla=jnp.concatenate([parts["qg"], parts["kg"], parts["vg"]], axis=1).astype(BF16),
        w_lr=parts["lr"].astype(BF16),
        w_a2=gla_w_a2.astype(BF16),
        b_a=gla_b_a.reshape(1, GLA_K),
        f_bias=fox_f_bias.reshape(FOX_HEADS, 1),
        w_vt=parts["v"].T.astype(BF16),
        w_ff=parts["ff"].astype(BF16),
        f_bias_row=fox_f_bias.reshape(1, FOX_HEADS),
    )
    w_rt = jnp.zeros((SUBLANES + N_EXPERTS, d), F32)
    w_rt = w_rt.at[:N_GROUPS].set(w_group_router.T).at[SUBLANES:].set(w_expert_router.T)
    b_rt = jnp.zeros((SUBLANES + N_EXPERTS, 1), F32)
    b_rt = b_rt.at[:N_GROUPS, 0].set(b_group_router).at[SUBLANES:, 0].set(b_expert_router)
    mw = dict(
        g_mix=g_mix,
        w_rg=jnp.concatenate([parts["r"], parts["gf"], parts["gg"]], axis=1).astype(BF16),
        w_fo=w_fox_out.astype(BF16), w_go=w_gla_out.astype(BF16), w_o=w_o.astype(BF16),
        g_gla=gla_norm_g.reshape(1, GLA_V), g_ffn=norm_ffn_g.reshape(1, d),
        w_rt=jnp.stack(_split3(w_rt)), b_rt=b_rt,
    )
    return pw, mw


def kernel(x_prompt, x_sample, cache_k, cache_v, cache_log_f, state_gla, page_table, meta_tokens, norm_mix_g,
           w_in, fox_f_bias, gla_w_a2, gla_b_a, gla_norm_g, w_fox_out, w_gla_out, w_o, norm_ffn_g,
           w_group_router, b_group_router, w_expert_router, b_expert_router, w_expert_gate, w_expert_up,
           w_expert_down, norm_final_g):
    depth = w_in.shape[0]
    assert depth == 1, "meta rows are only carried through one layer"
    b, seq, d = x_prompt.shape
    db, t_dec, _ = x_sample.shape
    n_phys = cache_k.shape[1]
    pw, mw = _layer_weights(norm_mix_g[0], w_in[0], fox_f_bias[0], gla_w_a2[0], gla_b_a[0], gla_norm_g[0],
                            w_fox_out[0], w_gla_out[0], w_o[0], norm_ffn_g[0], w_group_router[0],
                            b_group_router[0], w_expert_router[0], b_expert_router[0])
    wg = w_expert_gate[0].astype(BF16)
    wu = w_expert_up[0].astype(BF16)
    wd = w_expert_down[0].astype(BF16)
    g_fin = norm_final_g.reshape(1, d)

    (_, k_m, v_m, kb_m, _, _, qg_m, kg_m, vg_m, la_m, vt_m, lf_m) = _project(
        meta_tokens.reshape(1, N_META, d), N_META, pw)
    zero_state = jnp.zeros((1, GLA_HEADS // 2, LANES, GLA_VAL_DIM), F32)
    _, s_meta = _gla(qg_m, kg_m, vg_m, la_m, zero_state, N_META)

    tile = min(ATTN_TILE, seq)
    q, k, v, kb, _, _, qg, kg, vg, la, vt, lf = _project(x_prompt, tile, pw)
    o_fox = _fox_prompt(q, kb, vt, lf, kb_m[0], vt_m[0, 0], lf_m[0], tile)
    o_gla, s_gla_p = _gla(qg, kg, vg, la, s_meta, min(GLA_CHUNK, seq))
    n = b * seq
    x2, h2, eid, wsel = _merge(x_prompt.reshape(n, d), o_fox.reshape(n, FOX_WIDTH), o_gla.reshape(n, GLA_V),
                               mw, min(MERGE_TILE, n))
    y_prompt = _moe_prompt(x2, h2, eid, wsel, wg, wu, wd, g_fin).reshape(b, seq, d)

    ns = db * t_dec
    qs, ks, vs, kbs, vbs, lfts, qgs, kgs, vgs, las, _, lfs = _project(x_sample.reshape(1, ns, d), ns, pw)
    tok = lambda a: a.reshape(db, t_dec, a.shape[-1])
    lft_s = lfts.reshape(FOX_HEADS, db, t_dec).transpose(1, 0, 2)
    o_fox_s = _fox_sample(tok(qs), tok(kbs), tok(vbs), lft_s, cache_k[0].transpose(0, 2, 3, 1),
                          cache_v[0].transpose(0, 2, 3, 1), cache_log_f[0].transpose(0, 2, 1), page_table)
    s0 = state_gla[0].reshape(db, GLA_HEADS // 2, LANES, GLA_VAL_DIM)
    o_gla_s, s_gla_s = _gla(tok(qgs), tok(kgs), tok(vgs), tok(las), s0, t_dec)
    x2s, h2s, eids, wsels = _merge(x_sample.reshape(ns, d), o_fox_s.reshape(ns, FOX_WIDTH),
                                   o_gla_s.reshape(ns, GLA_V), mw, ns)
    y_sample = _moe_dense(x2s, h2s, eids, wsels, wg, wu, wd, g_fin).reshape(db, t_dec, d)

    heads = lambda a: a.reshape(a.shape[:-1] + (FOX_HEADS, FOX_HEAD_DIM))
    with_meta = lambda m, r: jnp.concatenate([jnp.broadcast_to(m, (b,) + m.shape[1:]), r], axis=1)
    new_k_prompt = heads(with_meta(k_m, k))[None]
    new_v_prompt = heads(with_meta(v_m, v))[None]
    new_log_f_prompt = with_meta(lf_m, lf)[None]
    new_gla_prompt = s_gla_p.reshape(1, b, GLA_HEADS, GLA_KEY_DIM, GLA_VAL_DIM)
    new_k_sample = heads(tok(ks))[None]
    new_v_sample = heads(tok(vs))[None]
    new_log_f_sample = lfs.reshape(1, db, t_dec, FOX_HEADS)
    new_gla_sample = s_gla_s.reshape(1, db, GLA_HEADS, GLA_KEY_DIM, GLA_VAL_DIM)
    return (y_prompt, y_sample, new_k_prompt, new_v_prompt, new_log_f_prompt, new_gla_prompt,
            new_k_sample, new_v_sample, new_log_f_sample, new_gla_sample)
```

```python
import functools

import numpy as np
import jax
import jax.numpy as jnp
from jax import lax
from jax.experimental import pallas as pl
from jax.experimental.pallas import tpu as pltpu

F32 = jnp.float32
BF16 = jnp.bfloat16
I32 = jnp.int32
EPS = 1e-6
LOG2E = 1.4426950408889634
N_META = 16
FOX_HEADS = 8
FOX_HEAD_DIM = 64
FOX_WIDTH = FOX_HEADS * FOX_HEAD_DIM
GLA_HEADS = 4
GLA_KEY_DIM = 64
GLA_VAL_DIM = 128
GLA_K = GLA_HEADS * GLA_KEY_DIM
GLA_V = GLA_HEADS * GLA_VAL_DIM
GLA_RANK = 16
GLA_TAU = 16.0
GLA_CHUNK = 128
N_GROUPS = 4
EXPERTS_PER_GROUP = 8
N_EXPERTS = N_GROUPS * EXPERTS_PER_GROUP
PAGE_SIZE = 128

LANES = 128
SUBLANES = 8
VMEM_LIMIT = 56 * 1024 * 1024

PROJ_TILE = 512
ATTN_TILE = 512
MERGE_TILE = 512
MOE_TOKENS = 512
MOE_ROWS = 512
RUN_ALIGN = SUBLANES
RUN_LARGE = 64
PAGES_PER_STEP = 32
GLA_SEQS_PER_STEP = 4
W_COLS = LANES


def _cparams(sem):
    return pltpu.CompilerParams(dimension_semantics=sem, vmem_limit_bytes=VMEM_LIMIT)


def _dot(a, b):
    return jnp.dot(a, b, preferred_element_type=F32)


def _dot_nt(a, b):
    return lax.dot_general(a, b, (((1,), (1,)), ((), ())), preferred_element_type=F32)


def _dot_tn(a, b):
    return lax.dot_general(a, b, (((0,), (0,)), ((), ())), preferred_element_type=F32)


def _split3(a):
    hi = a.astype(BF16)
    r1 = a - hi.astype(F32)
    mid = r1.astype(BF16)
    lo = (r1 - mid.astype(F32)).astype(BF16)
    return hi, mid, lo


def _dot_sel_l(m, a):
    hi, mid, lo = _split3(a)
    return _dot(m, lo) + _dot(m, mid) + _dot(m, hi)


def _dot_sel_r(a, m):
    hi, mid, lo = _split3(a)
    return _dot(lo, m) + _dot(mid, m) + _dot(hi, m)


def _rmsnorm(x, g):
    return x * lax.rsqrt(jnp.mean(x * x, axis=-1, keepdims=True) + EPS) * g


def _log_sigmoid(x):
    return jnp.minimum(x, 0.0) - jnp.log1p(jnp.exp(-jnp.abs(x)))


def _sigmoid(x):
    return 1.0 / (1.0 + jnp.exp(-x))


def _col_from_row(row, n):
    r = lax.broadcasted_iota(I32, (n, n), 0)
    c = lax.broadcasted_iota(I32, (n, n), 1)
    return jnp.sum(jnp.where(r == c, jnp.broadcast_to(row, (n, n)), 0.0), axis=1, keepdims=True)


def _proj_kernel(x_ref, g_ref, wqkv_ref, wfft_ref, wgla_ref, wlr_ref, wa2_ref, ba_ref, fb_ref,
                 wvt_ref, wff_ref, fbr_ref, *rest, lead):
    if lead:
        klead_ref, vlead_ref = rest[:2]
        rest = rest[2:]
    q_ref, k_ref, v_ref, kb_ref, vb_ref, lft_ref, qg_ref, kg_ref, vg_ref, la_ref, vt_ref, lf_ref = rest[:12]
    if not lead:
        def plain_store(k, v):
            k_ref[0] = k
            v_ref[0] = v

        _proj_rows(x_ref, g_ref, wqkv_ref, wfft_ref, wgla_ref, wlr_ref, wa2_ref, ba_ref, fb_ref, wvt_ref, wff_ref,
                   fbr_ref, q_ref, kb_ref, vb_ref, lft_ref, qg_ref, kg_ref, vg_ref, la_ref, vt_ref, lf_ref,
                   plain_store)
        return

    kcar_ref, vcar_ref = rest[12:]
    i = pl.program_id(1)
    tm = x_ref.shape[1]

    @pl.when(i == 0)
    def _():
        kcar_ref[...] = klead_ref[...]
        vcar_ref[...] = vlead_ref[...]

    def shifted_store(k, v):
        k_ref[0, :lead] = kcar_ref[...]
        v_ref[0, :lead] = vcar_ref[...]
        k_ref[0, lead:] = k[:tm - lead]
        v_ref[0, lead:] = v[:tm - lead]
        kcar_ref[...] = k[tm - lead:]
        vcar_ref[...] = v[tm - lead:]

    @pl.when(i < pl.num_programs(1) - 1)
    def _():
        _proj_rows(x_ref, g_ref, wqkv_ref, wfft_ref, wgla_ref, wlr_ref, wa2_ref, ba_ref, fb_ref, wvt_ref, wff_ref,
                   fbr_ref, q_ref, kb_ref, vb_ref, lft_ref, qg_ref, kg_ref, vg_ref, la_ref, vt_ref, lf_ref,
                   shifted_store)

    @pl.when(i == pl.num_programs(1) - 1)
    def _():
        zeros = jnp.zeros((tm - lead, k_ref.shape[2]), F32)
        k_ref[0, :lead] = kcar_ref[...]
        v_ref[0, :lead] = vcar_ref[...]
        k_ref[0, lead:] = zeros
        v_ref[0, lead:] = zeros


def _proj_rows(x_ref, g_ref, wqkv_ref, wfft_ref, wgla_ref, wlr_ref, wa2_ref, ba_ref, fb_ref, wvt_ref, wff_ref,
               fbr_ref, q_ref, kb_ref, vb_ref, lft_ref, qg_ref, kg_ref, vg_ref, la_ref, vt_ref, lf_ref, store_kv):
    h = _rmsnorm(x_ref[0], g_ref[...]).astype(BF16)
    z = _dot(h, wqkv_ref[...])
    q_ref[0] = (z[:, :FOX_WIDTH] * (FOX_HEAD_DIM ** -0.5 * LOG2E)).astype(BF16)
    k = z[:, FOX_WIDTH:2 * FOX_WIDTH]
    v = z[:, 2 * FOX_WIDTH:]
    store_kv(k, v)
    kb_ref[0] = k.astype(BF16)
    vb_ref[0] = v.astype(BF16)
    vt_ref[0, 0] = _dot_nt(wvt_ref[...], h).astype(BF16)
    fft = _dot_nt(wfft_ref[...], h)[:FOX_HEADS]
    lft_ref[0] = _log_sigmoid(fft + fb_ref[...])
    lf_ref[0] = _log_sigmoid(_dot(h, wff_ref[...]) + fbr_ref[...])
    zg = _dot(h, wgla_ref[...])
    qg_ref[0] = zg[:, :GLA_K] * GLA_KEY_DIM ** -0.5
    kg_ref[0] = zg[:, GLA_K:2 * GLA_K]
    vg_ref[0] = zg[:, 2 * GLA_K:]
    lr = _dot(h, wlr_ref[...])
    xa = _dot(lr.astype(BF16), wa2_ref[...]) + ba_ref[...]
    la_ref[0] = _log_sigmoid(xa) / GLA_TAU


def _project(x3, tm, pw, lead_kv=None):
    bx, lx, d = x3.shape
    assert lx % tm == 0
    n = lx // tm
    lead = 0 if lead_kv is None else lead_kv[0].shape[0]
    assert lead % SUBLANES == 0 and lead < tm
    grid = (bx, n + 1 if lead else n)
    last = n - 1
    row = lambda w: pl.BlockSpec((1, tm, w), lambda b, i: (b, jnp.minimum(i, last), 0))
    kv_row = pl.BlockSpec((1, tm, FOX_WIDTH), lambda b, i: (b, i, 0))
    full = lambda a: pl.BlockSpec(a.shape, lambda b, i: (0,) * a.ndim)
    ws = (pw["g_mix"], pw["w_qkv"], pw["w_fft"], pw["w_gla"], pw["w_lr"], pw["w_a2"], pw["b_a"], pw["f_bias"],
          pw["w_vt"], pw["w_ff"], pw["f_bias_row"]) + (tuple(lead_kv) if lead else ())
    out_shape = (
        jax.ShapeDtypeStruct((bx, lx, FOX_WIDTH), BF16),
        jax.ShapeDtypeStruct((bx, lx + lead, FOX_WIDTH), F32),
        jax.ShapeDtypeStruct((bx, lx + lead, FOX_WIDTH), F32),
        jax.ShapeDtypeStruct((bx, lx, FOX_WIDTH), BF16),
        jax.ShapeDtypeStruct((bx, lx, FOX_WIDTH), BF16),
        jax.ShapeDtypeStruct((bx, FOX_HEADS, lx), F32),
        jax.ShapeDtypeStruct((bx, lx, GLA_K), F32),
        jax.ShapeDtypeStruct((bx, lx, GLA_K), F32),
        jax.ShapeDtypeStruct((bx, lx, GLA_V), F32),
        jax.ShapeDtypeStruct((bx, lx, GLA_K), F32),
        jax.ShapeDtypeStruct((bx, lx // tm, FOX_WIDTH, tm), BF16),
        jax.ShapeDtypeStruct((bx, lx, FOX_HEADS), F32),
    )
    out_specs = (row(FOX_WIDTH), kv_row, kv_row, row(FOX_WIDTH), row(FOX_WIDTH),
                 pl.BlockSpec((1, FOX_HEADS, tm), lambda b, i: (b, 0, jnp.minimum(i, last))),
                 row(GLA_K), row(GLA_K), row(GLA_V), row(GLA_K),
                 pl.BlockSpec((1, 1, FOX_WIDTH, tm), lambda b, i: (b, jnp.minimum(i, last), 0, 0)), row(FOX_HEADS))
    return pl.pallas_call(
        functools.partial(_proj_kernel, lead=lead), grid=grid,
        in_specs=[row(d)] + [full(a) for a in ws],
        out_specs=out_specs, out_shape=out_shape,
        scratch_shapes=[pltpu.VMEM((lead, FOX_WIDTH), F32)] * 2 if lead else [],
        compiler_params=_cparams(("parallel", "arbitrary" if lead else "parallel")), name="proj",
    )(x3, *ws)


def _online_update_t(s, s_max, vt, state):
    m, acc = state
    m_new = jnp.maximum(m, s_max)
    p = jnp.exp2(s - m_new)
    return m_new, jnp.exp2(m - m_new) * acc + _dot(vt, p.astype(BF16))


def _with_ones_rows(vt):
    return jnp.concatenate([vt, jnp.ones((2 * SUBLANES, vt.shape[1]), BF16)], axis=0)


def _fox_prompt_kernel(q_ref, kb_ref, vt_ref, lf_ref, km_ref, vtm_ref, lfm_ref, tri_ref, um_ref,
                       o_ref, c_ref, s00_ref, s01_ref, s10_ref, s11_ref, cm00_ref, cm01_ref, cm10_ref, cm11_ref,
                       *, tq):
    qi = pl.program_id(1)
    n_blk = kb_ref.shape[1] // tq
    dh = FOX_HEAD_DIM
    s_refs = ((s00_ref, s01_ref), (s10_ref, s11_ref))
    cm_refs = ((cm00_ref, cm01_ref), (cm10_ref, cm11_ref))

    @pl.when(qi == 0)
    def _():
        carry = jnp.zeros((1, FOX_HEADS), F32)
        for jb in range(n_blk):
            for js in range(tq // LANES):
                off = jb * tq + js * LANES
                cs = _dot_sel_l(tri_ref[...], lf_ref[0, off:off + LANES, :]) + carry
                c_ref[jb, js * LANES:(js + 1) * LANES, :] = cs * LOG2E
                carry = cs[LANES - 1:LANES, :]

    bias_meta = _dot_sel_l(um_ref[...], lfm_ref[...]) * LOG2E
    lane = lax.broadcasted_iota(I32, (tq, LANES), 1)
    key = lax.broadcasted_iota(I32, (tq, tq), 0)
    qry = lax.broadcasted_iota(I32, (tq, tq), 1)
    in_a = lane < dh
    for hp in range(FOX_HEADS // 2):
        sl = slice(hp * LANES, (hp + 1) * LANES)
        ha, hb = 2 * hp, 2 * hp + 1
        qp = q_ref[0, :, sl].astype(F32)
        qa = jnp.where(in_a, qp, 0.0).astype(BF16)
        qb = jnp.where(in_a, 0.0, qp).astype(BF16)
        km = km_ref[:, sl]

        def first(qh, h, km=km):
            s = _dot_nt(km, qh) + bias_meta[:, h:h + 1]
            m = jnp.max(s, axis=0, keepdims=True)
            p = jnp.exp2(s - m)
            return m, _dot(_with_ones_rows(vtm_ref[h * dh:(h + 1) * dh, :]), p.astype(BF16))

        def raw_scores(j, qa=qa, qb=qb, ha=ha, hb=hb, sl=sl):
            start = pl.multiple_of(j * tq, tq)
            kc = kb_ref[0, pl.ds(start, tq), sl]
            cj = c_ref[j]
            return _dot_nt(kc, qa) - cj[:, ha:ha + 1], _dot_nt(kc, qb) - cj[:, hb:hb + 1]

        def values(j, sub, ha=ha, hb=hb):
            h = (ha, hb)[sub]
            return _with_ones_rows(vt_ref[0, j, h * dh:(h + 1) * dh, :])

        def score(j, buf):
            for sub, s in enumerate(raw_scores(j)):
                s_refs[buf][sub][...] = s
                cm_refs[buf][sub][...] = jnp.max(s, axis=0, keepdims=True)

        def advance(j, buf, states):
            return tuple(_online_update_t(s_refs[buf][sub][...], cm_refs[buf][sub][...], values(j, sub), states[sub])
                         for sub in range(2))

        def two_blocks(i, states):
            j = 2 * i
            score(j + 1, 1)
            states = advance(j, 0, states)
            score(jnp.minimum(j + 2, qi - 1), 0)
            return advance(j + 1, 1, states)

        score(0, 0)
        states = []
        for sub, (s, st) in enumerate(zip(raw_scores(qi), (first(qa, ha), first(qb, hb)))):
            s = jnp.where(key <= qry, s, -jnp.inf)
            states.append(_online_update_t(s, jnp.max(s, axis=0, keepdims=True), values(qi, sub), st))
        states = lax.fori_loop(0, qi // 2, two_blocks, tuple(states))
        (_, acc_a), (_, acc_b) = lax.cond(qi % 2 == 1, lambda st: advance(qi - 1, 0, st), lambda st: st, states)
        o_t = jnp.concatenate([acc_a[:dh] / acc_a[dh:dh + 1], acc_b[:dh] / acc_b[dh:dh + 1]], axis=0)
        o_ref[0, :, sl] = o_t.T.astype(BF16)


def _fox_prompt(q, kb, vt, lf, kb_meta, vt_meta, lf_meta, tq):
    b, l, w = q.shape
    assert l % tq == 0 and vt.shape == (b, l // tq, w, tq)
    n_blk = l // tq
    tri = jnp.asarray(np.tril(np.ones((LANES, LANES), np.float32)), BF16)
    um = jnp.asarray(np.triu(np.ones((N_META, N_META), np.float32), 1), BF16)
    full = lambda a: pl.BlockSpec(a.shape, lambda bi, qi: (0,) * a.ndim)
    return pl.pallas_call(
        functools.partial(_fox_prompt_kernel, tq=tq), grid=(b, n_blk),
        in_specs=[pl.BlockSpec((1, tq, w), lambda bi, qi: (bi, qi, 0)),
                  pl.BlockSpec((1, l, w), lambda bi, qi: (bi, 0, 0)),
                  pl.BlockSpec((1, n_blk, w, tq), lambda bi, qi: (bi, 0, 0, 0)),
                  pl.BlockSpec((1, l, FOX_HEADS), lambda bi, qi: (bi, 0, 0)),
                  full(kb_meta), full(vt_meta), full(lf_meta), full(tri), full(um)],
        out_specs=pl.BlockSpec((1, tq, w), lambda bi, qi: (bi, qi, 0)),
        out_shape=jax.ShapeDtypeStruct((b, l, w), BF16),
        scratch_shapes=([pltpu.VMEM((n_blk, tq, FOX_HEADS), F32)] + [pltpu.VMEM((tq, tq), F32)] * 4
                        + [pltpu.VMEM((1, tq), F32)] * 4),
        compiler_params=_cparams(("parallel", "arbitrary")), name="fox_prompt",
    )(q, kb, vt, lf, kb_meta, vt_meta, lf_meta, tri, um)


def _fox_sample_kernel(pt_ref, q_ref, kn_ref, vn_ref, lfn_ref, t8_ref, u_ref, *rest, g, t):
    ck = rest[:g]
    cv = rest[g:2 * g]
    clf = rest[2 * g:3 * g]
    o_ref, qbd_ref, m_ref, l_ref, acc_ref, r_ref = rest[3 * g:]
    gi = pl.program_id(1)
    rows = t * FOX_HEADS
    own = (lax.broadcasted_iota(I32, (rows, FOX_WIDTH), 0) % FOX_HEADS
           == lax.broadcasted_iota(I32, (rows, FOX_WIDTH), 1) // FOX_HEAD_DIM)

    @pl.when(gi == 0)
    def _():
        q = q_ref[0].astype(F32)
        qt = jnp.concatenate([jnp.broadcast_to(q[i:i + 1], (FOX_HEADS, FOX_WIDTH)) for i in range(t)], axis=0)
        qbd = jnp.where(own, qt, 0.0).astype(BF16)
        qbd_ref[...] = qbd
        a_t = _dot_sel_r(lfn_ref[0], t8_ref[...])
        s = _dot_nt(qbd, kn_ref[0]) - jnp.concatenate([a_t * LOG2E] * t, axis=0)
        rt = lax.broadcasted_iota(I32, (rows, t), 0) // FOX_HEADS
        ct = lax.broadcasted_iota(I32, (rows, t), 1)
        s = jnp.where(ct <= rt, s, -jnp.inf)
        m = jnp.max(s, axis=1, keepdims=True)
        p = jnp.exp2(s - m)
        m_ref[...] = m
        l_ref[...] = jnp.sum(p, axis=1, keepdims=True)
        acc_ref[...] = _dot(p.astype(BF16), vn_ref[0])
        r_ref[...] = jnp.zeros_like(r_ref)

    lf_all = jnp.concatenate([clf[jj][0] for jj in range(g)], axis=0)
    after = _dot_sel_r(lf_all, u_ref[...])
    r_run = r_ref[:, 0:1]
    qbd = qbd_ref[...]
    scores = [None] * g
    for jj in reversed(range(g)):
        bias = (after[jj * FOX_HEADS:(jj + 1) * FOX_HEADS] + r_run) * LOG2E
        r_run = r_run + jnp.sum(clf[jj][0], axis=1, keepdims=True)
        kt = ck[jj][0].reshape(FOX_WIDTH, PAGE_SIZE).astype(BF16)
        scores[jj] = _dot(qbd, kt) + jnp.concatenate([bias] * t, axis=0)
    r_ref[...] = jnp.broadcast_to(r_run, r_ref.shape)
    m = m_ref[...]
    m_new = m
    for s in scores:
        m_new = jnp.maximum(m_new, jnp.max(s, axis=1, keepdims=True))
    alpha = jnp.exp2(m - m_new)
    l = alpha * l_ref[...]
    acc = alpha * acc_ref[...]
    for jj, s in enumerate(scores):
        p = jnp.exp2(s - m_new)
        l = l + jnp.sum(p, axis=1, keepdims=True)
        acc = acc + _dot_nt(p.astype(BF16), cv[jj][0].reshape(FOX_WIDTH, PAGE_SIZE).astype(BF16))
    m_ref[...] = m_new
    l_ref[...] = l
    acc_ref[...] = acc

    @pl.when(gi == pl.num_programs(1) - 1)
    def _():
        o = jnp.where(own, acc / l, 0.0)
        o_ref[0] = jnp.concatenate(
            [jnp.sum(o[i * FOX_HEADS:(i + 1) * FOX_HEADS], axis=0, keepdims=True) for i in range(t)],
            axis=0).astype(BF16)


def _fox_sample(q, kb_new, vb_new, lft_new, cache_kt, cache_vt, cache_lft, page_table):
    db, t, w = q.shape
    n_pages = page_table.shape[1]
    g = min(PAGES_PER_STEP, n_pages)
    assert n_pages % g == 0
    ng = n_pages // g
    rows = FOX_HEADS * t
    t8 = jnp.asarray(np.triu(np.ones((t, t), np.float32)), BF16)
    u = jnp.asarray(np.tril(np.ones((PAGE_SIZE, PAGE_SIZE), np.float32), -1), BF16)
    tok = lambda wd: pl.BlockSpec((1, t, wd), lambda b, gi, pt: (b, 0, 0))
    full = lambda a: pl.BlockSpec(a.shape, lambda b, gi, pt: (0,) * a.ndim)

    def page(shape, jj):
        zeros = (0,) * (len(shape) - 1)
        return pl.BlockSpec(shape, lambda b, gi, pt: (pt[b * n_pages + (ng - 1 - gi) * g + jj],) + zeros)

    kv_page = (1, FOX_HEADS, FOX_HEAD_DIM, PAGE_SIZE)
    in_specs = ([tok(w), tok(w), tok(w), pl.BlockSpec((1, FOX_HEADS, t), lambda b, gi, pt: (b, 0, 0)),
                 full(t8), full(u)]
                + [page(kv_page, jj) for jj in range(g)]
                + [page(kv_page, jj) for jj in range(g)]
                + [page((1, FOX_HEADS, PAGE_SIZE), jj) for jj in range(g)])
    grid_spec = pltpu.PrefetchScalarGridSpec(
        num_scalar_prefetch=1, grid=(db, ng), in_specs=in_specs,
        out_specs=pl.BlockSpec((1, t, w), lambda b, gi, pt: (b, 0, 0)),
        scratch_shapes=[pltpu.VMEM((rows, w), BF16), pltpu.VMEM((rows, 1), F32), pltpu.VMEM((rows, 1), F32),
                        pltpu.VMEM((rows, w), F32), pltpu.VMEM((FOX_HEADS, LANES), F32)])
    return pl.pallas_call(
        functools.partial(_fox_sample_kernel, g=g, t=t), grid_spec=grid_spec,
        out_shape=jax.ShapeDtypeStruct((db, t, w), BF16),
        compiler_params=_cparams(("parallel", "arbitrary")), name="fox_sample",
    )(page_table.reshape(-1), q, kb_new, vb_new, lft_new, t8, u,
      *([cache_kt] * g), *([cache_vt] * g), *([cache_lft] * g))


def _gla_tables(c):
    idx = np.arange(c)
    ws = [np.tril(np.ones((c, c), np.float32))]
    ms = [np.eye(c, dtype=np.float32)]
    sz = 2
    while sz <= c:
        seg = idx // sz
        upper = (idx % sz) >= sz // 2
        mid = seg * sz + sz // 2 - 1
        j = idx[None, :]
        r = idx[:, None]
        w_up = (j > mid[:, None]) & (j <= r)
        w_lo = (j > r) & (j <= mid[:, None])
        ws.append(np.where(upper[:, None], w_up, w_lo).astype(np.float32))
        ms.append(((seg[:, None] == seg[None, :]) & upper[:, None] & ~upper[None, :]).astype(np.float32))
        sz *= 2
    return (jnp.asarray(np.concatenate(ws, axis=0), BF16),
            jnp.asarray(np.stack([np.concatenate([m, m], axis=0) for m in ms]), F32))


def _gla_kernel(q_ref, k_ref, v_ref, la_ref, s0_ref, wall_ref, msk_ref, o_ref, sfin_ref, state_ref,
                *, c, n_levels, nb, shared_s0):
    ci = pl.program_id(1)

    @pl.when(ci == 0)
    def _():
        for bi in range(nb):
            state_ref[bi] = s0_ref[0 if shared_s0 else bi]

    lane = lax.broadcasted_iota(I32, (c, LANES), 1)
    r128 = lax.broadcasted_iota(I32, (LANES, LANES), 0)
    half = GLA_KEY_DIM
    in_a = lane < half

    def stack_heads(x):
        return jnp.concatenate([jnp.where(in_a, x, 0.0), jnp.where(in_a, 0.0, x)], axis=0).astype(BF16)

    for bi in range(nb):
        la_hi, la_mid, _ = _split3(la_ref[bi])
        seg = _dot(wall_ref[...], jnp.concatenate([la_hi, la_mid], axis=1))
        seg = seg[:, GLA_K:] + seg[:, :GLA_K]
        b_all = seg[:c]
        e_all = jnp.exp(seg[c:])
        for p in range(GLA_HEADS // 2):
            ksl = slice(p * LANES, (p + 1) * LANES)
            q = q_ref[bi, :, ksl]
            k = k_ref[bi, :, ksl]
            v16 = v_ref[bi, :, 2 * p * LANES:(2 * p + 2) * LANES].astype(BF16)
            sab = msk_ref[0] * _dot_nt(stack_heads(q), k.astype(BF16))
            for lv in range(n_levels):
                e = e_all[lv * c:(lv + 1) * c, ksl]
                sab = sab + msk_ref[lv + 1] * _dot_nt(stack_heads(q * e), (k * e).astype(BF16))
            st = state_ref[bi, p]
            b = b_all[:, ksl]
            inter = _dot(stack_heads(q * jnp.exp(b)), st.astype(BF16))
            s16 = sab.astype(BF16)
            o_ref[bi, :, 2 * p * LANES:(2 * p + 1) * LANES] = inter[:c] + _dot(s16[:c], v16[:, :LANES])
            o_ref[bi, :, (2 * p + 1) * LANES:(2 * p + 2) * LANES] = inter[c:] + _dot(s16[c:], v16[:, LANES:])
            b_end = b[c - 1:c, :]
            kdec = (k * jnp.exp(b_end - b)).astype(BF16)
            upd = _dot_tn(kdec, v16)
            upd = jnp.where(r128 < half, upd[:, :LANES], upd[:, LANES:])
            state_ref[bi, p] = _col_from_row(jnp.exp(b_end), LANES) * st + upd

    @pl.when(ci == pl.num_programs(1) - 1)
    def _():
        sfin_ref[...] = state_ref[...]


def _gla(qg, kg, vg, la, s0, c):
    b, l, _ = qg.shape
    assert l % c == 0
    nb = GLA_SEQS_PER_STEP
    while b % nb:
        nb //= 2
    wall, msk = _gla_tables(c)
    n_levels = msk.shape[0] - 1
    shared_s0 = s0.shape[0] != b
    st_shape = (GLA_HEADS // 2, LANES, GLA_VAL_DIM)
    s0_spec = (pl.BlockSpec((1,) + st_shape, lambda bi, ci: (0, 0, 0, 0)) if shared_s0
               else pl.BlockSpec((nb,) + st_shape, lambda bi, ci: (bi, 0, 0, 0)))
    tok = lambda wd: pl.BlockSpec((nb, c, wd), lambda bi, ci: (bi, ci, 0))
    full = lambda a: pl.BlockSpec(a.shape, lambda bi, ci: (0,) * a.ndim)
    return pl.pallas_call(
        functools.partial(_gla_kernel, c=c, n_levels=n_levels, nb=nb, shared_s0=shared_s0),
        grid=(b // nb, l // c),
        in_specs=[tok(GLA_K), tok(GLA_K), tok(GLA_V), tok(GLA_K), s0_spec, full(wall), full(msk)],
        out_specs=(tok(GLA_V), pl.BlockSpec((nb,) + st_shape, lambda bi, ci: (bi, 0, 0, 0))),
        out_shape=(jax.ShapeDtypeStruct((b, l, GLA_V), F32), jax.ShapeDtypeStruct((b,) + st_shape, F32)),
        scratch_shapes=[pltpu.VMEM((nb,) + st_shape, F32)],
        compiler_params=_cparams(("parallel", "arbitrary")), name="gla",
    )(qg, kg, vg, la, s0, wall, msk)


def _merge_kernel(x_ref, of_ref, og_ref, gmix_ref, wrg_ref, wfo_ref, wgo_ref, wo_ref, gng_ref, gffn_ref,
                  wrt_ref, brt_ref, x2_ref, h2_ref, eid_ref, wsel_ref):
    x = x_ref[...]
    tm, d = x.shape
    h = _rmsnorm(x, gmix_ref[...]).astype(BF16)
    z = _dot(h, wrg_ref[...])
    r = z[:, :GLA_V]
    gate_f = z[:, GLA_V:GLA_V + d]
    gate_g = z[:, GLA_V + d:]
    og = og_ref[...]
    parts = []
    for hh in range(GLA_HEADS):
        blk = og[:, hh * GLA_VAL_DIM:(hh + 1) * GLA_VAL_DIM]
        parts.append(blk * lax.rsqrt(jnp.mean(blk * blk, axis=-1, keepdims=True) + EPS))
    ogn = jnp.concatenate(parts, axis=1)
    ogs = (ogn * gng_ref[...] * (r * _sigmoid(r))).astype(BF16)
    branch_f = _dot(of_ref[...], wfo_ref[...])
    branch_g = _dot(ogs, wgo_ref[...])
    merged = _sigmoid(gate_f) * branch_f + _sigmoid(gate_g) * branch_g
    x2 = x + _dot(merged.astype(BF16), wo_ref[...])
    x2_ref[...] = x2
    h2 = _rmsnorm(x2, gffn_ref[...])
    h2_ref[...] = h2.astype(BF16)

    a = _split3(h2)
    wr = (wrt_ref[0], wrt_ref[1], wrt_ref[2])
    lt = (_dot_nt(wr[1], a[1]) + _dot_nt(wr[0], a[2]) + _dot_nt(wr[2], a[0])
          + _dot_nt(wr[0], a[1]) + _dot_nt(wr[1], a[0]) + _dot_nt(wr[0], a[0])) + brt_ref[...]
    row8 = lax.broadcasted_iota(I32, (SUBLANES, tm), 0)
    gl = jnp.where(row8 < N_GROUPS, lt[:SUBLANES], -jnp.inf)
    gmax = jnp.max(gl, axis=0, keepdims=True)
    g_idx = jnp.min(jnp.where(gl == gmax, row8, SUBLANES), axis=0, keepdims=True)
    g_sel = 1.0 / jnp.sum(jnp.exp(gl - gmax), axis=0, keepdims=True)
    rowe = lax.broadcasted_iota(I32, (N_EXPERTS, tm), 0)
    e1 = jnp.where(rowe // EXPERTS_PER_GROUP == g_idx, lt[SUBLANES:], -jnp.inf)
    v1 = jnp.max(e1, axis=0, keepdims=True)
    i1 = jnp.min(jnp.where(e1 == v1, rowe, N_EXPERTS), axis=0, keepdims=True)
    e2 = jnp.where(rowe == i1, -jnp.inf, e1)
    v2 = jnp.max(e2, axis=0, keepdims=True)
    i2 = jnp.min(jnp.where(e2 == v2, rowe, N_EXPERTS), axis=0, keepdims=True)
    tt = jnp.exp(v2 - v1)
    eid_ref[...] = jnp.concatenate([i1, i2], axis=0)
    wsel_ref[...] = jnp.concatenate([g_sel / (1.0 + tt), g_sel * tt / (1.0 + tt)], axis=0)


def _merge(x, o_fox, o_gla, mw, tm):
    n, d = x.shape
    assert n % tm == 0
    row = lambda wd: pl.BlockSpec((tm, wd), lambda i: (i, 0))
    full = lambda a: pl.BlockSpec(a.shape, lambda i: (0,) * a.ndim, pipeline_mode=pl.Buffered(1))
    ws = (mw["g_mix"], mw["w_rg"], mw["w_fo"], mw["w_go"], mw["w_o"], mw["g_gla"], mw["g_ffn"],
          mw["w_rt"], mw["b_rt"])
    return pl.pallas_call(
        _merge_kernel, grid=(n // tm,),
        in_specs=[row(d), row(FOX_WIDTH), row(GLA_V)] + [full(a) for a in ws],
        out_specs=(row(d), row(d), pl.BlockSpec((2, tm), lambda i: (0, i)), pl.BlockSpec((2, tm), lambda i: (0, i))),
        out_shape=(jax.ShapeDtypeStruct((n, d), F32), jax.ShapeDtypeStruct((n, d), BF16),
                   jax.ShapeDtypeStruct((2, n), I32), jax.ShapeDtypeStruct((2, n), F32)),
        compiler_params=_cparams(("parallel",)), name="merge",
    )(x, o_fox, o_gla, *ws)


def _run_sizes(max_rows):
    sizes = []
    s = RUN_ALIGN
    while s <= max_rows:
        sizes.append(s)
        s *= 2
    return tuple(reversed(sizes))


def _for_each_run_piece(n_rows, a_off, b_off, sizes, fn):
    def pieces(n, a, b, szs):
        for size in szs:
            bit = (n & size) != 0

            @pl.when(bit)
            def _(a=a, b=b, size=size):
                fn(pl.multiple_of(a, RUN_ALIGN), pl.multiple_of(b, RUN_ALIGN), size)

            inc = jnp.where(bit, size, 0)
            a = a + inc
            b = b + inc

    large = [s for s in sizes if s >= RUN_LARGE]
    small = [s for s in sizes if s < RUN_LARGE]
    n_large = n_rows - (n_rows & (RUN_LARGE - 1))
    if large:
        @pl.when(n_large != 0)
        def _():
            pieces(n_large, a_off, b_off, large)
    pieces(n_rows, a_off + n_large, b_off + n_large, small)


def _dispatch_kernel(p_ref, loc_ref, dst_ref, toff_ref, tlen_ref, nu_ref,
                     h2_ref, eid_ref, wsel_ref, locf_ref, triu_ref,
                     xs_ref, pos_ref, xloc_ref, zero_ref, sems, *, nt, bt, sl, d, n_tiles):
    i = pl.program_id(0)
    slot = i % 2
    sizes = _run_sizes(bt)
    t = zero_ref.shape[0]

    def runs(tile, buf, act):
        def body(e, carry):
            idx = tile * N_EXPERTS + e
            _for_each_run_piece(
                p_ref[idx], loc_ref[idx], dst_ref[idx], sizes,
                lambda lo, do, size: act(pltpu.make_async_copy(
                    xloc_ref.at[buf, pl.ds(lo, size)], xs_ref.at[pl.ds(do, size)], sems.at[buf])))
            return carry
        lax.fori_loop(0, N_EXPERTS, body, 0)

    @pl.when(i < nt)
    def _():
        eid = eid_ref[...]
        wsel = wsel_ref[...]
        rowe = lax.broadcasted_iota(I32, (N_EXPERTS, bt), 0)
        oh0 = jnp.where(rowe == eid[0:1], 1.0, 0.0)
        oh1 = jnp.where(rowe == eid[1:2], 1.0, 0.0)
        pref0 = _dot(oh0.astype(BF16), triu_ref[...])
        pref1 = _dot(oh1.astype(BF16), triu_ref[...])
        n0 = jnp.sum(oh0, axis=1, keepdims=True)
        loc = locf_ref[0]
        pos0 = jnp.sum(oh0 * (loc + pref0), axis=0, keepdims=True)
        pos1 = jnp.sum(oh1 * (loc + n0 + pref1), axis=0, keepdims=True)
        pos_ref[...] = jnp.concatenate([pos0, pos1], axis=0)
        srow = lax.broadcasted_iota(I32, (sl, bt), 0).astype(F32)
        m0 = srow == pos0
        m1 = srow == pos1
        perm = jnp.where(m0, 1.0, jnp.where(m1, 1.0, 0.0)).astype(BF16)
        wslot = jnp.sum(jnp.where(m0, wsel[0:1], jnp.where(m1, wsel[1:2], 0.0)), axis=1, keepdims=True)
        xloc_ref[slot, :, :d] = _dot(perm, h2_ref[...])
        xloc_ref[slot, :, d:] = jnp.broadcast_to(wslot, (sl, W_COLS))
        runs(i, slot, lambda cp: cp.start())

    @pl.when(i > 0)
    def _():
        runs(i - 1, 1 - slot, lambda cp: cp.wait())

    @pl.when(i == nt)
    def _():
        zero_ref[...] = jnp.zeros_like(zero_ref)
        tsizes = _run_sizes(t // 2)

        def tail_copy(zo, do, size):
            return pltpu.make_async_copy(zero_ref.at[pl.ds(0, size)], xs_ref.at[pl.ds(do, size)], sems.at[slot])

        def tails(act):
            def body(e, carry):
                _for_each_run_piece(tlen_ref[e], jnp.int32(0), toff_ref[e], tsizes,
                                    lambda zo, do, size: act(tail_copy(zo, do, size)))
                return carry
            lax.fori_loop(0, N_EXPERTS, body, 0)

            def unused(j, carry):
                act(tail_copy(0, pl.multiple_of(j * t, t), t))
                return carry
            lax.fori_loop(nu_ref[0], n_tiles, unused, 0)

        tails(lambda cp: cp.start())
        tails(lambda cp: cp.wait())


def _ffn_kernel(te_ref, nu_ref, xs_ref, wg_ref, wu_ref, wd_ref, y_ref, *, d):
    i = pl.program_id(0)

    @pl.when(i < nu_ref[0])
    def _():
        x = xs_ref[:, :d].astype(BF16)
        g = _dot(x, wg_ref[0])
        u = _dot(x, wu_ref[0])
        a = (g * _sigmoid(g) * u).astype(BF16)
        y = _dot(a, wd_ref[0])
        wrow = xs_ref[:, d:]
        y_ref[...] = y * jnp.concatenate([wrow] * (d // W_COLS), axis=1)

    @pl.when(i >= nu_ref[0])
    def _():
        y_ref[...] = jnp.zeros_like(y_ref)


def _combine_kernel(p_ref, loc_ref, dst_ref, pos_ref, x2_ref, gfin_ref, y_hbm, out_ref, yloc_ref, sems,
                    *, nt, bt, sl):
    i = pl.program_id(0)
    slot = i % 2
    sizes = _run_sizes(bt)

    def runs(tile, buf, act):
        def body(e, carry):
            idx = tile * N_EXPERTS + e
            _for_each_run_piece(
                p_ref[idx], loc_ref[idx], dst_ref[idx], sizes,
                lambda lo, do, size: act(pltpu.make_async_copy(
                    y_hbm.at[pl.ds(do, size)], yloc_ref.at[buf, pl.ds(lo, size)], sems.at[buf])))
            return carry
        lax.fori_loop(0, N_EXPERTS, body, 0)

    @pl.when(i == 0)
    def _():
        yloc_ref[...] = jnp.zeros_like(yloc_ref)
        runs(0, 0, lambda cp: cp.start())

    @pl.when(i + 1 < nt)
    def _():
        runs(i + 1, 1 - slot, lambda cp: cp.start())

    runs(i, slot, lambda cp: cp.wait())
    pos = pos_ref[...]
    pos0 = _col_from_row(pos[0:1], bt)
    pos1 = _col_from_row(pos[1:2], bt)
    scol = lax.broadcasted_iota(I32, (bt, sl), 1).astype(F32)
    perm_t = jnp.where(scol == pos0, 1.0, jnp.where(scol == pos1, 1.0, 0.0)).astype(BF16)
    moe = _dot(perm_t, yloc_ref[slot].astype(BF16))
    out_ref[...] = _rmsnorm(x2_ref[...] + moe, gfin_ref[...])


def _moe_prompt(x2, h2, eid, wsel, wg, wu, wd, g_fin):
    n, d = x2.shape
    bt, t = MOE_TOKENS, MOE_ROWS
    assert n % bt == 0
    nt = n // bt
    sl = -(-(2 * bt + N_EXPERTS * (RUN_ALIGN - 1)) // LANES) * LANES
    onehot = eid.reshape(2, nt, bt)[..., None] == jnp.arange(N_EXPERTS, dtype=I32)
    cnt = jnp.sum(onehot, axis=(0, 2), dtype=I32)
    p = (cnt + RUN_ALIGN - 1) // RUN_ALIGN * RUN_ALIGN
    loc = jnp.cumsum(p, axis=1) - p
    tot = jnp.sum(p, axis=0)
    reg = (tot + t - 1) // t * t
    reg_end = jnp.cumsum(reg)
    reg_start = reg_end - reg
    dst = reg_start[None, :] + jnp.cumsum(p, axis=0) - p
    n_tiles = (2 * n + nt * N_EXPERTS * (RUN_ALIGN - 1) + N_EXPERTS * (t - 1)) // t + 1
    s_max = n_tiles * t
    n_used = (reg_end[-1] // t).astype(I32).reshape(1)
    tile_start = jnp.arange(n_tiles, dtype=I32) * t
    tile_expert = jnp.minimum(jnp.sum(reg_end[None, :] <= tile_start[:, None], axis=1, dtype=I32), N_EXPERTS - 1)
    p_f, loc_f, dst_f = p.reshape(-1), loc.reshape(-1), dst.reshape(-1).astype(I32)
    locf = loc.astype(F32).reshape(nt, N_EXPERTS, 1)
    triu = jnp.asarray(np.triu(np.ones((bt, bt), np.float32), 1), BF16)

    last = nt - 1
    xs, pos = pl.pallas_call(
        functools.partial(_dispatch_kernel, nt=nt, bt=bt, sl=sl, d=d, n_tiles=n_tiles),
        grid_spec=pltpu.PrefetchScalarGridSpec(
            num_scalar_prefetch=6, grid=(nt + 1,),
            in_specs=[pl.BlockSpec((bt, d), lambda i, *_: (jnp.minimum(i, last), 0)),
                      pl.BlockSpec((2, bt), lambda i, *_: (0, jnp.minimum(i, last))),
                      pl.BlockSpec((2, bt), lambda i, *_: (0, jnp.minimum(i, last))),
                      pl.BlockSpec((1, N_EXPERTS, 1), lambda i, *_: (jnp.minimum(i, last), 0, 0)),
                      pl.BlockSpec((bt, bt), lambda i, *_: (0, 0))],
            out_specs=(pl.BlockSpec(memory_space=pl.ANY),
                       pl.BlockSpec((2, bt), lambda i, *_: (0, jnp.minimum(i, last)))),
            scratch_shapes=[pltpu.VMEM((2, sl, d + W_COLS), F32), pltpu.VMEM((t, d + W_COLS), F32),
                            pltpu.SemaphoreType.DMA((2,))]),
        out_shape=(jax.ShapeDtypeStruct((s_max, d + W_COLS), F32), jax.ShapeDtypeStruct((2, n), F32)),
        compiler_params=_cparams(("arbitrary",)), name="moe_dispatch",
    )(p_f, loc_f, dst_f, (reg_start + tot).astype(I32), (reg - tot).astype(I32), n_used, h2, eid, wsel, locf, triu)

    de = wg.shape[2]
    used = lambda i, te, nu: jnp.minimum(i, nu[0] - 1)
    y = pl.pallas_call(
        functools.partial(_ffn_kernel, d=d),
        grid_spec=pltpu.PrefetchScalarGridSpec(
            num_scalar_prefetch=2, grid=(n_tiles,),
            in_specs=[pl.BlockSpec((t, d + W_COLS), lambda i, te, nu: (used(i, te, nu), 0)),
                      pl.BlockSpec((1, d, de), lambda i, te, nu: (te[used(i, te, nu)], 0, 0)),
                      pl.BlockSpec((1, d, de), lambda i, te, nu: (te[used(i, te, nu)], 0, 0)),
                      pl.BlockSpec((1, de, d), lambda i, te, nu: (te[used(i, te, nu)], 0, 0))],
            out_specs=pl.BlockSpec((t, d), lambda i, te, nu: (i, 0))),
        out_shape=jax.ShapeDtypeStruct((s_max, d), F32),
        compiler_params=_cparams(("arbitrary",)), name="moe_ffn",
    )(tile_expert, n_used, xs, wg, wu, wd)

    return pl.pallas_call(
        functools.partial(_combine_kernel, nt=nt, bt=bt, sl=sl),
        grid_spec=pltpu.PrefetchScalarGridSpec(
            num_scalar_prefetch=3, grid=(nt,),
            in_specs=[pl.BlockSpec((2, bt), lambda i, *_: (0, i)),
                      pl.BlockSpec((bt, d), lambda i, *_: (i, 0)),
                      pl.BlockSpec((1, d), lambda i, *_: (0, 0)),
                      pl.BlockSpec(memory_space=pl.ANY)],
            out_specs=pl.BlockSpec((bt, d), lambda i, *_: (i, 0)),
            scratch_shapes=[pltpu.VMEM((2, sl, d), F32), pltpu.SemaphoreType.DMA((2,))]),
        out_shape=jax.ShapeDtypeStruct((n, d), F32),
        compiler_params=_cparams(("arbitrary",)), name="moe_combine",
    )(p_f, loc_f, dst_f, pos, x2, g_fin, y)


def _moe_dense_kernel(x2_ref, h2_ref, eid_ref, wsel_ref, gfin_ref, wg_ref, wu_ref, wd_ref, out_ref, acc_ref):
    e = pl.program_id(0)
    n = x2_ref.shape[0]

    @pl.when(e == 0)
    def _():
        acc_ref[...] = jnp.zeros_like(acc_ref)

    eid = eid_ref[...]
    wsel = wsel_ref[...]
    gate_row = jnp.where(eid[0:1] == e, wsel[0:1], 0.0) + jnp.where(eid[1:2] == e, wsel[1:2], 0.0)
    gate = _col_from_row(gate_row, n)
    x = h2_ref[...]
    g = _dot(x, wg_ref[0])
    u = _dot(x, wu_ref[0])
    a = (g * _sigmoid(g) * u).astype(BF16)
    acc_ref[...] += gate * _dot(a, wd_ref[0])

    @pl.when(e == pl.num_programs(0) - 1)
    def _():
        out_ref[...] = _rmsnorm(x2_ref[...] + acc_ref[...], gfin_ref[...])


def _moe_dense(x2, h2, eid, wsel, wg, wu, wd, g_fin):
    n, d = x2.shape
    de = wg.shape[2]
    full = lambda a: pl.BlockSpec(a.shape, lambda e: (0,) * a.ndim)
    return pl.pallas_call(
        _moe_dense_kernel, grid=(wg.shape[0],),
        in_specs=[full(x2), full(h2), full(eid), full(wsel), full(g_fin),
                  pl.BlockSpec((1, d, de), lambda e: (e, 0, 0)),
                  pl.BlockSpec((1, d, de), lambda e: (e, 0, 0)),
                  pl.BlockSpec((1, de, d), lambda e: (e, 0, 0))],
        out_specs=pl.BlockSpec((n, d), lambda e: (0, 0)),
        out_shape=jax.ShapeDtypeStruct((n, d), F32),
        scratch_shapes=[pltpu.VMEM((n, d), F32)],
        compiler_params=_cparams(("arbitrary",)), name="moe_dense",
    )(x2, h2, eid, wsel, g_fin, wg, wu, wd)


def _layer_weights(norm_mix_g, w_in, fox_f_bias, gla_w_a2, gla_b_a, gla_norm_g, w_fox_out, w_gla_out, w_o,
                   norm_ffn_g, w_group_router, b_group_router, w_expert_router, b_expert_router):
    d = w_in.shape[0]
    c = 0
    parts = {}
    for name, width in (("q", FOX_WIDTH), ("k", FOX_WIDTH), ("v", FOX_WIDTH), ("ff", FOX_HEADS),
                        ("qg", GLA_K), ("kg", GLA_K), ("vg", GLA_V), ("lr", GLA_RANK), ("r", GLA_V),
                        ("gf", d), ("gg", d)):
        parts[name] = w_in[:, c:c + width]
        c += width
    assert c == w_in.shape[1]
    g_mix = norm_mix_g.reshape(1, d)
    w_fft = jnp.zeros((2 * SUBLANES, d), F32).at[:FOX_HEADS].set(parts["ff"].T).astype(BF16)
    pw = dict(
        g_mix=g_mix,
        w_qkv=jnp.concatenate([parts["q"], parts["k"], parts["v"]], axis=1).astype(BF16),
        w_fft=w_fft,
        w_gla=jnp.concatenate([parts["qg"], parts["kg"], parts["vg"]], axis=1).astype(BF16),
        w_lr=parts["lr"].astype(BF16),
        w_a2=gla_w_a2.astype(BF16),
        b_a=gla_b_a.reshape(1, GLA_K),
        f_bias=fox_f_bias.reshape(FOX_HEADS, 1),
        w_vt=parts["v"].T.astype(BF16),
        w_ff=parts["ff"].astype(BF16),
        f_bias_row=fox_f_bias.reshape(1, FOX_HEADS),
    )
    w_rt = jnp.zeros((SUBLANES + N_EXPERTS, d), F32)
    w_rt = w_rt.at[:N_GROUPS].set(w_group_router.T).at[SUBLANES:].set(w_expert_router.T)
    b_rt = jnp.zeros((SUBLANES + N_EXPERTS, 1), F32)
    b_rt = b_rt.at[:N_GROUPS, 0].set(b_group_router).at[SUBLANES:, 0].set(b_expert_router)
    mw = dict(
        g_mix=g_mix,
        w_rg=jnp.concatenate([parts["r"], parts["gf"], parts["gg"]], axis=1).astype(BF16),
        w_fo=w_fox_out.astype(BF16), w_go=w_gla_out.astype(BF16), w_o=w_o.astype(BF16),
        g_gla=gla_norm_g.reshape(1, GLA_V), g_ffn=norm_ffn_g.reshape(1, d),
        w_rt=jnp.stack(_split3(w_rt)), b_rt=b_rt,
    )
    return pw, mw


def kernel(x_prompt, x_sample, cache_k, cache_v, cache_log_f, state_gla, page_table, meta_tokens, norm_mix_g,
           w_in, fox_f_bias, gla_w_a2, gla_b_a, gla_norm_g, w_fox_out, w_gla_out, w_o, norm_ffn_g,
           w_group_router, b_group_router, w_expert_router, b_expert_router, w_expert_gate, w_expert_up,
           w_expert_down, norm_final_g):
    depth = w_in.shape[0]
    assert depth == 1, "meta rows are only carried through one layer"
    b, seq, d = x_prompt.shape
    db, t_dec, _ = x_sample.shape
    n_phys = cache_k.shape[1]
    pw, mw = _layer_weights(norm_mix_g[0], w_in[0], fox_f_bias[0], gla_w_a2[0], gla_b_a[0], gla_norm_g[0],
                            w_fox_out[0], w_gla_out[0], w_o[0], norm_ffn_g[0], w_group_router[0],
                            b_group_router[0], w_expert_router[0], b_expert_router[0])
    wg = w_expert_gate[0].astype(BF16)
    wu = w_expert_up[0].astype(BF16)
    wd = w_expert_down[0].astype(BF16)
    g_fin = norm_final_g.reshape(1, d)

    (_, k_m, v_m, kb_m, _, _, qg_m, kg_m, vg_m, la_m, vt_m, lf_m) = _project(
        meta_tokens.reshape(1, N_META, d), N_META, pw)
    zero_state = jnp.zeros((1, GLA_HEADS // 2, LANES, GLA_VAL_DIM), F32)
    _, s_meta = _gla(qg_m, kg_m, vg_m, la_m, zero_state, N_META)

    tile = min(ATTN_TILE, seq)
    q, k_all, v_all, kb, _, _, qg, kg, vg, la, vt, lf = _project(x_prompt, tile, pw, lead_kv=(k_m[0], v_m[0]))
    o_fox = _fox_prompt(q, kb, vt, lf, kb_m[0], vt_m[0, 0], lf_m[0], tile)
    o_gla, s_gla_p = _gla(qg, kg, vg, la, s_meta, min(GLA_CHUNK, seq))
    n = b * seq
    x2, h2, eid, wsel = _merge(x_prompt.reshape(n, d), o_fox.reshape(n, FOX_WIDTH), o_gla.reshape(n, GLA_V),
                               mw, min(MERGE_TILE, n))
    y_prompt = _moe_prompt(x2, h2, eid, wsel, wg, wu, wd, g_fin).reshape(b, seq, d)

    ns = db * t_dec
    qs, ks, vs, kbs, vbs, lfts, qgs, kgs, vgs, las, _, lfs = _project(x_sample.reshape(1, ns, d), ns, pw)
    tok = lambda a: a.reshape(db, t_dec, a.shape[-1])
    lft_s = lfts.reshape(FOX_HEADS, db, t_dec).transpose(1, 0, 2)
    o_fox_s = _fox_sample(tok(qs), tok(kbs), tok(vbs), lft_s, cache_k[0].transpose(0, 2, 3, 1),
                          cache_v[0].transpose(0, 2, 3, 1), cache_log_f[0].transpose(0, 2, 1), page_table)
    s0 = state_gla[0].reshape(db, GLA_HEADS // 2, LANES, GLA_VAL_DIM)
    o_gla_s, s_gla_s = _gla(tok(qgs), tok(kgs), tok(vgs), tok(las), s0, t_dec)
    x2s, h2s, eids, wsels = _merge(x_sample.reshape(ns, d), o_fox_s.reshape(ns, FOX_WIDTH),
                                   o_gla_s.reshape(ns, GLA_V), mw, ns)
    y_sample = _moe_dense(x2s, h2s, eids, wsels, wg, wu, wd, g_fin).reshape(db, t_dec, d)

    heads = lambda a: a.reshape(a.shape[:-1] + (FOX_HEADS, FOX_HEAD_DIM))
    with_meta = lambda m, r: jnp.concatenate([jnp.broadcast_to(m, (b,) + m.shape[1:]), r], axis=1)
    new_k_prompt = heads(k_all)[None]
    new_v_prompt = heads(v_all)[None]
    new_log_f_prompt = with_meta(lf_m, lf)[None]
    new_gla_prompt = s_gla_p.reshape(1, b, GLA_HEADS, GLA_KEY_DIM, GLA_VAL_DIM)
    new_k_sample = heads(tok(ks))[None]
    new_v_sample = heads(tok(vs))[None]
    new_log_f_sample = lfs.reshape(1, db, t_dec, FOX_HEADS)
    new_gla_sample = s_gla_s.reshape(1, db, GLA_HEADS, GLA_KEY_DIM, GLA_VAL_DIM)
    return (y_prompt, y_sample, new_k_prompt, new_v_prompt, new_log_f_prompt, new_gla_prompt,
            new_k_sample, new_v_sample, new_log_f_sample, new_gla_sample)
```

```python
import functools

import numpy as np
import jax
import jax.numpy as jnp
from jax import lax
from jax.experimental import pallas as pl
from jax.experimental.pallas import tpu as pltpu

F32 = jnp.float32
BF16 = jnp.bfloat16
I32 = jnp.int32
EPS = 1e-6
LOG2E = 1.4426950408889634
N_META = 16
FOX_HEADS = 8
FOX_HEAD_DIM = 64
FOX_WIDTH = FOX_HEADS * FOX_HEAD_DIM
GLA_HEADS = 4
GLA_KEY_DIM = 64
GLA_VAL_DIM = 128
GLA_K = GLA_HEADS * GLA_KEY_DIM
GLA_V = GLA_HEADS * GLA_VAL_DIM
GLA_RANK = 16
GLA_TAU = 16.0
GLA_CHUNK = 128
N_GROUPS = 4
EXPERTS_PER_GROUP = 8
N_EXPERTS = N_GROUPS * EXPERTS_PER_GROUP
PAGE_SIZE = 128

LANES = 128
SUBLANES = 8
VMEM_LIMIT = 56 * 1024 * 1024

PROJ_TILE = 512
ATTN_TILE = 512
MERGE_TILE = 512
MOE_TOKENS = 512
MOE_ROWS = 512
RUN_ALIGN = SUBLANES
RUN_LARGE = 64
PAGES_PER_STEP = 32
GLA_SEQS_PER_STEP = 4
W_COLS = LANES


def _cparams(sem):
    return pltpu.CompilerParams(dimension_semantics=sem, vmem_limit_bytes=VMEM_LIMIT)


def _dot(a, b):
    return jnp.dot(a, b, preferred_element_type=F32)


def _dot_nt(a, b):
    return lax.dot_general(a, b, (((1,), (1,)), ((), ())), preferred_element_type=F32)


def _dot_tn(a, b):
    return lax.dot_general(a, b, (((0,), (0,)), ((), ())), preferred_element_type=F32)


def _split3(a):
    hi = a.astype(BF16)
    r1 = a - hi.astype(F32)
    mid = r1.astype(BF16)
    lo = (r1 - mid.astype(F32)).astype(BF16)
    return hi, mid, lo


def _dot_sel_l(m, a):
    hi, mid, lo = _split3(a)
    return _dot(m, lo) + _dot(m, mid) + _dot(m, hi)


def _dot_sel_r(a, m):
    hi, mid, lo = _split3(a)
    return _dot(lo, m) + _dot(mid, m) + _dot(hi, m)


def _rmsnorm(x, g):
    return x * lax.rsqrt(jnp.mean(x * x, axis=-1, keepdims=True) + EPS) * g


def _log_sigmoid(x):
    return jnp.minimum(x, 0.0) - jnp.log1p(jnp.exp(-jnp.abs(x)))


def _sigmoid(x):
    return 1.0 / (1.0 + jnp.exp(-x))


def _col_from_row(row, n):
    r = lax.broadcasted_iota(I32, (n, n), 0)
    c = lax.broadcasted_iota(I32, (n, n), 1)
    return jnp.sum(jnp.where(r == c, jnp.broadcast_to(row, (n, n)), 0.0), axis=1, keepdims=True)


def _proj_kernel(x_ref, g_ref, wqkv_ref, wfft_ref, wgla_ref, wlr_ref, wa2_ref, ba_ref, fb_ref,
                 wvt_ref, wff_ref, fbr_ref, *rest, lead):
    if lead:
        klead_ref, vlead_ref = rest[:2]
        rest = rest[2:]
    q_ref, k_ref, v_ref, kb_ref, vb_ref, lft_ref, qg_ref, kg_ref, vg_ref, la_ref, vt_ref, lf_ref = rest[:12]
    if not lead:
        def plain_store(k, v):
            k_ref[0] = k
            v_ref[0] = v

        _proj_rows(x_ref, g_ref, wqkv_ref, wfft_ref, wgla_ref, wlr_ref, wa2_ref, ba_ref, fb_ref, wvt_ref, wff_ref,
                   fbr_ref, q_ref, kb_ref, vb_ref, lft_ref, qg_ref, kg_ref, vg_ref, la_ref, vt_ref, lf_ref,
                   plain_store)
        return

    kcar_ref, vcar_ref = rest[12:]
    i = pl.program_id(1)
    tm = x_ref.shape[1]

    @pl.when(i == 0)
    def _():
        kcar_ref[...] = klead_ref[...]
        vcar_ref[...] = vlead_ref[...]

    def shifted_store(k, v):
        k_ref[0, :lead] = kcar_ref[...]
        v_ref[0, :lead] = vcar_ref[...]
        k_ref[0, lead:] = k[:tm - lead]
        v_ref[0, lead:] = v[:tm - lead]
        kcar_ref[...] = k[tm - lead:]
        vcar_ref[...] = v[tm - lead:]

    @pl.when(i < pl.num_programs(1) - 1)
    def _():
        _proj_rows(x_ref, g_ref, wqkv_ref, wfft_ref, wgla_ref, wlr_ref, wa2_ref, ba_ref, fb_ref, wvt_ref, wff_ref,
                   fbr_ref, q_ref, kb_ref, vb_ref, lft_ref, qg_ref, kg_ref, vg_ref, la_ref, vt_ref, lf_ref,
                   shifted_store)

    @pl.when(i == pl.num_programs(1) - 1)
    def _():
        zeros = jnp.zeros((tm - lead, k_ref.shape[2]), F32)
        k_ref[0, :lead] = kcar_ref[...]
        v_ref[0, :lead] = vcar_ref[...]
        k_ref[0, lead:] = zeros
        v_ref[0, lead:] = zeros


def _proj_rows(x_ref, g_ref, wqkv_ref, wfft_ref, wgla_ref, wlr_ref, wa2_ref, ba_ref, fb_ref, wvt_ref, wff_ref,
               fbr_ref, q_ref, kb_ref, vb_ref, lft_ref, qg_ref, kg_ref, vg_ref, la_ref, vt_ref, lf_ref, store_kv):
    h = _rmsnorm(x_ref[0], g_ref[...]).astype(BF16)
    z = _dot(h, wqkv_ref[...])
    q_ref[0] = (z[:, :FOX_WIDTH] * (FOX_HEAD_DIM ** -0.5 * LOG2E)).astype(BF16)
    k = z[:, FOX_WIDTH:2 * FOX_WIDTH]
    v = z[:, 2 * FOX_WIDTH:]
    store_kv(k, v)
    kb_ref[0] = k.astype(BF16)
    vb_ref[0] = v.astype(BF16)
    vt_ref[0, 0] = _dot_nt(wvt_ref[...], h).astype(BF16)
    fft = _dot_nt(wfft_ref[...], h)[:FOX_HEADS]
    lft_ref[0] = _log_sigmoid(fft + fb_ref[...])
    lf_ref[0] = _log_sigmoid(_dot(h, wff_ref[...]) + fbr_ref[...])
    zg = _dot(h, wgla_ref[...])
    qg_ref[0] = zg[:, :GLA_K] * GLA_KEY_DIM ** -0.5
    kg_ref[0] = zg[:, GLA_K:2 * GLA_K]
    vg_ref[0] = zg[:, 2 * GLA_K:]
    lr = _dot(h, wlr_ref[...])
    xa = _dot(lr.astype(BF16), wa2_ref[...]) + ba_ref[...]
    la_ref[0] = _log_sigmoid(xa) / GLA_TAU


def _project(x3, tm, pw, lead_kv=None):
    bx, lx, d = x3.shape
    assert lx % tm == 0
    n = lx // tm
    lead = 0 if lead_kv is None else lead_kv[0].shape[0]
    assert lead % SUBLANES == 0 and lead < tm
    grid = (bx, n + 1 if lead else n)
    last = n - 1
    row = lambda w: pl.BlockSpec((1, tm, w), lambda b, i: (b, jnp.minimum(i, last), 0))
    kv_row = pl.BlockSpec((1, tm, FOX_WIDTH), lambda b, i: (b, i, 0))
    full = lambda a: pl.BlockSpec(a.shape, lambda b, i: (0,) * a.ndim)
    ws = (pw["g_mix"], pw["w_qkv"], pw["w_fft"], pw["w_gla"], pw["w_lr"], pw["w_a2"], pw["b_a"], pw["f_bias"],
          pw["w_vt"], pw["w_ff"], pw["f_bias_row"]) + (tuple(lead_kv) if lead else ())
    out_shape = (
        jax.ShapeDtypeStruct((bx, lx, FOX_WIDTH), BF16),
        jax.ShapeDtypeStruct((bx, lx + lead, FOX_WIDTH), F32),
        jax.ShapeDtypeStruct((bx, lx + lead, FOX_WIDTH), F32),
        jax.ShapeDtypeStruct((bx, lx, FOX_WIDTH), BF16),
        jax.ShapeDtypeStruct((bx, lx, FOX_WIDTH), BF16),
        jax.ShapeDtypeStruct((bx, FOX_HEADS, lx), F32),
        jax.ShapeDtypeStruct((bx, lx, GLA_K), F32),
        jax.ShapeDtypeStruct((bx, lx, GLA_K), F32),
        jax.ShapeDtypeStruct((bx, lx, GLA_V), F32),
        jax.ShapeDtypeStruct((bx, lx, GLA_K), F32),
        jax.ShapeDtypeStruct((bx, lx // tm, FOX_WIDTH, tm), BF16),
        jax.ShapeDtypeStruct((bx, lx, FOX_HEADS), F32),
    )
    out_specs = (row(FOX_WIDTH), kv_row, kv_row, row(FOX_WIDTH), row(FOX_WIDTH),
                 pl.BlockSpec((1, FOX_HEADS, tm), lambda b, i: (b, 0, jnp.minimum(i, last))),
                 row(GLA_K), row(GLA_K), row(GLA_V), row(GLA_K),
                 pl.BlockSpec((1, 1, FOX_WIDTH, tm), lambda b, i: (b, jnp.minimum(i, last), 0, 0)), row(FOX_HEADS))
    return pl.pallas_call(
        functools.partial(_proj_kernel, lead=lead), grid=grid,
        in_specs=[row(d)] + [full(a) for a in ws],
        out_specs=out_specs, out_shape=out_shape,
        scratch_shapes=[pltpu.VMEM((lead, FOX_WIDTH), F32)] * 2 if lead else [],
        compiler_params=_cparams(("parallel", "arbitrary" if lead else "parallel")), name="proj",
    )(x3, *ws)


def _online_update_t(s, s_max, vt, state):
    m, acc = state
    m_new = jnp.maximum(m, s_max)
    p = jnp.exp2(s - m_new)
    return m_new, jnp.exp2(m - m_new) * acc + _dot(vt, p.astype(BF16))


def _with_ones_rows(vt):
    return jnp.concatenate([vt, jnp.ones((2 * SUBLANES, vt.shape[1]), BF16)], axis=0)


def _fox_prompt_kernel(q_ref, kb_ref, vt_ref, lf_ref, km_ref, vtm_ref, lfm_ref, tri_ref, um_ref,
                       o_ref, c_ref, s00_ref, s01_ref, s10_ref, s11_ref, cm00_ref, cm01_ref, cm10_ref, cm11_ref,
                       *, tq):
    qi = pl.program_id(1)
    n_blk = kb_ref.shape[1] // tq
    dh = FOX_HEAD_DIM
    s_refs = ((s00_ref, s01_ref), (s10_ref, s11_ref))
    cm_refs = ((cm00_ref, cm01_ref), (cm10_ref, cm11_ref))

    @pl.when(qi == 0)
    def _():
        carry = jnp.zeros((1, FOX_HEADS), F32)
        for jb in range(n_blk):
            for js in range(tq // LANES):
                off = jb * tq + js * LANES
                cs = _dot_sel_l(tri_ref[...], lf_ref[0, off:off + LANES, :]) + carry
                c_ref[jb, js * LANES:(js + 1) * LANES, :] = cs * LOG2E
                carry = cs[LANES - 1:LANES, :]

    bias_meta = _dot_sel_l(um_ref[...], lfm_ref[...]) * LOG2E
    lane = lax.broadcasted_iota(I32, (tq, LANES), 1)
    key = lax.broadcasted_iota(I32, (tq, tq), 0)
    qry = lax.broadcasted_iota(I32, (tq, tq), 1)
    in_a = lane < dh
    for hp in range(FOX_HEADS // 2):
        sl = slice(hp * LANES, (hp + 1) * LANES)
        ha, hb = 2 * hp, 2 * hp + 1
        qp = q_ref[0, :, sl].astype(F32)
        qa = jnp.where(in_a, qp, 0.0).astype(BF16)
        qb = jnp.where(in_a, 0.0, qp).astype(BF16)
        km = km_ref[:, sl]

        def first(qh, h, km=km):
            s = _dot_nt(km, qh) + bias_meta[:, h:h + 1]
            m = jnp.max(s, axis=0, keepdims=True)
            p = jnp.exp2(s - m)
            return m, _dot(_with_ones_rows(vtm_ref[h * dh:(h + 1) * dh, :]), p.astype(BF16))

        def raw_scores(j, qa=qa, qb=qb, ha=ha, hb=hb, sl=sl):
            start = pl.multiple_of(j * tq, tq)
            kc = kb_ref[0, pl.ds(start, tq), sl]
            cj = c_ref[j]
            return _dot_nt(kc, qa) - cj[:, ha:ha + 1], _dot_nt(kc, qb) - cj[:, hb:hb + 1]

        def values(j, sub, ha=ha, hb=hb):
            h = (ha, hb)[sub]
            return _with_ones_rows(vt_ref[0, j, h * dh:(h + 1) * dh, :])

        def score(j, buf):
            for sub, s in enumerate(raw_scores(j)):
                s_refs[buf][sub][...] = s
                cm_refs[buf][sub][...] = jnp.max(s, axis=0, keepdims=True)

        def advance(j, buf, states):
            return tuple(_online_update_t(s_refs[buf][sub][...], cm_refs[buf][sub][...], values(j, sub), states[sub])
                         for sub in range(2))

        def two_blocks(i, states):
            j = 2 * i
            score(j + 1, 1)
            states = advance(j, 0, states)
            score(jnp.minimum(j + 2, qi - 1), 0)
            return advance(j + 1, 1, states)

        score(0, 0)
        states = []
        for sub, (s, st) in enumerate(zip(raw_scores(qi), (first(qa, ha), first(qb, hb)))):
            s = jnp.where(key <= qry, s, -jnp.inf)
            states.append(_online_update_t(s, jnp.max(s, axis=0, keepdims=True), values(qi, sub), st))
        states = lax.fori_loop(0, qi // 2, two_blocks, tuple(states))
        (_, acc_a), (_, acc_b) = lax.cond(qi % 2 == 1, lambda st: advance(qi - 1, 0, st), lambda st: st, states)
        o_t = jnp.concatenate([acc_a[:dh] / acc_a[dh:dh + 1], acc_b[:dh] / acc_b[dh:dh + 1]], axis=0)
        o_ref[0, :, sl] = o_t.T.astype(BF16)


def _fox_prompt(q, kb, vt, lf, kb_meta, vt_meta, lf_meta, tq):
    b, l, w = q.shape
    assert l % tq == 0 and vt.shape == (b, l // tq, w, tq)
    n_blk = l // tq
    tri = jnp.asarray(np.tril(np.ones((LANES, LANES), np.float32)), BF16)
    um = jnp.asarray(np.triu(np.ones((N_META, N_META), np.float32), 1), BF16)
    full = lambda a: pl.BlockSpec(a.shape, lambda bi, qi: (0,) * a.ndim)
    return pl.pallas_call(
        functools.partial(_fox_prompt_kernel, tq=tq), grid=(b, n_blk),
        in_specs=[pl.BlockSpec((1, tq, w), lambda bi, qi: (bi, qi, 0)),
                  pl.BlockSpec((1, l, w), lambda bi, qi: (bi, 0, 0)),
                  pl.BlockSpec((1, n_blk, w, tq), lambda bi, qi: (bi, 0, 0, 0)),
                  pl.BlockSpec((1, l, FOX_HEADS), lambda bi, qi: (bi, 0, 0)),
                  full(kb_meta), full(vt_meta), full(lf_meta), full(tri), full(um)],
        out_specs=pl.BlockSpec((1, tq, w), lambda bi, qi: (bi, qi, 0)),
        out_shape=jax.ShapeDtypeStruct((b, l, w), BF16),
        scratch_shapes=([pltpu.VMEM((n_blk, tq, FOX_HEADS), F32)] + [pltpu.VMEM((tq, tq), F32)] * 4
                        + [pltpu.VMEM((1, tq), F32)] * 4),
        compiler_params=_cparams(("parallel", "arbitrary")), name="fox_prompt",
    )(q, kb, vt, lf, kb_meta, vt_meta, lf_meta, tri, um)


def _fox_sample_kernel(pt_ref, q_ref, kn_ref, vn_ref, lfn_ref, t8_ref, u_ref, *rest, g, t):
    ck = rest[:g]
    cv = rest[g:2 * g]
    clf = rest[2 * g:3 * g]
    o_ref, qbd_ref, m_ref, l_ref, acc_ref, r_ref = rest[3 * g:]
    gi = pl.program_id(1)
    rows = t * FOX_HEADS
    own = (lax.broadcasted_iota(I32, (rows, FOX_WIDTH), 0) % FOX_HEADS
           == lax.broadcasted_iota(I32, (rows, FOX_WIDTH), 1) // FOX_HEAD_DIM)

    @pl.when(gi == 0)
    def _():
        q = q_ref[0].astype(F32)
        qt = jnp.concatenate([jnp.broadcast_to(q[i:i + 1], (FOX_HEADS, FOX_WIDTH)) for i in range(t)], axis=0)
        qbd = jnp.where(own, qt, 0.0).astype(BF16)
        qbd_ref[...] = qbd
        a_t = _dot_sel_r(lfn_ref[0], t8_ref[...])
        s = _dot_nt(qbd, kn_ref[0]) - jnp.concatenate([a_t * LOG2E] * t, axis=0)
        rt = lax.broadcasted_iota(I32, (rows, t), 0) // FOX_HEADS
        ct = lax.broadcasted_iota(I32, (rows, t), 1)
        s = jnp.where(ct <= rt, s, -jnp.inf)
        m = jnp.max(s, axis=1, keepdims=True)
        p = jnp.exp2(s - m)
        m_ref[...] = m
        l_ref[...] = jnp.sum(p, axis=1, keepdims=True)
        acc_ref[...] = _dot(p.astype(BF16), vn_ref[0])
        r_ref[...] = jnp.zeros_like(r_ref)

    lf_all = jnp.concatenate([clf[jj][0] for jj in range(g)], axis=0)
    after = _dot_sel_r(lf_all, u_ref[...])
    r_run = r_ref[:, 0:1]
    qbd = qbd_ref[...]
    scores = [None] * g
    for jj in reversed(range(g)):
        bias = (after[jj * FOX_HEADS:(jj + 1) * FOX_HEADS] + r_run) * LOG2E
        r_run = r_run + jnp.sum(clf[jj][0], axis=1, keepdims=True)
        kt = ck[jj][0].reshape(FOX_WIDTH, PAGE_SIZE).astype(BF16)
        scores[jj] = _dot(qbd, kt) + jnp.concatenate([bias] * t, axis=0)
    r_ref[...] = jnp.broadcast_to(r_run, r_ref.shape)
    m = m_ref[...]
    m_new = m
    for s in scores:
        m_new = jnp.maximum(m_new, jnp.max(s, axis=1, keepdims=True))
    alpha = jnp.exp2(m - m_new)
    l = alpha * l_ref[...]
    acc = alpha * acc_ref[...]
    for jj, s in enumerate(scores):
        p = jnp.exp2(s - m_new)
        l = l + jnp.sum(p, axis=1, keepdims=True)
        acc = acc + _dot_nt(p.astype(BF16), cv[jj][0].reshape(FOX_WIDTH, PAGE_SIZE).astype(BF16))
    m_ref[...] = m_new
    l_ref[...] = l
    acc_ref[...] = acc

    @pl.when(gi == pl.num_programs(1) - 1)
    def _():
        o = jnp.where(own, acc / l, 0.0)
        o_ref[0] = jnp.concatenate(
            [jnp.sum(o[i * FOX_HEADS:(i + 1) * FOX_HEADS], axis=0, keepdims=True) for i in range(t)],
            axis=0).astype(BF16)


def _fox_sample(q, kb_new, vb_new, lft_new, cache_kt, cache_vt, cache_lft, page_table):
    db, t, w = q.shape
    n_pages = page_table.shape[1]
    g = min(PAGES_PER_STEP, n_pages)
    assert n_pages % g == 0
    ng = n_pages // g
    rows = FOX_HEADS * t
    t8 = jnp.asarray(np.triu(np.ones((t, t), np.float32)), BF16)
    u = jnp.asarray(np.tril(np.ones((PAGE_SIZE, PAGE_SIZE), np.float32), -1), BF16)
    tok = lambda wd: pl.BlockSpec((1, t, wd), lambda b, gi, pt: (b, 0, 0))
    full = lambda a: pl.BlockSpec(a.shape, lambda b, gi, pt: (0,) * a.ndim)

    def page(shape, jj):
        zeros = (0,) * (len(shape) - 1)
        return pl.BlockSpec(shape, lambda b, gi, pt: (pt[b * n_pages + (ng - 1 - gi) * g + jj],) + zeros)

    kv_page = (1, FOX_HEADS, FOX_HEAD_DIM, PAGE_SIZE)
    in_specs = ([tok(w), tok(w), tok(w), pl.BlockSpec((1, FOX_HEADS, t), lambda b, gi, pt: (b, 0, 0)),
                 full(t8), full(u)]
                + [page(kv_page, jj) for jj in range(g)]
                + [page(kv_page, jj) for jj in range(g)]
                + [page((1, FOX_HEADS, PAGE_SIZE), jj) for jj in range(g)])
    grid_spec = pltpu.PrefetchScalarGridSpec(
        num_scalar_prefetch=1, grid=(db, ng), in_specs=in_specs,
        out_specs=pl.BlockSpec((1, t, w), lambda b, gi, pt: (b, 0, 0)),
        scratch_shapes=[pltpu.VMEM((rows, w), BF16), pltpu.VMEM((rows, 1), F32), pltpu.VMEM((rows, 1), F32),
                        pltpu.VMEM((rows, w), F32), pltpu.VMEM((FOX_HEADS, LANES), F32)])
    return pl.pallas_call(
        functools.partial(_fox_sample_kernel, g=g, t=t), grid_spec=grid_spec,
        out_shape=jax.ShapeDtypeStruct((db, t, w), BF16),
        compiler_params=_cparams(("parallel", "arbitrary")), name="fox_sample",
    )(page_table.reshape(-1), q, kb_new, vb_new, lft_new, t8, u,
      *([cache_kt] * g), *([cache_vt] * g), *([cache_lft] * g))


def _gla_tables(c):
    idx = np.arange(c)
    ws = [np.tril(np.ones((c, c), np.float32))]
    ms = [np.eye(c, dtype=np.float32)]
    sz = 2
    while sz <= c:
        seg = idx // sz
        upper = (idx % sz) >= sz // 2
        mid = seg * sz + sz // 2 - 1
        j = idx[None, :]
        r = idx[:, None]
        w_up = (j > mid[:, None]) & (j <= r)
        w_lo = (j > r) & (j <= mid[:, None])
        ws.append(np.where(upper[:, None], w_up, w_lo).astype(np.float32))
        ms.append(((seg[:, None] == seg[None, :]) & upper[:, None] & ~upper[None, :]).astype(np.float32))
        sz *= 2
    return (jnp.asarray(np.concatenate(ws, axis=0), BF16),
            jnp.asarray(np.stack([np.concatenate([m, m], axis=0) for m in ms]), F32))


def _gla_kernel(q_ref, k_ref, v_ref, la_ref, s0_ref, wall_ref, msk_ref, o_ref, sfin_ref, state_ref,
                *, c, n_levels, nb, shared_s0):
    ci = pl.program_id(1)

    @pl.when(ci == 0)
    def _():
        for bi in range(nb):
            state_ref[bi] = s0_ref[0 if shared_s0 else bi]

    lane = lax.broadcasted_iota(I32, (c, LANES), 1)
    r128 = lax.broadcasted_iota(I32, (LANES, LANES), 0)
    half = GLA_KEY_DIM
    in_a = lane < half

    def stack_heads(x):
        return jnp.concatenate([jnp.where(in_a, x, 0.0), jnp.where(in_a, 0.0, x)], axis=0).astype(BF16)

    for bi in range(nb):
        la_hi, la_mid, _ = _split3(la_ref[bi])
        seg = _dot(wall_ref[...], jnp.concatenate([la_hi, la_mid], axis=1))
        seg = seg[:, GLA_K:] + seg[:, :GLA_K]
        b_all = seg[:c]
        e_all = jnp.exp(seg[c:])
        for p in range(GLA_HEADS // 2):
            ksl = slice(p * LANES, (p + 1) * LANES)
            q = q_ref[bi, :, ksl]
            k = k_ref[bi, :, ksl]
            v16 = v_ref[bi, :, 2 * p * LANES:(2 * p + 2) * LANES].astype(BF16)
            sab = msk_ref[0] * _dot_nt(stack_heads(q), k.astype(BF16))
            for lv in range(n_levels):
                e = e_all[lv * c:(lv + 1) * c, ksl]
                sab = sab + msk_ref[lv + 1] * _dot_nt(stack_heads(q * e), (k * e).astype(BF16))
            st = state_ref[bi, p]
            b = b_all[:, ksl]
            inter = _dot(stack_heads(q * jnp.exp(b)), st.astype(BF16))
            s16 = sab.astype(BF16)
            o_ref[bi, :, 2 * p * LANES:(2 * p + 1) * LANES] = inter[:c] + _dot(s16[:c], v16[:, :LANES])
            o_ref[bi, :, (2 * p + 1) * LANES:(2 * p + 2) * LANES] = inter[c:] + _dot(s16[c:], v16[:, LANES:])
            b_end = b[c - 1:c, :]
            kdec = (k * jnp.exp(b_end - b)).astype(BF16)
            upd = _dot_tn(kdec, v16)
            upd = jnp.where(r128 < half, upd[:, :LANES], upd[:, LANES:])
            state_ref[bi, p] = _col_from_row(jnp.exp(b_end), LANES) * st + upd

    @pl.when(ci == pl.num_programs(1) - 1)
    def _():
        sfin_ref[...] = state_ref[...]


def _gla(qg, kg, vg, la, s0, c):
    b, l, _ = qg.shape
    assert l % c == 0
    nb = GLA_SEQS_PER_STEP
    while b % nb:
        nb //= 2
    wall, msk = _gla_tables(c)
    n_levels = msk.shape[0] - 1
    shared_s0 = s0.shape[0] != b
    st_shape = (GLA_HEADS // 2, LANES, GLA_VAL_DIM)
    s0_spec = (pl.BlockSpec((1,) + st_shape, lambda bi, ci: (0, 0, 0, 0)) if shared_s0
               else pl.BlockSpec((nb,) + st_shape, lambda bi, ci: (bi, 0, 0, 0)))
    tok = lambda wd: pl.BlockSpec((nb, c, wd), lambda bi, ci: (bi, ci, 0))
    full = lambda a: pl.BlockSpec(a.shape, lambda bi, ci: (0,) * a.ndim)
    return pl.pallas_call(
        functools.partial(_gla_kernel, c=c, n_levels=n_levels, nb=nb, shared_s0=shared_s0),
        grid=(b // nb, l // c),
        in_specs=[tok(GLA_K), tok(GLA_K), tok(GLA_V), tok(GLA_K), s0_spec, full(wall), full(msk)],
        out_specs=(tok(GLA_V), pl.BlockSpec((nb,) + st_shape, lambda bi, ci: (bi, 0, 0, 0))),
        out_shape=(jax.ShapeDtypeStruct((b, l, GLA_V), F32), jax.ShapeDtypeStruct((b,) + st_shape, F32)),
        scratch_shapes=[pltpu.VMEM((nb,) + st_shape, F32)],
        compiler_params=_cparams(("parallel", "arbitrary")), name="gla",
    )(qg, kg, vg, la, s0, wall, msk)


def _merge_kernel(x_ref, of_ref, og_ref, gmix_ref, wrg_ref, wfo_ref, wgo_ref, wo_ref, gng_ref, gffn_ref,
                  wrt_ref, brt_ref, x2_ref, h2_ref, eid_ref, wsel_ref):
    x = x_ref[...]
    tm, d = x.shape
    h = _rmsnorm(x, gmix_ref[...]).astype(BF16)
    z = _dot(h, wrg_ref[...])
    r = z[:, :GLA_V]
    gate_f = z[:, GLA_V:GLA_V + d]
    gate_g = z[:, GLA_V + d:]
    og = og_ref[...]
    parts = []
    for hh in range(GLA_HEADS):
        blk = og[:, hh * GLA_VAL_DIM:(hh + 1) * GLA_VAL_DIM]
        parts.append(blk * lax.rsqrt(jnp.mean(blk * blk, axis=-1, keepdims=True) + EPS))
    ogn = jnp.concatenate(parts, axis=1)
    ogs = (ogn * gng_ref[...] * (r * _sigmoid(r))).astype(BF16)
    branch_f = _dot(of_ref[...], wfo_ref[...])
    branch_g = _dot(ogs, wgo_ref[...])
    merged = _sigmoid(gate_f) * branch_f + _sigmoid(gate_g) * branch_g
    x2 = x + _dot(merged.astype(BF16), wo_ref[...])
    x2_ref[...] = x2
    h2 = _rmsnorm(x2, gffn_ref[...])
    h2_ref[...] = h2.astype(BF16)

    a = _split3(h2)
    wr = (wrt_ref[0], wrt_ref[1], wrt_ref[2])
    lt = (_dot_nt(wr[1], a[1]) + _dot_nt(wr[0], a[2]) + _dot_nt(wr[2], a[0])
          + _dot_nt(wr[0], a[1]) + _dot_nt(wr[1], a[0]) + _dot_nt(wr[0], a[0])) + brt_ref[...]
    row8 = lax.broadcasted_iota(I32, (SUBLANES, tm), 0)
    gl = jnp.where(row8 < N_GROUPS, lt[:SUBLANES], -jnp.inf)
    gmax = jnp.max(gl, axis=0, keepdims=True)
    g_idx = jnp.min(jnp.where(gl == gmax, row8, SUBLANES), axis=0, keepdims=True)
    g_sel = 1.0 / jnp.sum(jnp.exp(gl - gmax), axis=0, keepdims=True)
    rowe = lax.broadcasted_iota(I32, (N_EXPERTS, tm), 0)
    e1 = jnp.where(rowe // EXPERTS_PER_GROUP == g_idx, lt[SUBLANES:], -jnp.inf)
    v1 = jnp.max(e1, axis=0, keepdims=True)
    i1 = jnp.min(jnp.where(e1 == v1, rowe, N_EXPERTS), axis=0, keepdims=True)
    e2 = jnp.where(rowe == i1, -jnp.inf, e1)
    v2 = jnp.max(e2, axis=0, keepdims=True)
    i2 = jnp.min(jnp.where(e2 == v2, rowe, N_EXPERTS), axis=0, keepdims=True)
    tt = jnp.exp(v2 - v1)
    eid_ref[...] = jnp.concatenate([i1, i2], axis=0)
    wsel_ref[...] = jnp.concatenate([g_sel / (1.0 + tt), g_sel * tt / (1.0 + tt)], axis=0)


def _merge(x, o_fox, o_gla, mw, tm):
    n, d = x.shape
    assert n % tm == 0
    row = lambda wd: pl.BlockSpec((tm, wd), lambda i: (i, 0))
    full = lambda a: pl.BlockSpec(a.shape, lambda i: (0,) * a.ndim, pipeline_mode=pl.Buffered(1))
    ws = (mw["g_mix"], mw["w_rg"], mw["w_fo"], mw["w_go"], mw["w_o"], mw["g_gla"], mw["g_ffn"],
          mw["w_rt"], mw["b_rt"])
    return pl.pallas_call(
        _merge_kernel, grid=(n // tm,),
        in_specs=[row(d), row(FOX_WIDTH), row(GLA_V)] + [full(a) for a in ws],
        out_specs=(row(d), row(d), pl.BlockSpec((2, tm), lambda i: (0, i)), pl.BlockSpec((2, tm), lambda i: (0, i))),
        out_shape=(jax.ShapeDtypeStruct((n, d), F32), jax.ShapeDtypeStruct((n, d), BF16),
                   jax.ShapeDtypeStruct((2, n), I32), jax.ShapeDtypeStruct((2, n), F32)),
        compiler_params=_cparams(("parallel",)), name="merge",
    )(x, o_fox, o_gla, *ws)


def _run_sizes(max_rows):
    sizes = []
    s = RUN_ALIGN
    while s <= max_rows:
        sizes.append(s)
        s *= 2
    return tuple(reversed(sizes))


def _for_each_run_piece(n_rows, a_off, b_off, sizes, fn):
    def pieces(n, a, b, szs):
        for size in szs:
            bit = (n & size) != 0

            @pl.when(bit)
            def _(a=a, b=b, size=size):
                fn(pl.multiple_of(a, RUN_ALIGN), pl.multiple_of(b, RUN_ALIGN), size)

            inc = jnp.where(bit, size, 0)
            a = a + inc
            b = b + inc

    large = [s for s in sizes if s >= RUN_LARGE]
    small = [s for s in sizes if s < RUN_LARGE]
    n_large = n_rows - (n_rows & (RUN_LARGE - 1))
    if large:
        @pl.when(n_large != 0)
        def _():
            pieces(n_large, a_off, b_off, large)
    pieces(n_rows, a_off + n_large, b_off + n_large, small)


def _for_each_listed_piece(tile, cnt_ref, loc_ref, dst_ref, sizes, fn):
    for cls, size in enumerate(sizes):
        first = (tile * len(sizes) + cls) * N_EXPERTS

        def body(k, carry, first=first, size=size):
            fn(pl.multiple_of(loc_ref[first + k], RUN_ALIGN), pl.multiple_of(dst_ref[first + k], RUN_ALIGN), size)
            return carry

        lax.fori_loop(0, cnt_ref[tile * len(sizes) + cls], body, 0)


def _dispatch_kernel(cnt_ref, loc_ref, dst_ref, toff_ref, tlen_ref, nu_ref,
                     h2_ref, eid_ref, wsel_ref, locf_ref, triu_ref,
                     xs_ref, pos_ref, xloc_ref, zero_ref, sems, *, nt, bt, sl, d, n_tiles):
    i = pl.program_id(0)
    slot = i % 2
    sizes = _run_sizes(bt)
    t = zero_ref.shape[0]

    def runs(tile, buf, act):
        _for_each_listed_piece(
            tile, cnt_ref, loc_ref, dst_ref, sizes,
            lambda lo, do, size: act(pltpu.make_async_copy(
                xloc_ref.at[buf, pl.ds(lo, size)], xs_ref.at[pl.ds(do, size)], sems.at[buf])))

    @pl.when(i < nt)
    def _():
        eid = eid_ref[...]
        wsel = wsel_ref[...]
        rowe = lax.broadcasted_iota(I32, (N_EXPERTS, bt), 0)
        oh0 = jnp.where(rowe == eid[0:1], 1.0, 0.0)
        oh1 = jnp.where(rowe == eid[1:2], 1.0, 0.0)
        pref0 = _dot(oh0.astype(BF16), triu_ref[...])
        pref1 = _dot(oh1.astype(BF16), triu_ref[...])
        n0 = jnp.sum(oh0, axis=1, keepdims=True)
        loc = locf_ref[0]
        pos0 = jnp.sum(oh0 * (loc + pref0), axis=0, keepdims=True)
        pos1 = jnp.sum(oh1 * (loc + n0 + pref1), axis=0, keepdims=True)
        pos_ref[...] = jnp.concatenate([pos0, pos1], axis=0)
        srow = lax.broadcasted_iota(I32, (sl, bt), 0).astype(F32)
        m0 = srow == pos0
        m1 = srow == pos1
        perm = jnp.where(m0, 1.0, jnp.where(m1, 1.0, 0.0)).astype(BF16)
        wslot = jnp.sum(jnp.where(m0, wsel[0:1], jnp.where(m1, wsel[1:2], 0.0)), axis=1, keepdims=True)
        xloc_ref[slot, :, :d] = _dot(perm, h2_ref[...])
        xloc_ref[slot, :, d:] = jnp.broadcast_to(wslot, (sl, W_COLS))
        runs(i, slot, lambda cp: cp.start())

    @pl.when(i > 0)
    def _():
        runs(i - 1, 1 - slot, lambda cp: cp.wait())

    @pl.when(i == nt)
    def _():
        zero_ref[...] = jnp.zeros_like(zero_ref)
        tsizes = _run_sizes(t // 2)

        def tail_copy(zo, do, size):
            return pltpu.make_async_copy(zero_ref.at[pl.ds(0, size)], xs_ref.at[pl.ds(do, size)], sems.at[slot])

        def tails(act):
            def body(e, carry):
                _for_each_run_piece(tlen_ref[e], jnp.int32(0), toff_ref[e], tsizes,
                                    lambda zo, do, size: act(tail_copy(zo, do, size)))
                return carry
            lax.fori_loop(0, N_EXPERTS, body, 0)

            def unused(j, carry):
                act(tail_copy(0, pl.multiple_of(j * t, t), t))
                return carry
            lax.fori_loop(nu_ref[0], n_tiles, unused, 0)

        tails(lambda cp: cp.start())
        tails(lambda cp: cp.wait())


def _ffn_kernel(te_ref, nu_ref, xs_ref, wg_ref, wu_ref, wd_ref, y_ref, *, d):
    i = pl.program_id(0)

    @pl.when(i < nu_ref[0])
    def _():
        x = xs_ref[:, :d].astype(BF16)
        g = _dot(x, wg_ref[0])
        u = _dot(x, wu_ref[0])
        a = (g * _sigmoid(g) * u).astype(BF16)
        y = _dot(a, wd_ref[0])
        wrow = xs_ref[:, d:]
        y_ref[...] = y * jnp.concatenate([wrow] * (d // W_COLS), axis=1)

    @pl.when(i >= nu_ref[0])
    def _():
        y_ref[...] = jnp.zeros_like(y_ref)


def _combine_kernel(cnt_ref, loc_ref, dst_ref, pos_ref, x2_ref, gfin_ref, y_hbm, out_ref, yloc_ref, sems,
                    *, nt, bt, sl):
    i = pl.program_id(0)
    slot = i % 2
    sizes = _run_sizes(bt)

    def runs(tile, buf, act):
        _for_each_listed_piece(
            tile, cnt_ref, loc_ref, dst_ref, sizes,
            lambda lo, do, size: act(pltpu.make_async_copy(
                y_hbm.at[pl.ds(do, size)], yloc_ref.at[buf, pl.ds(lo, size)], sems.at[buf])))

    @pl.when(i == 0)
    def _():
        yloc_ref[...] = jnp.zeros_like(yloc_ref)
        runs(0, 0, lambda cp: cp.start())

    @pl.when(i + 1 < nt)
    def _():
        runs(i + 1, 1 - slot, lambda cp: cp.start())

    runs(i, slot, lambda cp: cp.wait())
    pos = pos_ref[...]
    pos0 = _col_from_row(pos[0:1], bt)
    pos1 = _col_from_row(pos[1:2], bt)
    scol = lax.broadcasted_iota(I32, (bt, sl), 1).astype(F32)
    perm_t = jnp.where(scol == pos0, 1.0, jnp.where(scol == pos1, 1.0, 0.0)).astype(BF16)
    moe = _dot(perm_t, yloc_ref[slot].astype(BF16))
    out_ref[...] = _rmsnorm(x2_ref[...] + moe, gfin_ref[...])


def _moe_prompt(x2, h2, eid, wsel, wg, wu, wd, g_fin):
    n, d = x2.shape
    bt, t = MOE_TOKENS, MOE_ROWS
    assert n % bt == 0
    nt = n // bt
    sl = -(-(2 * bt + N_EXPERTS * (RUN_ALIGN - 1)) // LANES) * LANES
    onehot = eid.reshape(2, nt, bt)[..., None] == jnp.arange(N_EXPERTS, dtype=I32)
    cnt = jnp.sum(onehot, axis=(0, 2), dtype=I32)
    p = (cnt + RUN_ALIGN - 1) // RUN_ALIGN * RUN_ALIGN
    loc = jnp.cumsum(p, axis=1) - p
    tot = jnp.sum(p, axis=0)
    reg = (tot + t - 1) // t * t
    reg_end = jnp.cumsum(reg)
    reg_start = reg_end - reg
    dst = reg_start[None, :] + jnp.cumsum(p, axis=0) - p
    n_tiles = (2 * n + nt * N_EXPERTS * (RUN_ALIGN - 1) + N_EXPERTS * (t - 1)) // t + 1
    s_max = n_tiles * t
    n_used = (reg_end[-1] // t).astype(I32).reshape(1)
    tile_start = jnp.arange(n_tiles, dtype=I32) * t
    tile_expert = jnp.minimum(jnp.sum(reg_end[None, :] <= tile_start[:, None], axis=1, dtype=I32), N_EXPERTS - 1)
    sz = jnp.asarray(_run_sizes(bt), I32)[None, :, None]
    p3 = p[:, None, :]
    has = (p3 & sz) != 0
    covered = p3 & ~(2 * sz - 1)
    rank = jnp.cumsum(has, axis=-1, dtype=I32) - 1
    slot_of = has[..., None] & (rank[..., None] == jnp.arange(N_EXPERTS, dtype=I32))
    compact = lambda a: jnp.sum(jnp.where(slot_of, a[..., None], 0), axis=2, dtype=I32).reshape(-1)
    p_f = jnp.sum(has, axis=-1, dtype=I32).reshape(-1)
    loc_f = compact(loc[:, None, :] + covered)
    dst_f = compact(dst[:, None, :].astype(I32) + covered)
    locf = loc.astype(F32).reshape(nt, N_EXPERTS, 1)
    triu = jnp.asarray(np.triu(np.ones((bt, bt), np.float32), 1), BF16)

    last = nt - 1
    xs, pos = pl.pallas_call(
        functools.partial(_dispatch_kernel, nt=nt, bt=bt, sl=sl, d=d, n_tiles=n_tiles),
        grid_spec=pltpu.PrefetchScalarGridSpec(
            num_scalar_prefetch=6, grid=(nt + 1,),
            in_specs=[pl.BlockSpec((bt, d), lambda i, *_: (jnp.minimum(i, last), 0)),
                      pl.BlockSpec((2, bt), lambda i, *_: (0, jnp.minimum(i, last))),
                      pl.BlockSpec((2, bt), lambda i, *_: (0, jnp.minimum(i, last))),
                      pl.BlockSpec((1, N_EXPERTS, 1), lambda i, *_: (jnp.minimum(i, last), 0, 0)),
                      pl.BlockSpec((bt, bt), lambda i, *_: (0, 0))],
            out_specs=(pl.BlockSpec(memory_space=pl.ANY),
                       pl.BlockSpec((2, bt), lambda i, *_: (0, jnp.minimum(i, last)))),
            scratch_shapes=[pltpu.VMEM((2, sl, d + W_COLS), F32), pltpu.VMEM((t, d + W_COLS), F32),
                            pltpu.SemaphoreType.DMA((2,))]),
        out_shape=(jax.ShapeDtypeStruct((s_max, d + W_COLS), F32), jax.ShapeDtypeStruct((2, n), F32)),
        compiler_params=_cparams(("arbitrary",)), name="moe_dispatch",
    )(p_f, loc_f, dst_f, (reg_start + tot).astype(I32), (reg - tot).astype(I32), n_used, h2, eid, wsel, locf, triu)

    de = wg.shape[2]
    used = lambda i, te, nu: jnp.minimum(i, nu[0] - 1)
    y = pl.pallas_call(
        functools.partial(_ffn_kernel, d=d),
        grid_spec=pltpu.PrefetchScalarGridSpec(
            num_scalar_prefetch=2, grid=(n_tiles,),
            in_specs=[pl.BlockSpec((t, d + W_COLS), lambda i, te, nu: (used(i, te, nu), 0)),
                      pl.BlockSpec((1, d, de), lambda i, te, nu: (te[used(i, te, nu)], 0, 0)),
                      pl.BlockSpec((1, d, de), lambda i, te, nu: (te[used(i, te, nu)], 0, 0)),
                      pl.BlockSpec((1, de, d), lambda i, te, nu: (te[used(i, te, nu)], 0, 0))],
            out_specs=pl.BlockSpec((t, d), lambda i, te, nu: (i, 0))),
        out_shape=jax.ShapeDtypeStruct((s_max, d), F32),
        compiler_params=_cparams(("arbitrary",)), name="moe_ffn",
    )(tile_expert, n_used, xs, wg, wu, wd)

    return pl.pallas_call(
        functools.partial(_combine_kernel, nt=nt, bt=bt, sl=sl),
        grid_spec=pltpu.PrefetchScalarGridSpec(
            num_scalar_prefetch=3, grid=(nt,),
            in_specs=[pl.BlockSpec((2, bt), lambda i, *_: (0, i)),
                      pl.BlockSpec((bt, d), lambda i, *_: (i, 0)),
                      pl.BlockSpec((1, d), lambda i, *_: (0, 0)),
                      pl.BlockSpec(memory_space=pl.ANY)],
            out_specs=pl.BlockSpec((bt, d), lambda i, *_: (i, 0)),
            scratch_shapes=[pltpu.VMEM((2, sl, d), F32), pltpu.SemaphoreType.DMA((2,))]),
        out_shape=jax.ShapeDtypeStruct((n, d), F32),
        compiler_params=_cparams(("arbitrary",)), name="moe_combine",
    )(p_f, loc_f, dst_f, pos, x2, g_fin, y)


def _moe_dense_kernel(x2_ref, h2_ref, eid_ref, wsel_ref, gfin_ref, wg_ref, wu_ref, wd_ref, out_ref, acc_ref):
    e = pl.program_id(0)
    n = x2_ref.shape[0]

    @pl.when(e == 0)
    def _():
        acc_ref[...] = jnp.zeros_like(acc_ref)

    eid = eid_ref[...]
    wsel = wsel_ref[...]
    gate_row = jnp.where(eid[0:1] == e, wsel[0:1], 0.0) + jnp.where(eid[1:2] == e, wsel[1:2], 0.0)
    gate = _col_from_row(gate_row, n)
    x = h2_ref[...]
    g = _dot(x, wg_ref[0])
    u = _dot(x, wu_ref[0])
    a = (g * _sigmoid(g) * u).astype(BF16)
    acc_ref[...] += gate * _dot(a, wd_ref[0])

    @pl.when(e == pl.num_programs(0) - 1)
    def _():
        out_ref[...] = _rmsnorm(x2_ref[...] + acc_ref[...], gfin_ref[...])


def _moe_dense(x2, h2, eid, wsel, wg, wu, wd, g_fin):
    n, d = x2.shape
    de = wg.shape[2]
    full = lambda a: pl.BlockSpec(a.shape, lambda e: (0,) * a.ndim)
    return pl.pallas_call(
        _moe_dense_kernel, grid=(wg.shape[0],),
        in_specs=[full(x2), full(h2), full(eid), full(wsel), full(g_fin),
                  pl.BlockSpec((1, d, de), lambda e: (e, 0, 0)),
                  pl.BlockSpec((1, d, de), lambda e: (e, 0, 0)),
                  pl.BlockSpec((1, de, d), lambda e: (e, 0, 0))],
        out_specs=pl.BlockSpec((n, d), lambda e: (0, 0)),
        out_shape=jax.ShapeDtypeStruct((n, d), F32),
        scratch_shapes=[pltpu.VMEM((n, d), F32)],
        compiler_params=_cparams(("arbitrary",)), name="moe_dense",
    )(x2, h2, eid, wsel, g_fin, wg, wu, wd)


def _layer_weights(norm_mix_g, w_in, fox_f_bias, gla_w_a2, gla_b_a, gla_norm_g, w_fox_out, w_gla_out, w_o,
                   norm_ffn_g, w_group_router, b_group_router, w_expert_router, b_expert_router):
    d = w_in.shape[0]
    c = 0
    parts = {}
    for name, width in (("q", FOX_WIDTH), ("k", FOX_WIDTH), ("v", FOX_WIDTH), ("ff", FOX_HEADS),
                        ("qg", GLA_K), ("kg", GLA_K), ("vg", GLA_V), ("lr", GLA_RANK), ("r", GLA_V),
                        ("gf", d), ("gg", d)):
        parts[name] = w_in[:, c:c + width]
        c += width
    assert c == w_in.shape[1]
    g_mix = norm_mix_g.reshape(1, d)
    w_fft = jnp.zeros((2 * SUBLANES, d), F32).at[:FOX_HEADS].set(parts["ff"].T).astype(BF16)
    pw = dict(
        g_mix=g_mix,
        w_qkv=jnp.concatenate([parts["q"], parts["k"], parts["v"]], axis=1).astype(BF16),
        w_fft=w_fft,
        w_gla=jnp.concatenate([parts["qg"], parts["kg"], parts["vg"]], axis=1).astype(BF16),
        w_lr=parts["lr"].astype(BF16),
        w_a2=gla_w_a2.astype(BF16),
        b_a=gla_b_a.reshape(1, GLA_K),
        f_bias=fox_f_bias.reshape(FOX_HEADS, 1),
        w_vt=parts["v"].T.astype(BF16),
        w_ff=parts["ff"].astype(BF16),
        f_bias_row=fox_f_bias.reshape(1, FOX_HEADS),
    )
    w_rt = jnp.zeros((SUBLANES + N_EXPERTS, d), F32)
    w_rt = w_rt.at[:N_GROUPS].set(w_group_router.T).at[SUBLANES:].set(w_expert_router.T)
    b_rt = jnp.zeros((SUBLANES + N_EXPERTS, 1), F32)
    b_rt = b_rt.at[:N_GROUPS, 0].set(b_group_router).at[SUBLANES:, 0].set(b_expert_router)
    mw = dict(
        g_mix=g_mix,
        w_rg=jnp.concatenate([parts["r"], parts["gf"], parts["gg"]], axis=1).astype(BF16),
        w_fo=w_fox_out.astype(BF16), w_go=w_gla_out.astype(BF16), w_o=w_o.astype(BF16),
        g_gla=gla_norm_g.reshape(1, GLA_V), g_ffn=norm_ffn_g.reshape(1, d),
        w_rt=jnp.stack(_split3(w_rt)), b_rt=b_rt,
    )
    return pw, mw


def kernel(x_prompt, x_sample, cache_k, cache_v, cache_log_f, state_gla, page_table, meta_tokens, norm_mix_g,
           w_in, fox_f_bias, gla_w_a2, gla_b_a, gla_norm_g, w_fox_out, w_gla_out, w_o, norm_ffn_g,
           w_group_router, b_group_router, w_expert_router, b_expert_router, w_expert_gate, w_expert_up,
           w_expert_down, norm_final_g):
    depth = w_in.shape[0]
    assert depth == 1, "meta rows are only carried through one layer"
    b, seq, d = x_prompt.shape
    db, t_dec, _ = x_sample.shape
    n_phys = cache_k.shape[1]
    pw, mw = _layer_weights(norm_mix_g[0], w_in[0], fox_f_bias[0], gla_w_a2[0], gla_b_a[0], gla_norm_g[0],
                            w_fox_out[0], w_gla_out[0], w_o[0], norm_ffn_g[0], w_group_router[0],
                            b_group_router[0], w_expert_router[0], b_expert_router[0])
    wg = w_expert_gate[0].astype(BF16)
    wu = w_expert_up[0].astype(BF16)
    wd = w_expert_down[0].astype(BF16)
    g_fin = norm_final_g.reshape(1, d)

    (_, k_m, v_m, kb_m, _, _, qg_m, kg_m, vg_m, la_m, vt_m, lf_m) = _project(
        meta_tokens.reshape(1, N_META, d), N_META, pw)
    zero_state = jnp.zeros((1, GLA_HEADS // 2, LANES, GLA_VAL_DIM), F32)
    _, s_meta = _gla(qg_m, kg_m, vg_m, la_m, zero_state, N_META)

    tile = min(ATTN_TILE, seq)
    q, k_all, v_all, kb, _, _, qg, kg, vg, la, vt, lf = _project(x_prompt, tile, pw, lead_kv=(k_m[0], v_m[0]))
    o_fox = _fox_prompt(q, kb, vt, lf, kb_m[0], vt_m[0, 0], lf_m[0], tile)
    o_gla, s_gla_p = _gla(qg, kg, vg, la, s_meta, min(GLA_CHUNK, seq))
    n = b * seq
    x2, h2, eid, wsel = _merge(x_prompt.reshape(n, d), o_fox.reshape(n, FOX_WIDTH), o_gla.reshape(n, GLA_V),
                               mw, min(MERGE_TILE, n))
    y_prompt = _moe_prompt(x2, h2, eid, wsel, wg, wu, wd, g_fin).reshape(b, seq, d)

    ns = db * t_dec
    qs, ks, vs, kbs, vbs, lfts, qgs, kgs, vgs, las, _, lfs = _project(x_sample.reshape(1, ns, d), ns, pw)
    tok = lambda a: a.reshape(db, t_dec, a.shape[-1])
    lft_s = lfts.reshape(FOX_HEADS, db, t_dec).transpose(1, 0, 2)
    o_fox_s = _fox_sample(tok(qs), tok(kbs), tok(vbs), lft_s, cache_k[0].transpose(0, 2, 3, 1),
                          cache_v[0].transpose(0, 2, 3, 1), cache_log_f[0].transpose(0, 2, 1), page_table)
    s0 = state_gla[0].reshape(db, GLA_HEADS // 2, LANES, GLA_VAL_DIM)
    o_gla_s, s_gla_s = _gla(tok(qgs), tok(kgs), tok(vgs), tok(las), s0, t_dec)
    x2s, h2s, eids, wsels = _merge(x_sample.reshape(ns, d), o_fox_s.reshape(ns, FOX_WIDTH),
                                   o_gla_s.reshape(ns, GLA_V), mw, ns)
    y_sample = _moe_dense(x2s, h2s, eids, wsels, wg, wu, wd, g_fin).reshape(db, t_dec, d)

    heads = lambda a: a.reshape(a.shape[:-1] + (FOX_HEADS, FOX_HEAD_DIM))
    with_meta = lambda m, r: jnp.concatenate([jnp.broadcast_to(m, (b,) + m.shape[1:]), r], axis=1)
    new_k_prompt = heads(k_all)[None]
    new_v_prompt = heads(v_all)[None]
    new_log_f_prompt = with_meta(lf_m, lf)[None]
    new_gla_prompt = s_gla_p.reshape(1, b, GLA_HEADS, GLA_KEY_DIM, GLA_VAL_DIM)
    new_k_sample = heads(tok(ks))[None]
    new_v_sample = heads(tok(vs))[None]
    new_log_f_sample = lfs.reshape(1, db, t_dec, FOX_HEADS)
    new_gla_sample = s_gla_s.reshape(1, db, GLA_HEADS, GLA_KEY_DIM, GLA_VAL_DIM)
    return (y_prompt, y_sample, new_k_prompt, new_v_prompt, new_log_f_prompt, new_gla_prompt,
            new_k_sample, new_v_sample, new_log_f_sample, new_gla_sample)
```

```python
import functools

import numpy as np
import jax
import jax.numpy as jnp
from jax import lax
from jax.experimental import pallas as pl
from jax.experimental.pallas import tpu as pltpu

F32 = jnp.float32
BF16 = jnp.bfloat16
I32 = jnp.int32
EPS = 1e-6
LOG2E = 1.4426950408889634
N_META = 16
FOX_HEADS = 8
FOX_HEAD_DIM = 64
FOX_WIDTH = FOX_HEADS * FOX_HEAD_DIM
GLA_HEADS = 4
GLA_KEY_DIM = 64
GLA_VAL_DIM = 128
GLA_K = GLA_HEADS * GLA_KEY_DIM
GLA_V = GLA_HEADS * GLA_VAL_DIM
GLA_RANK = 16
GLA_TAU = 16.0
GLA_CHUNK = 128
N_GROUPS = 4
EXPERTS_PER_GROUP = 8
N_EXPERTS = N_GROUPS * EXPERTS_PER_GROUP
PAGE_SIZE = 128

LANES = 128
SUBLANES = 8
VMEM_LIMIT = 56 * 1024 * 1024

ATTN_TILE = 512
MERGE_TILE = 512
MOE_TOKENS = 512
MOE_ROWS = 512
RUN_ALIGN = SUBLANES
RUN_LARGE = 64
PAGES_PER_STEP = 32
GLA_SEQS_PER_STEP = 4
W_COLS = LANES


def _cparams(sem):
    return pltpu.CompilerParams(dimension_semantics=sem, vmem_limit_bytes=VMEM_LIMIT)


def _dot(a, b):
    return jnp.dot(a, b, preferred_element_type=F32)


def _dot_nt(a, b):
    return lax.dot_general(a, b, (((1,), (1,)), ((), ())), preferred_element_type=F32)


def _dot_tn(a, b):
    return lax.dot_general(a, b, (((0,), (0,)), ((), ())), preferred_element_type=F32)


def _split3(a):
    hi = a.astype(BF16)
    r1 = a - hi.astype(F32)
    mid = r1.astype(BF16)
    lo = (r1 - mid.astype(F32)).astype(BF16)
    return hi, mid, lo


def _dot_sel_l(m, a):
    hi, mid, lo = _split3(a)
    return _dot(m, lo) + _dot(m, mid) + _dot(m, hi)


def _dot_sel_r(a, m):
    hi, mid, lo = _split3(a)
    return _dot(lo, m) + _dot(mid, m) + _dot(hi, m)


def _rmsnorm(x, g):
    return x * lax.rsqrt(jnp.mean(x * x, axis=-1, keepdims=True) + EPS) * g


def _log_sigmoid(x):
    return jnp.minimum(x, 0.0) - jnp.log1p(jnp.exp(-jnp.abs(x)))


def _sigmoid(x):
    return 1.0 / (1.0 + jnp.exp(-x))


def _col_from_row(row, n):
    r = lax.broadcasted_iota(I32, (n, n), 0)
    c = lax.broadcasted_iota(I32, (n, n), 1)
    return jnp.sum(jnp.where(r == c, jnp.broadcast_to(row, (n, n)), 0.0), axis=1, keepdims=True)


def _proj_kernel(x_ref, g_ref, wqkv_ref, wfft_ref, wgla_ref, wlr_ref, wa2_ref, ba_ref, fb_ref,
                 wff_ref, fbr_ref, *rest, lead):
    if lead:
        klead_ref, vlead_ref = rest[:2]
        rest = rest[2:]
    q_ref, k_ref, v_ref, kb_ref, vb_ref, lft_ref, qg_ref, kg_ref, vg_ref, la_ref, vt_ref, lf_ref = rest[:12]
    if not lead:
        def plain_store(k, v):
            k_ref[0] = k
            v_ref[0] = v

        _proj_rows(x_ref, g_ref, wqkv_ref, wfft_ref, wgla_ref, wlr_ref, wa2_ref, ba_ref, fb_ref, wff_ref,
                   fbr_ref, q_ref, kb_ref, vb_ref, lft_ref, qg_ref, kg_ref, vg_ref, la_ref, vt_ref, lf_ref,
                   plain_store)
        return

    kcar_ref, vcar_ref = rest[12:]
    i = pl.program_id(1)
    tm = x_ref.shape[1]

    @pl.when(i == 0)
    def _():
        kcar_ref[...] = klead_ref[...]
        vcar_ref[...] = vlead_ref[...]

    def shifted_store(k, v):
        k_ref[0, :lead] = kcar_ref[...]
        v_ref[0, :lead] = vcar_ref[...]
        k_ref[0, lead:] = k[:tm - lead]
        v_ref[0, lead:] = v[:tm - lead]
        kcar_ref[...] = k[tm - lead:]
        vcar_ref[...] = v[tm - lead:]

    @pl.when(i < pl.num_programs(1) - 1)
    def _():
        _proj_rows(x_ref, g_ref, wqkv_ref, wfft_ref, wgla_ref, wlr_ref, wa2_ref, ba_ref, fb_ref, wff_ref,
                   fbr_ref, q_ref, kb_ref, vb_ref, lft_ref, qg_ref, kg_ref, vg_ref, la_ref, vt_ref, lf_ref,
                   shifted_store)

    @pl.when(i == pl.num_programs(1) - 1)
    def _():
        zeros = jnp.zeros((tm - lead, k_ref.shape[2]), F32)
        k_ref[0, :lead] = kcar_ref[...]
        v_ref[0, :lead] = vcar_ref[...]
        k_ref[0, lead:] = zeros
        v_ref[0, lead:] = zeros


def _proj_rows(x_ref, g_ref, wqkv_ref, wfft_ref, wgla_ref, wlr_ref, wa2_ref, ba_ref, fb_ref, wff_ref,
               fbr_ref, q_ref, kb_ref, vb_ref, lft_ref, qg_ref, kg_ref, vg_ref, la_ref, vt_ref, lf_ref, store_kv):
    h = _rmsnorm(x_ref[0], g_ref[...]).astype(BF16)
    z = _dot(h, wqkv_ref[...])
    q_ref[0] = (z[:, :FOX_WIDTH] * (FOX_HEAD_DIM ** -0.5 * LOG2E)).astype(BF16)
    k = z[:, FOX_WIDTH:2 * FOX_WIDTH]
    v = z[:, 2 * FOX_WIDTH:]
    store_kv(k, v)
    kb_ref[0] = k.astype(BF16)
    vb_ref[0] = v.astype(BF16)
    vt_ref[0, 0] = v.T.astype(BF16)
    fft = _dot_nt(wfft_ref[...], h)[:FOX_HEADS]
    lft_ref[0] = _log_sigmoid(fft + fb_ref[...])
    lf_ref[0] = _log_sigmoid(_dot(h, wff_ref[...]) + fbr_ref[...])
    zg = _dot(h, wgla_ref[...])
    qg_ref[0] = zg[:, :GLA_K] * GLA_KEY_DIM ** -0.5
    kg_ref[0] = zg[:, GLA_K:2 * GLA_K]
    vg_ref[0] = zg[:, 2 * GLA_K:]
    lr = _dot(h, wlr_ref[...])
    xa = _dot(lr.astype(BF16), wa2_ref[...]) + ba_ref[...]
    la_ref[0] = _log_sigmoid(xa) / GLA_TAU


def _project(x3, tm, pw, lead_kv=None):
    bx, lx, d = x3.shape
    assert lx % tm == 0
    n = lx // tm
    lead = 0 if lead_kv is None else lead_kv[0].shape[0]
    assert lead % SUBLANES == 0 and lead < tm
    grid = (bx, n + 1 if lead else n)
    last = n - 1
    row = lambda w: pl.BlockSpec((1, tm, w), lambda b, i: (b, jnp.minimum(i, last), 0))
    kv_row = pl.BlockSpec((1, tm, FOX_WIDTH), lambda b, i: (b, i, 0))
    full = lambda a: pl.BlockSpec(a.shape, lambda b, i: (0,) * a.ndim)
    ws = (pw["g_mix"], pw["w_qkv"], pw["w_fft"], pw["w_gla"], pw["w_lr"], pw["w_a2"], pw["b_a"], pw["f_bias"],
          pw["w_ff"], pw["f_bias_row"]) + (tuple(lead_kv) if lead else ())
    out_shape = (
        jax.ShapeDtypeStruct((bx, lx, FOX_WIDTH), BF16),
        jax.ShapeDtypeStruct((bx, lx + lead, FOX_WIDTH), F32),
        jax.ShapeDtypeStruct((bx, lx + lead, FOX_WIDTH), F32),
        jax.ShapeDtypeStruct((bx, lx, FOX_WIDTH), BF16),
        jax.ShapeDtypeStruct((bx, lx, FOX_WIDTH), BF16),
        jax.ShapeDtypeStruct((bx, FOX_HEADS, lx), F32),
        jax.ShapeDtypeStruct((bx, lx, GLA_K), F32),
        jax.ShapeDtypeStruct((bx, lx, GLA_K), F32),
        jax.ShapeDtypeStruct((bx, lx, GLA_V), F32),
        jax.ShapeDtypeStruct((bx, lx, GLA_K), F32),
        jax.ShapeDtypeStruct((bx, lx // tm, FOX_WIDTH, tm), BF16),
        jax.ShapeDtypeStruct((bx, lx, FOX_HEADS), F32),
    )
    out_specs = (row(FOX_WIDTH), kv_row, kv_row, row(FOX_WIDTH), row(FOX_WIDTH),
                 pl.BlockSpec((1, FOX_HEADS, tm), lambda b, i: (b, 0, jnp.minimum(i, last))),
                 row(GLA_K), row(GLA_K), row(GLA_V), row(GLA_K),
                 pl.BlockSpec((1, 1, FOX_WIDTH, tm), lambda b, i: (b, jnp.minimum(i, last), 0, 0)), row(FOX_HEADS))
    return pl.pallas_call(
        functools.partial(_proj_kernel, lead=lead), grid=grid,
        in_specs=[row(d)] + [full(a) for a in ws],
        out_specs=out_specs, out_shape=out_shape,
        scratch_shapes=[pltpu.VMEM((lead, FOX_WIDTH), F32)] * 2 if lead else [],
        compiler_params=_cparams(("parallel", "arbitrary" if lead else "parallel")), name="proj",
    )(x3, *ws)


def _online_update_t(s, s_max, vt, state):
    m, acc = state
    m_new = jnp.maximum(m, s_max)
    p = jnp.exp2(s - m_new)
    return m_new, jnp.exp2(m - m_new) * acc + _dot(vt, p.astype(BF16))


def _with_ones_rows(vt):
    return jnp.concatenate([vt, jnp.ones((2 * SUBLANES, vt.shape[1]), BF16)], axis=0)


def _fox_prompt_kernel(q_ref, kb_ref, vt_ref, lf_ref, km_ref, vtm_ref, lfm_ref, tri_ref, um_ref,
                       o_ref, c_ref, s00_ref, s01_ref, s10_ref, s11_ref, cm00_ref, cm01_ref, cm10_ref, cm11_ref,
                       *, tq):
    qi = pl.program_id(1)
    n_blk = kb_ref.shape[1] // tq
    dh = FOX_HEAD_DIM
    s_refs = ((s00_ref, s01_ref), (s10_ref, s11_ref))
    cm_refs = ((cm00_ref, cm01_ref), (cm10_ref, cm11_ref))

    @pl.when(qi == 0)
    def _():
        carry = jnp.zeros((1, FOX_HEADS), F32)
        for jb in range(n_blk):
            for js in range(tq // LANES):
                off = jb * tq + js * LANES
                cs = _dot_sel_l(tri_ref[...], lf_ref[0, off:off + LANES, :]) + carry
                c_ref[jb, js * LANES:(js + 1) * LANES, :] = cs * LOG2E
                carry = cs[LANES - 1:LANES, :]

    bias_meta = _dot_sel_l(um_ref[...], lfm_ref[...]) * LOG2E
    lane = lax.broadcasted_iota(I32, (tq, LANES), 1)
    key = lax.broadcasted_iota(I32, (tq, tq), 0)
    qry = lax.broadcasted_iota(I32, (tq, tq), 1)
    in_a = lane < dh
    for hp in range(FOX_HEADS // 2):
        sl = slice(hp * LANES, (hp + 1) * LANES)
        ha, hb = 2 * hp, 2 * hp + 1
        qp = q_ref[0, :, sl].astype(F32)
        qa = jnp.where(in_a, qp, 0.0).astype(BF16)
        qb = jnp.where(in_a, 0.0, qp).astype(BF16)
        km = km_ref[:, sl]

        def first(qh, h, km=km):
            s = _dot_nt(km, qh) + bias_meta[:, h:h + 1]
            m = jnp.max(s, axis=0, keepdims=True)
            p = jnp.exp2(s - m)
            return m, _dot(_with_ones_rows(vtm_ref[h * dh:(h + 1) * dh, :]), p.astype(BF16))

        def raw_scores(j, qa=qa, qb=qb, ha=ha, hb=hb, sl=sl):
            start = pl.multiple_of(j * tq, tq)
            kc = kb_ref[0, pl.ds(start, tq), sl]
            cj = c_ref[j]
            return _dot_nt(kc, qa) - cj[:, ha:ha + 1], _dot_nt(kc, qb) - cj[:, hb:hb + 1]

        def values(j, sub, ha=ha, hb=hb):
            h = (ha, hb)[sub]
            return _with_ones_rows(vt_ref[0, j, h * dh:(h + 1) * dh, :])

        def score(j, buf):
            for sub, s in enumerate(raw_scores(j)):
                s_refs[buf][sub][...] = s
                cm_refs[buf][sub][...] = jnp.max(s, axis=0, keepdims=True)

        def advance(j, buf, states):
            return tuple(_online_update_t(s_refs[buf][sub][...], cm_refs[buf][sub][...], values(j, sub), states[sub])
                         for sub in range(2))

        def two_blocks(i, states):
            j = 2 * i
            score(j + 1, 1)
            states = advance(j, 0, states)
            score(jnp.minimum(j + 2, qi - 1), 0)
            return advance(j + 1, 1, states)

        score(0, 0)
        states = []
        for sub, (s, st) in enumerate(zip(raw_scores(qi), (first(qa, ha), first(qb, hb)))):
            s = jnp.where(key <= qry, s, -jnp.inf)
            states.append(_online_update_t(s, jnp.max(s, axis=0, keepdims=True), values(qi, sub), st))
        states = lax.fori_loop(0, qi // 2, two_blocks, tuple(states))
        (_, acc_a), (_, acc_b) = lax.cond(qi % 2 == 1, lambda st: advance(qi - 1, 0, st), lambda st: st, states)
        o_t = jnp.concatenate([acc_a[:dh] / acc_a[dh:dh + 1], acc_b[:dh] / acc_b[dh:dh + 1]], axis=0)
        o_ref[0, :, sl] = o_t.T.astype(BF16)


def _fox_prompt(q, kb, vt, lf, kb_meta, vt_meta, lf_meta, tq):
    b, l, w = q.shape
    assert l % tq == 0 and vt.shape == (b, l // tq, w, tq)
    n_blk = l // tq
    tri = jnp.asarray(np.tril(np.ones((LANES, LANES), np.float32)), BF16)
    um = jnp.asarray(np.triu(np.ones((N_META, N_META), np.float32), 1), BF16)
    full = lambda a: pl.BlockSpec(a.shape, lambda bi, qi: (0,) * a.ndim)
    return pl.pallas_call(
        functools.partial(_fox_prompt_kernel, tq=tq), grid=(b, n_blk),
        in_specs=[pl.BlockSpec((1, tq, w), lambda bi, qi: (bi, qi, 0)),
                  pl.BlockSpec((1, l, w), lambda bi, qi: (bi, 0, 0)),
                  pl.BlockSpec((1, n_blk, w, tq), lambda bi, qi: (bi, 0, 0, 0)),
                  pl.BlockSpec((1, l, FOX_HEADS), lambda bi, qi: (bi, 0, 0)),
                  full(kb_meta), full(vt_meta), full(lf_meta), full(tri), full(um)],
        out_specs=pl.BlockSpec((1, tq, w), lambda bi, qi: (bi, qi, 0)),
        out_shape=jax.ShapeDtypeStruct((b, l, w), BF16),
        scratch_shapes=([pltpu.VMEM((n_blk, tq, FOX_HEADS), F32)] + [pltpu.VMEM((tq, tq), F32)] * 4
                        + [pltpu.VMEM((1, tq), F32)] * 4),
        compiler_params=_cparams(("parallel", "arbitrary")), name="fox_prompt",
    )(q, kb, vt, lf, kb_meta, vt_meta, lf_meta, tri, um)


def _fox_sample_kernel(pt_ref, q_ref, kn_ref, vn_ref, lfn_ref, t8_ref, u_ref, *rest, g, t):
    ck = rest[:g]
    cv = rest[g:2 * g]
    clf = rest[2 * g:3 * g]
    o_ref, qbd_ref, m_ref, l_ref, acc_ref, r_ref = rest[3 * g:]
    gi = pl.program_id(1)
    rows = t * FOX_HEADS
    own = (lax.broadcasted_iota(I32, (rows, FOX_WIDTH), 0) % FOX_HEADS
           == lax.broadcasted_iota(I32, (rows, FOX_WIDTH), 1) // FOX_HEAD_DIM)

    @pl.when(gi == 0)
    def _():
        q = q_ref[0].astype(F32)
        qt = jnp.concatenate([jnp.broadcast_to(q[i:i + 1], (FOX_HEADS, FOX_WIDTH)) for i in range(t)], axis=0)
        qbd = jnp.where(own, qt, 0.0).astype(BF16)
        qbd_ref[...] = qbd
        a_t = _dot_sel_r(lfn_ref[0], t8_ref[...])
        s = _dot_nt(qbd, kn_ref[0]) - jnp.concatenate([a_t * LOG2E] * t, axis=0)
        rt = lax.broadcasted_iota(I32, (rows, t), 0) // FOX_HEADS
        ct = lax.broadcasted_iota(I32, (rows, t), 1)
        s = jnp.where(ct <= rt, s, -jnp.inf)
        m = jnp.max(s, axis=1, keepdims=True)
        p = jnp.exp2(s - m)
        m_ref[...] = m
        l_ref[...] = jnp.sum(p, axis=1, keepdims=True)
        acc_ref[...] = _dot(p.astype(BF16), vn_ref[0])
        r_ref[...] = jnp.zeros_like(r_ref)

    lf_all = jnp.concatenate([clf[jj][0] for jj in range(g)], axis=0)
    after = _dot_sel_r(lf_all, u_ref[...])
    r_run = r_ref[:, 0:1]
    qbd = qbd_ref[...]
    scores = [None] * g
    for jj in reversed(range(g)):
        bias = (after[jj * FOX_HEADS:(jj + 1) * FOX_HEADS] + r_run) * LOG2E
        r_run = r_run + jnp.sum(clf[jj][0], axis=1, keepdims=True)
        kt = ck[jj][0].reshape(FOX_WIDTH, PAGE_SIZE).astype(BF16)
        scores[jj] = _dot(qbd, kt) + jnp.concatenate([bias] * t, axis=0)
    r_ref[...] = jnp.broadcast_to(r_run, r_ref.shape)
    m = m_ref[...]
    m_new = m
    for s in scores:
        m_new = jnp.maximum(m_new, jnp.max(s, axis=1, keepdims=True))
    alpha = jnp.exp2(m - m_new)
    l = alpha * l_ref[...]
    acc = alpha * acc_ref[...]
    for jj, s in enumerate(scores):
        p = jnp.exp2(s - m_new)
        l = l + jnp.sum(p, axis=1, keepdims=True)
        acc = acc + _dot_nt(p.astype(BF16), cv[jj][0].reshape(FOX_WIDTH, PAGE_SIZE).astype(BF16))
    m_ref[...] = m_new
    l_ref[...] = l
    acc_ref[...] = acc

    @pl.when(gi == pl.num_programs(1) - 1)
    def _():
        o = jnp.where(own, acc / l, 0.0)
        o_ref[0] = jnp.concatenate(
            [jnp.sum(o[i * FOX_HEADS:(i + 1) * FOX_HEADS], axis=0, keepdims=True) for i in range(t)],
            axis=0).astype(BF16)


def _fox_sample(q, kb_new, vb_new, lft_new, cache_kt, cache_vt, cache_lft, page_table):
    db, t, w = q.shape
    n_pages = page_table.shape[1]
    g = min(PAGES_PER_STEP, n_pages)
    assert n_pages % g == 0
    ng = n_pages // g
    rows = FOX_HEADS * t
    t8 = jnp.asarray(np.triu(np.ones((t, t), np.float32)), BF16)
    u = jnp.asarray(np.tril(np.ones((PAGE_SIZE, PAGE_SIZE), np.float32), -1), BF16)
    tok = lambda wd: pl.BlockSpec((1, t, wd), lambda b, gi, pt: (b, 0, 0))
    full = lambda a: pl.BlockSpec(a.shape, lambda b, gi, pt: (0,) * a.ndim)

    def page(shape, jj):
        zeros = (0,) * (len(shape) - 1)
        return pl.BlockSpec(shape, lambda b, gi, pt: (pt[b * n_pages + (ng - 1 - gi) * g + jj],) + zeros)

    kv_page = (1, FOX_HEADS, FOX_HEAD_DIM, PAGE_SIZE)
    in_specs = ([tok(w), tok(w), tok(w), pl.BlockSpec((1, FOX_HEADS, t), lambda b, gi, pt: (b, 0, 0)),
                 full(t8), full(u)]
                + [page(kv_page, jj) for jj in range(g)]
                + [page(kv_page, jj) for jj in range(g)]
                + [page((1, FOX_HEADS, PAGE_SIZE), jj) for jj in range(g)])
    grid_spec = pltpu.PrefetchScalarGridSpec(
        num_scalar_prefetch=1, grid=(db, ng), in_specs=in_specs,
        out_specs=pl.BlockSpec((1, t, w), lambda b, gi, pt: (b, 0, 0)),
        scratch_shapes=[pltpu.VMEM((rows, w), BF16), pltpu.VMEM((rows, 1), F32), pltpu.VMEM((rows, 1), F32),
                        pltpu.VMEM((rows, w), F32), pltpu.VMEM((FOX_HEADS, LANES), F32)])
    return pl.pallas_call(
        functools.partial(_fox_sample_kernel, g=g, t=t), grid_spec=grid_spec,
        out_shape=jax.ShapeDtypeStruct((db, t, w), BF16),
        compiler_params=_cparams(("parallel", "arbitrary")), name="fox_sample",
    )(page_table.reshape(-1), q, kb_new, vb_new, lft_new, t8, u,
      *([cache_kt] * g), *([cache_vt] * g), *([cache_lft] * g))


def _gla_tables(c):
    idx = np.arange(c)
    ws = [np.tril(np.ones((c, c), np.float32))]
    ms = [np.eye(c, dtype=np.float32)]
    sz = 2
    while sz <= c:
        seg = idx // sz
        upper = (idx % sz) >= sz // 2
        mid = seg * sz + sz // 2 - 1
        j = idx[None, :]
        r = idx[:, None]
        w_up = (j > mid[:, None]) & (j <= r)
        w_lo = (j > r) & (j <= mid[:, None])
        ws.append(np.where(upper[:, None], w_up, w_lo).astype(np.float32))
        ms.append(((seg[:, None] == seg[None, :]) & upper[:, None] & ~upper[None, :]).astype(np.float32))
        sz *= 2
    return (jnp.asarray(np.concatenate(ws, axis=0), BF16),
            jnp.asarray(np.stack([np.concatenate([m, m], axis=0) for m in ms]), F32))


def _gla_kernel(q_ref, k_ref, v_ref, la_ref, s0_ref, wall_ref, msk_ref, o_ref, sfin_ref, state_ref,
                *, c, n_levels, nb, shared_s0):
    ci = pl.program_id(1)

    @pl.when(ci == 0)
    def _():
        for bi in range(nb):
            state_ref[bi] = s0_ref[0 if shared_s0 else bi]

    lane = lax.broadcasted_iota(I32, (c, LANES), 1)
    r128 = lax.broadcasted_iota(I32, (LANES, LANES), 0)
    half = GLA_KEY_DIM
    in_a = lane < half

    def stack_heads(x):
        return jnp.concatenate([jnp.where(in_a, x, 0.0), jnp.where(in_a, 0.0, x)], axis=0).astype(BF16)

    for bi in range(nb):
        la_hi, la_mid, _ = _split3(la_ref[bi])
        seg = _dot(wall_ref[...], jnp.concatenate([la_hi, la_mid], axis=1))
        seg = seg[:, GLA_K:] + seg[:, :GLA_K]
        b_all = seg[:c]
        e_all = jnp.exp(seg[c:])
        for p in range(GLA_HEADS // 2):
            ksl = slice(p * LANES, (p + 1) * LANES)
            q = q_ref[bi, :, ksl]
            k = k_ref[bi, :, ksl]
            v16 = v_ref[bi, :, 2 * p * LANES:(2 * p + 2) * LANES].astype(BF16)
            sab = msk_ref[0] * _dot_nt(stack_heads(q), k.astype(BF16))
            for lv in range(n_levels):
                e = e_all[lv * c:(lv + 1) * c, ksl]
                sab = sab + msk_ref[lv + 1] * _dot_nt(stack_heads(q * e), (k * e).astype(BF16))
            st = state_ref[bi, p]
            b = b_all[:, ksl]
            inter = _dot(stack_heads(q * jnp.exp(b)), st.astype(BF16))
            s16 = sab.astype(BF16)
            o_ref[bi, :, 2 * p * LANES:(2 * p + 1) * LANES] = inter[:c] + _dot(s16[:c], v16[:, :LANES])
            o_ref[bi, :, (2 * p + 1) * LANES:(2 * p + 2) * LANES] = inter[c:] + _dot(s16[c:], v16[:, LANES:])
            b_end = b[c - 1:c, :]
            kdec = (k * jnp.exp(b_end - b)).astype(BF16)
            upd = _dot_tn(kdec, v16)
            upd = jnp.where(r128 < half, upd[:, :LANES], upd[:, LANES:])
            state_ref[bi, p] = _col_from_row(jnp.exp(b_end), LANES) * st + upd

    @pl.when(ci == pl.num_programs(1) - 1)
    def _():
        sfin_ref[...] = state_ref[...]


def _gla(qg, kg, vg, la, s0, c):
    b, l, _ = qg.shape
    assert l % c == 0
    nb = GLA_SEQS_PER_STEP
    while b % nb:
        nb //= 2
    wall, msk = _gla_tables(c)
    n_levels = msk.shape[0] - 1
    shared_s0 = s0.shape[0] != b
    st_shape = (GLA_HEADS // 2, LANES, GLA_VAL_DIM)
    s0_spec = (pl.BlockSpec((1,) + st_shape, lambda bi, ci: (0, 0, 0, 0)) if shared_s0
               else pl.BlockSpec((nb,) + st_shape, lambda bi, ci: (bi, 0, 0, 0)))
    tok = lambda wd: pl.BlockSpec((nb, c, wd), lambda bi, ci: (bi, ci, 0))
    full = lambda a: pl.BlockSpec(a.shape, lambda bi, ci: (0,) * a.ndim)
    return pl.pallas_call(
        functools.partial(_gla_kernel, c=c, n_levels=n_levels, nb=nb, shared_s0=shared_s0),
        grid=(b // nb, l // c),
        in_specs=[tok(GLA_K), tok(GLA_K), tok(GLA_V), tok(GLA_K), s0_spec, full(wall), full(msk)],
        out_specs=(tok(GLA_V), pl.BlockSpec((nb,) + st_shape, lambda bi, ci: (bi, 0, 0, 0))),
        out_shape=(jax.ShapeDtypeStruct((b, l, GLA_V), F32), jax.ShapeDtypeStruct((b,) + st_shape, F32)),
        scratch_shapes=[pltpu.VMEM((nb,) + st_shape, F32)],
        compiler_params=_cparams(("parallel", "arbitrary")), name="gla",
    )(qg, kg, vg, la, s0, wall, msk)


def _merge_kernel(x_ref, of_ref, og_ref, gmix_ref, wrg_ref, wfo_ref, wgo_ref, wo_ref, gng_ref, gffn_ref,
                  wrt_ref, brt_ref, x2_ref, h2_ref, eid_ref, wsel_ref):
    x = x_ref[...]
    tm, d = x.shape
    h = _rmsnorm(x, gmix_ref[...]).astype(BF16)
    z = _dot(h, wrg_ref[...])
    r = z[:, :GLA_V]
    gate_f = z[:, GLA_V:GLA_V + d]
    gate_g = z[:, GLA_V + d:]
    og = og_ref[...]
    parts = []
    for hh in range(GLA_HEADS):
        blk = og[:, hh * GLA_VAL_DIM:(hh + 1) * GLA_VAL_DIM]
        parts.append(blk * lax.rsqrt(jnp.mean(blk * blk, axis=-1, keepdims=True) + EPS))
    ogn = jnp.concatenate(parts, axis=1)
    ogs = (ogn * gng_ref[...] * (r * _sigmoid(r))).astype(BF16)
    branch_f = _dot(of_ref[...], wfo_ref[...])
    branch_g = _dot(ogs, wgo_ref[...])
    merged = _sigmoid(gate_f) * branch_f + _sigmoid(gate_g) * branch_g
    x2 = x + _dot(merged.astype(BF16), wo_ref[...])
    x2_ref[...] = x2
    h2 = _rmsnorm(x2, gffn_ref[...])
    h2_ref[...] = h2.astype(BF16)

    a = _split3(h2)
    wr = (wrt_ref[0], wrt_ref[1], wrt_ref[2])
    lt = (_dot_nt(wr[1], a[1]) + _dot_nt(wr[0], a[2]) + _dot_nt(wr[2], a[0])
          + _dot_nt(wr[0], a[1]) + _dot_nt(wr[1], a[0]) + _dot_nt(wr[0], a[0])) + brt_ref[...]
    row8 = lax.broadcasted_iota(I32, (SUBLANES, tm), 0)
    gl = jnp.where(row8 < N_GROUPS, lt[:SUBLANES], -jnp.inf)
    gmax = jnp.max(gl, axis=0, keepdims=True)
    g_idx = jnp.min(jnp.where(gl == gmax, row8, SUBLANES), axis=0, keepdims=True)
    g_sel = 1.0 / jnp.sum(jnp.exp(gl - gmax), axis=0, keepdims=True)
    rowe = lax.broadcasted_iota(I32, (N_EXPERTS, tm), 0)
    e1 = jnp.where(rowe // EXPERTS_PER_GROUP == g_idx, lt[SUBLANES:], -jnp.inf)
    v1 = jnp.max(e1, axis=0, keepdims=True)
    i1 = jnp.min(jnp.where(e1 == v1, rowe, N_EXPERTS), axis=0, keepdims=True)
    e2 = jnp.where(rowe == i1, -jnp.inf, e1)
    v2 = jnp.max(e2, axis=0, keepdims=True)
    i2 = jnp.min(jnp.where(e2 == v2, rowe, N_EXPERTS), axis=0, keepdims=True)
    tt = jnp.exp(v2 - v1)
    eid_ref[...] = jnp.concatenate([i1, i2], axis=0)
    wsel_ref[...] = jnp.concatenate([g_sel / (1.0 + tt), g_sel * tt / (1.0 + tt)], axis=0)


def _merge(x, o_fox, o_gla, mw, tm):
    n, d = x.shape
    assert n % tm == 0
    row = lambda wd: pl.BlockSpec((tm, wd), lambda i: (i, 0))
    full = lambda a: pl.BlockSpec(a.shape, lambda i: (0,) * a.ndim, pipeline_mode=pl.Buffered(1))
    ws = (mw["g_mix"], mw["w_rg"], mw["w_fo"], mw["w_go"], mw["w_o"], mw["g_gla"], mw["g_ffn"],
          mw["w_rt"], mw["b_rt"])
    return pl.pallas_call(
        _merge_kernel, grid=(n // tm,),
        in_specs=[row(d), row(FOX_WIDTH), row(GLA_V)] + [full(a) for a in ws],
        out_specs=(row(d), row(d), pl.BlockSpec((2, tm), lambda i: (0, i)), pl.BlockSpec((2, tm), lambda i: (0, i))),
        out_shape=(jax.ShapeDtypeStruct((n, d), F32), jax.ShapeDtypeStruct((n, d), BF16),
                   jax.ShapeDtypeStruct((2, n), I32), jax.ShapeDtypeStruct((2, n), F32)),
        compiler_params=_cparams(("parallel",)), name="merge",
    )(x, o_fox, o_gla, *ws)


def _run_sizes(max_rows):
    sizes = []
    s = RUN_ALIGN
    while s <= max_rows:
        sizes.append(s)
        s *= 2
    return tuple(reversed(sizes))


def _for_each_run_piece(n_rows, a_off, b_off, sizes, fn):
    def pieces(n, a, b, szs):
        for size in szs:
            bit = (n & size) != 0

            @pl.when(bit)
            def _(a=a, b=b, size=size):
                fn(pl.multiple_of(a, RUN_ALIGN), pl.multiple_of(b, RUN_ALIGN), size)

            inc = jnp.where(bit, size, 0)
            a = a + inc
            b = b + inc

    large = [s for s in sizes if s >= RUN_LARGE]
    small = [s for s in sizes if s < RUN_LARGE]
    n_large = n_rows - (n_rows & (RUN_LARGE - 1))
    if large:
        @pl.when(n_large != 0)
        def _():
            pieces(n_large, a_off, b_off, large)
    pieces(n_rows, a_off + n_large, b_off + n_large, small)


def _for_each_listed_piece(tile, cnt_ref, loc_ref, dst_ref, sizes, fn):
    for cls, size in enumerate(sizes):
        first = (tile * len(sizes) + cls) * N_EXPERTS

        def body(k, carry, first=first, size=size):
            fn(pl.multiple_of(loc_ref[first + k], RUN_ALIGN), pl.multiple_of(dst_ref[first + k], RUN_ALIGN), size)
            return carry

        lax.fori_loop(0, cnt_ref[tile * len(sizes) + cls], body, 0)


def _dispatch_kernel(cnt_ref, loc_ref, dst_ref, toff_ref, tlen_ref, nu_ref,
                     h2_ref, eid_ref, wsel_ref, locf_ref, triu_ref,
                     xs_ref, pos_ref, xloc_ref, zero_ref, sems, *, nt, bt, sl, d, n_tiles):
    i = pl.program_id(0)
    slot = i % 2
    sizes = _run_sizes(bt)
    t = zero_ref.shape[0]

    def runs(tile, buf, act):
        _for_each_listed_piece(
            tile, cnt_ref, loc_ref, dst_ref, sizes,
            lambda lo, do, size: act(pltpu.make_async_copy(
                xloc_ref.at[buf, pl.ds(lo, size)], xs_ref.at[pl.ds(do, size)], sems.at[buf])))

    @pl.when(i < nt)
    def _():
        eid = eid_ref[...]
        wsel = wsel_ref[...]
        rowe = lax.broadcasted_iota(I32, (N_EXPERTS, bt), 0)
        oh0 = jnp.where(rowe == eid[0:1], 1.0, 0.0)
        oh1 = jnp.where(rowe == eid[1:2], 1.0, 0.0)
        pref0 = _dot(oh0.astype(BF16), triu_ref[...])
        pref1 = _dot(oh1.astype(BF16), triu_ref[...])
        n0 = jnp.sum(oh0, axis=1, keepdims=True)
        loc = locf_ref[0]
        pos0 = jnp.sum(oh0 * (loc + pref0), axis=0, keepdims=True)
        pos1 = jnp.sum(oh1 * (loc + n0 + pref1), axis=0, keepdims=True)
        pos_ref[...] = jnp.concatenate([pos0, pos1], axis=0)
        srow = lax.broadcasted_iota(I32, (sl, bt), 0).astype(F32)
        m0 = srow == pos0
        m1 = srow == pos1
        perm = jnp.where(m0, 1.0, jnp.where(m1, 1.0, 0.0)).astype(BF16)
        wslot = jnp.sum(jnp.where(m0, wsel[0:1], jnp.where(m1, wsel[1:2], 0.0)), axis=1, keepdims=True)
        xloc_ref[slot, :, :d] = _dot(perm, h2_ref[...])
        xloc_ref[slot, :, d:] = jnp.broadcast_to(wslot, (sl, W_COLS))
        runs(i, slot, lambda cp: cp.start())

    @pl.when(i > 0)
    def _():
        runs(i - 1, 1 - slot, lambda cp: cp.wait())

    @pl.when(i == nt)
    def _():
        zero_ref[...] = jnp.zeros_like(zero_ref)
        tsizes = _run_sizes(t // 2)

        def tail_copy(zo, do, size):
            return pltpu.make_async_copy(zero_ref.at[pl.ds(0, size)], xs_ref.at[pl.ds(do, size)], sems.at[slot])

        def tails(act):
            def body(e, carry):
                _for_each_run_piece(tlen_ref[e], jnp.int32(0), toff_ref[e], tsizes,
                                    lambda zo, do, size: act(tail_copy(zo, do, size)))
                return carry
            lax.fori_loop(0, N_EXPERTS, body, 0)

            def unused(j, carry):
                act(tail_copy(0, pl.multiple_of(j * t, t), t))
                return carry
            lax.fori_loop(nu_ref[0], n_tiles, unused, 0)

        tails(lambda cp: cp.start())
        tails(lambda cp: cp.wait())


def _ffn_kernel(te_ref, nu_ref, xs_ref, wg_ref, wu_ref, wd_ref, y_ref, *, d):
    i = pl.program_id(0)

    @pl.when(i < nu_ref[0])
    def _():
        x = xs_ref[:, :d].astype(BF16)
        g = _dot(x, wg_ref[0])
        u = _dot(x, wu_ref[0])
        a = (g * _sigmoid(g) * u).astype(BF16)
        y = _dot(a, wd_ref[0])
        wrow = xs_ref[:, d:]
        y_ref[...] = y * jnp.concatenate([wrow] * (d // W_COLS), axis=1)

    @pl.when(i >= nu_ref[0])
    def _():
        y_ref[...] = jnp.zeros_like(y_ref)


def _combine_kernel(cnt_ref, loc_ref, dst_ref, pos_ref, x2_ref, gfin_ref, y_hbm, out_ref, yloc_ref, sems,
                    *, nt, bt, sl):
    i = pl.program_id(0)
    slot = i % 2
    sizes = _run_sizes(bt)

    def runs(tile, buf, act):
        _for_each_listed_piece(
            tile, cnt_ref, loc_ref, dst_ref, sizes,
            lambda lo, do, size: act(pltpu.make_async_copy(
                y_hbm.at[pl.ds(do, size)], yloc_ref.at[buf, pl.ds(lo, size)], sems.at[buf])))

    @pl.when(i == 0)
    def _():
        yloc_ref[...] = jnp.zeros_like(yloc_ref)
        runs(0, 0, lambda cp: cp.start())

    @pl.when(i + 1 < nt)
    def _():
        runs(i + 1, 1 - slot, lambda cp: cp.start())

    runs(i, slot, lambda cp: cp.wait())
    pos = pos_ref[...]
    pos0 = _col_from_row(pos[0:1], bt)
    pos1 = _col_from_row(pos[1:2], bt)
    scol = lax.broadcasted_iota(I32, (bt, sl), 1).astype(F32)
    perm_t = jnp.where(scol == pos0, 1.0, jnp.where(scol == pos1, 1.0, 0.0)).astype(BF16)
    moe = _dot(perm_t, yloc_ref[slot].astype(BF16))
    out_ref[...] = _rmsnorm(x2_ref[...] + moe, gfin_ref[...])


def _moe_prompt(x2, h2, eid, wsel, wg, wu, wd, g_fin):
    n, d = x2.shape
    bt, t = MOE_TOKENS, MOE_ROWS
    assert n % bt == 0
    nt = n // bt
    sl = -(-(2 * bt + N_EXPERTS * (RUN_ALIGN - 1)) // LANES) * LANES
    onehot = eid.reshape(2, nt, bt)[..., None] == jnp.arange(N_EXPERTS, dtype=I32)
    cnt = jnp.sum(onehot, axis=(0, 2), dtype=I32)
    p = (cnt + RUN_ALIGN - 1) // RUN_ALIGN * RUN_ALIGN
    loc = jnp.cumsum(p, axis=1) - p
    tot = jnp.sum(p, axis=0)
    reg = (tot + t - 1) // t * t
    reg_end = jnp.cumsum(reg)
    reg_start = reg_end - reg
    dst = reg_start[None, :] + jnp.cumsum(p, axis=0) - p
    n_tiles = (2 * n + nt * N_EXPERTS * (RUN_ALIGN - 1) + N_EXPERTS * (t - 1)) // t + 1
    s_max = n_tiles * t
    n_used = (reg_end[-1] // t).astype(I32).reshape(1)
    tile_start = jnp.arange(n_tiles, dtype=I32) * t
    tile_expert = jnp.minimum(jnp.sum(reg_end[None, :] <= tile_start[:, None], axis=1, dtype=I32), N_EXPERTS - 1)
    sz = jnp.asarray(_run_sizes(bt), I32)[None, :, None]
    p3 = p[:, None, :]
    has = (p3 & sz) != 0
    covered = p3 & ~(2 * sz - 1)
    rank = jnp.cumsum(has, axis=-1, dtype=I32) - 1
    slot_of = has[..., None] & (rank[..., None] == jnp.arange(N_EXPERTS, dtype=I32))
    compact = lambda a: jnp.sum(jnp.where(slot_of, a[..., None], 0), axis=2, dtype=I32).reshape(-1)
    p_f = jnp.sum(has, axis=-1, dtype=I32).reshape(-1)
    loc_f = compact(loc[:, None, :] + covered)
    dst_f = compact(dst[:, None, :].astype(I32) + covered)
    locf = loc.astype(F32).reshape(nt, N_EXPERTS, 1)
    triu = jnp.asarray(np.triu(np.ones((bt, bt), np.float32), 1), BF16)

    last = nt - 1
    xs, pos = pl.pallas_call(
        functools.partial(_dispatch_kernel, nt=nt, bt=bt, sl=sl, d=d, n_tiles=n_tiles),
        grid_spec=pltpu.PrefetchScalarGridSpec(
            num_scalar_prefetch=6, grid=(nt + 1,),
            in_specs=[pl.BlockSpec((bt, d), lambda i, *_: (jnp.minimum(i, last), 0)),
                      pl.BlockSpec((2, bt), lambda i, *_: (0, jnp.minimum(i, last))),
                      pl.BlockSpec((2, bt), lambda i, *_: (0, jnp.minimum(i, last))),
                      pl.BlockSpec((1, N_EXPERTS, 1), lambda i, *_: (jnp.minimum(i, last), 0, 0)),
                      pl.BlockSpec((bt, bt), lambda i, *_: (0, 0))],
            out_specs=(pl.BlockSpec(memory_space=pl.ANY),
                       pl.BlockSpec((2, bt), lambda i, *_: (0, jnp.minimum(i, last)))),
            scratch_shapes=[pltpu.VMEM((2, sl, d + W_COLS), F32), pltpu.VMEM((t, d + W_COLS), F32),
                            pltpu.SemaphoreType.DMA((2,))]),
        out_shape=(jax.ShapeDtypeStruct((s_max, d + W_COLS), F32), jax.ShapeDtypeStruct((2, n), F32)),
        compiler_params=_cparams(("arbitrary",)), name="moe_dispatch",
    )(p_f, loc_f, dst_f, (reg_start + tot).astype(I32), (reg - tot).astype(I32), n_used, h2, eid, wsel, locf, triu)

    de = wg.shape[2]
    used = lambda i, te, nu: jnp.minimum(i, nu[0] - 1)
    y = pl.pallas_call(
        functools.partial(_ffn_kernel, d=d),
        grid_spec=pltpu.PrefetchScalarGridSpec(
            num_scalar_prefetch=2, grid=(n_tiles,),
            in_specs=[pl.BlockSpec((t, d + W_COLS), lambda i, te, nu: (used(i, te, nu), 0)),
                      pl.BlockSpec((1, d, de), lambda i, te, nu: (te[used(i, te, nu)], 0, 0)),
                      pl.BlockSpec((1, d, de), lambda i, te, nu: (te[used(i, te, nu)], 0, 0)),
                      pl.BlockSpec((1, de, d), lambda i, te, nu: (te[used(i, te, nu)], 0, 0))],
            out_specs=pl.BlockSpec((t, d), lambda i, te, nu: (i, 0))),
        out_shape=jax.ShapeDtypeStruct((s_max, d), F32),
        compiler_params=_cparams(("arbitrary",)), name="moe_ffn",
    )(tile_expert, n_used, xs, wg, wu, wd)

    return pl.pallas_call(
        functools.partial(_combine_kernel, nt=nt, bt=bt, sl=sl),
        grid_spec=pltpu.PrefetchScalarGridSpec(
            num_scalar_prefetch=3, grid=(nt,),
            in_specs=[pl.BlockSpec((2, bt), lambda i, *_: (0, i)),
                      pl.BlockSpec((bt, d), lambda i, *_: (i, 0)),
                      pl.BlockSpec((1, d), lambda i, *_: (0, 0)),
                      pl.BlockSpec(memory_space=pl.ANY)],
            out_specs=pl.BlockSpec((bt, d), lambda i, *_: (i, 0)),
            scratch_shapes=[pltpu.VMEM((2, sl, d), F32), pltpu.SemaphoreType.DMA((2,))]),
        out_shape=jax.ShapeDtypeStruct((n, d), F32),
        compiler_params=_cparams(("arbitrary",)), name="moe_combine",
    )(p_f, loc_f, dst_f, pos, x2, g_fin, y)


def _moe_dense_kernel(x2_ref, h2_ref, eid_ref, wsel_ref, gfin_ref, wg_ref, wu_ref, wd_ref, out_ref, acc_ref):
    e = pl.program_id(0)
    n = x2_ref.shape[0]

    @pl.when(e == 0)
    def _():
        acc_ref[...] = jnp.zeros_like(acc_ref)

    eid = eid_ref[...]
    wsel = wsel_ref[...]
    gate_row = jnp.where(eid[0:1] == e, wsel[0:1], 0.0) + jnp.where(eid[1:2] == e, wsel[1:2], 0.0)
    gate = _col_from_row(gate_row, n)
    x = h2_ref[...]
    g = _dot(x, wg_ref[0])
    u = _dot(x, wu_ref[0])
    a = (g * _sigmoid(g) * u).astype(BF16)
    acc_ref[...] += gate * _dot(a, wd_ref[0])

    @pl.when(e == pl.num_programs(0) - 1)
    def _():
        out_ref[...] = _rmsnorm(x2_ref[...] + acc_ref[...], gfin_ref[...])


def _moe_dense(x2, h2, eid, wsel, wg, wu, wd, g_fin):
    n, d = x2.shape
    de = wg.shape[2]
    full = lambda a: pl.BlockSpec(a.shape, lambda e: (0,) * a.ndim)
    return pl.pallas_call(
        _moe_dense_kernel, grid=(wg.shape[0],),
        in_specs=[full(x2), full(h2), full(eid), full(wsel), full(g_fin),
                  pl.BlockSpec((1, d, de), lambda e: (e, 0, 0)),
                  pl.BlockSpec((1, d, de), lambda e: (e, 0, 0)),
                  pl.BlockSpec((1, de, d), lambda e: (e, 0, 0))],
        out_specs=pl.BlockSpec((n, d), lambda e: (0, 0)),
        out_shape=jax.ShapeDtypeStruct((n, d), F32),
        scratch_shapes=[pltpu.VMEM((n, d), F32)],
        compiler_params=_cparams(("arbitrary",)), name="moe_dense",
    )(x2, h2, eid, wsel, g_fin, wg, wu, wd)


def _layer_weights(norm_mix_g, w_in, fox_f_bias, gla_w_a2, gla_b_a, gla_norm_g, w_fox_out, w_gla_out, w_o,
                   norm_ffn_g, w_group_router, b_group_router, w_expert_router, b_expert_router):
    d = w_in.shape[0]
    c = 0
    parts = {}
    for name, width in (("q", FOX_WIDTH), ("k", FOX_WIDTH), ("v", FOX_WIDTH), ("ff", FOX_HEADS),
                        ("qg", GLA_K), ("kg", GLA_K), ("vg", GLA_V), ("lr", GLA_RANK), ("r", GLA_V),
                        ("gf", d), ("gg", d)):
        parts[name] = w_in[:, c:c + width]
        c += width
    assert c == w_in.shape[1]
    g_mix = norm_mix_g.reshape(1, d)
    w_fft = jnp.zeros((2 * SUBLANES, d), F32).at[:FOX_HEADS].set(parts["ff"].T).astype(BF16)
    pw = dict(
        g_mix=g_mix,
        w_qkv=jnp.concatenate([parts["q"], parts["k"], parts["v"]], axis=1).astype(BF16),
        w_fft=w_fft,
        w_gla=jnp.concatenate([parts["qg"], parts["kg"], parts["vg"]], axis=1).astype(BF16),
        w_lr=parts["lr"].astype(BF16),
        w_a2=gla_w_a2.astype(BF16),
        b_a=gla_b_a.reshape(1, GLA_K),
        f_bias=fox_f_bias.reshape(FOX_HEADS, 1),
        w_ff=parts["ff"].astype(BF16),
        f_bias_row=fox_f_bias.reshape(1, FOX_HEADS),
    )
    w_rt = jnp.zeros((SUBLANES + N_EXPERTS, d), F32)
    w_rt = w_rt.at[:N_GROUPS].set(w_group_router.T).at[SUBLANES:].set(w_expert_router.T)
    b_rt = jnp.zeros((SUBLANES + N_EXPERTS, 1), F32)
    b_rt = b_rt.at[:N_GROUPS, 0].set(b_group_router).at[SUBLANES:, 0].set(b_expert_router)
    mw = dict(
        g_mix=g_mix,
        w_rg=jnp.concatenate([parts["r"], parts["gf"], parts["gg"]], axis=1).astype(BF16),
        w_fo=w_fox_out.astype(BF16), w_go=w_gla_out.astype(BF16), w_o=w_o.astype(BF16),
        g_gla=gla_norm_g.reshape(1, GLA_V), g_ffn=norm_ffn_g.reshape(1, d),
        w_rt=jnp.stack(_split3(w_rt)), b_rt=b_rt,
    )
    return pw, mw


def kernel(x_prompt, x_sample, cache_k, cache_v, cache_log_f, state_gla, page_table, meta_tokens, norm_mix_g,
           w_in, fox_f_bias, gla_w_a2, gla_b_a, gla_norm_g, w_fox_out, w_gla_out, w_o, norm_ffn_g,
           w_group_router, b_group_router, w_expert_router, b_expert_router, w_expert_gate, w_expert_up,
           w_expert_down, norm_final_g):
    depth = w_in.shape[0]
    assert depth == 1, "meta rows are only carried through one layer"
    b, seq, d = x_prompt.shape
    db, t_dec, _ = x_sample.shape
    pw, mw = _layer_weights(norm_mix_g[0], w_in[0], fox_f_bias[0], gla_w_a2[0], gla_b_a[0], gla_norm_g[0],
                            w_fox_out[0], w_gla_out[0], w_o[0], norm_ffn_g[0], w_group_router[0],
                            b_group_router[0], w_expert_router[0], b_expert_router[0])
    wg = w_expert_gate[0].astype(BF16)
    wu = w_expert_up[0].astype(BF16)
    wd = w_expert_down[0].astype(BF16)
    g_fin = norm_final_g.reshape(1, d)

    (_, k_m, v_m, kb_m, _, _, qg_m, kg_m, vg_m, la_m, vt_m, lf_m) = _project(
        meta_tokens.reshape(1, N_META, d), N_META, pw)
    zero_state = jnp.zeros((1, GLA_HEADS // 2, LANES, GLA_VAL_DIM), F32)
    _, s_meta = _gla(qg_m, kg_m, vg_m, la_m, zero_state, N_META)

    tile = min(ATTN_TILE, seq)
    q, k_all, v_all, kb, _, _, qg, kg, vg, la, vt, lf = _project(x_prompt, tile, pw, lead_kv=(k_m[0], v_m[0]))
    o_fox = _fox_prompt(q, kb, vt, lf, kb_m[0], vt_m[0, 0], lf_m[0], tile)
    o_gla, s_gla_p = _gla(qg, kg, vg, la, s_meta, min(GLA_CHUNK, seq))
    n = b * seq
    x2, h2, eid, wsel = _merge(x_prompt.reshape(n, d), o_fox.reshape(n, FOX_WIDTH), o_gla.reshape(n, GLA_V),
                               mw, min(MERGE_TILE, n))
    y_prompt = _moe_prompt(x2, h2, eid, wsel, wg, wu, wd, g_fin).reshape(b, seq, d)

    ns = db * t_dec
    qs, ks, vs, kbs, vbs, lfts, qgs, kgs, vgs, las, _, lfs = _project(x_sample.reshape(1, ns, d), ns, pw)
    tok = lambda a: a.reshape(db, t_dec, a.shape[-1])
    lft_s = lfts.reshape(FOX_HEADS, db, t_dec).transpose(1, 0, 2)
    o_fox_s = _fox_sample(tok(qs), tok(kbs), tok(vbs), lft_s, cache_k[0].transpose(0, 2, 3, 1),
                          cache_v[0].transpose(0, 2, 3, 1), cache_log_f[0].transpose(0, 2, 1), page_table)
    s0 = state_gla[0].reshape(db, GLA_HEADS // 2, LANES, GLA_VAL_DIM)
    o_gla_s, s_gla_s = _gla(tok(qgs), tok(kgs), tok(vgs), tok(las), s0, t_dec)
    x2s, h2s, eids, wsels = _merge(x_sample.reshape(ns, d), o_fox_s.reshape(ns, FOX_WIDTH),
                                   o_gla_s.reshape(ns, GLA_V), mw, ns)
    y_sample = _moe_dense(x2s, h2s, eids, wsels, wg, wu, wd, g_fin).reshape(db, t_dec, d)

    heads = lambda a: a.reshape(a.shape[:-1] + (FOX_HEADS, FOX_HEAD_DIM))
    with_meta = lambda m, r: jnp.concatenate([jnp.broadcast_to(m, (b,) + m.shape[1:]), r], axis=1)
    new_k_prompt = heads(k_all)[None]
    new_v_prompt = heads(v_all)[None]
    new_log_f_prompt = with_meta(lf_m, lf)[None]
    new_gla_prompt = s_gla_p.reshape(1, b, GLA_HEADS, GLA_KEY_DIM, GLA_VAL_DIM)
    new_k_sample = heads(tok(ks))[None]
    new_v_sample = heads(tok(vs))[None]
    new_log_f_sample = lfs.reshape(1, db, t_dec, FOX_HEADS)
    new_gla_sample = s_gla_s.reshape(1, db, GLA_HEADS, GLA_KEY_DIM, GLA_VAL_DIM)
    return (y_prompt, y_sample, new_k_prompt, new_v_prompt, new_log_f_prompt, new_gla_prompt,
            new_k_sample, new_v_sample, new_log_f_sample, new_gla_sample)
```

```python
import functools

import numpy as np
import jax
import jax.numpy as jnp
from jax import lax
from jax.experimental import pallas as pl
from jax.experimental.pallas import tpu as pltpu

F32 = jnp.float32
BF16 = jnp.bfloat16
I32 = jnp.int32
EPS = 1e-6
LOG2E = 1.4426950408889634
N_META = 16
FOX_HEADS = 8
FOX_HEAD_DIM = 64
FOX_WIDTH = FOX_HEADS * FOX_HEAD_DIM
GLA_HEADS = 4
GLA_KEY_DIM = 64
GLA_VAL_DIM = 128
GLA_K = GLA_HEADS * GLA_KEY_DIM
GLA_V = GLA_HEADS * GLA_VAL_DIM
GLA_RANK = 16
GLA_TAU = 16.0
GLA_CHUNK = 128
N_GROUPS = 4
EXPERTS_PER_GROUP = 8
N_EXPERTS = N_GROUPS * EXPERTS_PER_GROUP
PAGE_SIZE = 128

LANES = 128
SUBLANES = 8
VMEM_LIMIT = 56 * 1024 * 1024

ATTN_TILE = 512
MERGE_TILE = 1024
MOE_TOKENS = 512
MOE_ROWS = 512
RUN_ALIGN = SUBLANES
RUN_LARGE = 64
PAGES_PER_STEP = 32
GLA_SEQS_PER_STEP = 4
W_COLS = LANES


def _cparams(sem):
    return pltpu.CompilerParams(dimension_semantics=sem, vmem_limit_bytes=VMEM_LIMIT)


def _dot(a, b):
    return jnp.dot(a, b, preferred_element_type=F32)


def _dot_nt(a, b):
    return lax.dot_general(a, b, (((1,), (1,)), ((), ())), preferred_element_type=F32)


def _dot_tn(a, b):
    return lax.dot_general(a, b, (((0,), (0,)), ((), ())), preferred_element_type=F32)


def _split3(a):
    hi = a.astype(BF16)
    r1 = a - hi.astype(F32)
    mid = r1.astype(BF16)
    lo = (r1 - mid.astype(F32)).astype(BF16)
    return hi, mid, lo


def _dot_sel_l(m, a):
    hi, mid, lo = _split3(a)
    return _dot(m, lo) + _dot(m, mid) + _dot(m, hi)


def _dot_sel_r(a, m):
    hi, mid, lo = _split3(a)
    return _dot(lo, m) + _dot(mid, m) + _dot(hi, m)


def _rmsnorm(x, g):
    return x * lax.rsqrt(jnp.mean(x * x, axis=-1, keepdims=True) + EPS) * g


def _log_sigmoid(x):
    return jnp.minimum(x, 0.0) - jnp.log1p(jnp.exp(-jnp.abs(x)))


def _sigmoid(x):
    return 1.0 / (1.0 + jnp.exp(-x))


def _col_from_row(row, n):
    r = lax.broadcasted_iota(I32, (n, n), 0)
    c = lax.broadcasted_iota(I32, (n, n), 1)
    return jnp.sum(jnp.where(r == c, jnp.broadcast_to(row, (n, n)), 0.0), axis=1, keepdims=True)


def _proj_kernel(x_ref, g_ref, wqkv_ref, wfft_ref, wgla_ref, wlr_ref, wa2_ref, ba_ref, fb_ref,
                 wff_ref, fbr_ref, *rest, lead):
    if lead:
        klead_ref, vlead_ref = rest[:2]
        rest = rest[2:]
    q_ref, k_ref, v_ref, kb_ref, vb_ref, lft_ref, qg_ref, kg_ref, vg_ref, la_ref, vt_ref, lf_ref = rest[:12]
    if not lead:
        def plain_store(k, v):
            k_ref[0] = k
            v_ref[0] = v

        _proj_rows(x_ref, g_ref, wqkv_ref, wfft_ref, wgla_ref, wlr_ref, wa2_ref, ba_ref, fb_ref, wff_ref,
                   fbr_ref, q_ref, kb_ref, vb_ref, lft_ref, qg_ref, kg_ref, vg_ref, la_ref, vt_ref, lf_ref,
                   plain_store)
        return

    kcar_ref, vcar_ref = rest[12:]
    i = pl.program_id(1)
    tm = x_ref.shape[1]

    @pl.when(i == 0)
    def _():
        kcar_ref[...] = klead_ref[...]
        vcar_ref[...] = vlead_ref[...]

    def shifted_store(k, v):
        k_ref[0, :lead] = kcar_ref[...]
        v_ref[0, :lead] = vcar_ref[...]
        k_ref[0, lead:] = k[:tm - lead]
        v_ref[0, lead:] = v[:tm - lead]
        kcar_ref[...] = k[tm - lead:]
        vcar_ref[...] = v[tm - lead:]

    @pl.when(i < pl.num_programs(1) - 1)
    def _():
        _proj_rows(x_ref, g_ref, wqkv_ref, wfft_ref, wgla_ref, wlr_ref, wa2_ref, ba_ref, fb_ref, wff_ref,
                   fbr_ref, q_ref, kb_ref, vb_ref, lft_ref, qg_ref, kg_ref, vg_ref, la_ref, vt_ref, lf_ref,
                   shifted_store)

    @pl.when(i == pl.num_programs(1) - 1)
    def _():
        zeros = jnp.zeros((tm - lead, k_ref.shape[2]), F32)
        k_ref[0, :lead] = kcar_ref[...]
        v_ref[0, :lead] = vcar_ref[...]
        k_ref[0, lead:] = zeros
        v_ref[0, lead:] = zeros


def _proj_rows(x_ref, g_ref, wqkv_ref, wfft_ref, wgla_ref, wlr_ref, wa2_ref, ba_ref, fb_ref, wff_ref,
               fbr_ref, q_ref, kb_ref, vb_ref, lft_ref, qg_ref, kg_ref, vg_ref, la_ref, vt_ref, lf_ref, store_kv):
    h = _rmsnorm(x_ref[0], g_ref[...]).astype(BF16)
    z = _dot(h, wqkv_ref[...])
    q_ref[0] = (z[:, :FOX_WIDTH] * (FOX_HEAD_DIM ** -0.5 * LOG2E)).astype(BF16)
    k = z[:, FOX_WIDTH:2 * FOX_WIDTH]
    v = z[:, 2 * FOX_WIDTH:]
    store_kv(k, v)
    kb_ref[0] = k.astype(BF16)
    vb_ref[0] = v.astype(BF16)
    vt_ref[0, 0] = v.T.astype(BF16)
    fft = _dot_nt(wfft_ref[...], h)[:FOX_HEADS]
    lft_ref[0] = _log_sigmoid(fft + fb_ref[...])
    lf_ref[0] = _log_sigmoid(_dot(h, wff_ref[...]) + fbr_ref[...])
    zg = _dot(h, wgla_ref[...])
    qg_ref[0] = zg[:, :GLA_K] * GLA_KEY_DIM ** -0.5
    kg_ref[0] = zg[:, GLA_K:2 * GLA_K]
    vg_ref[0] = zg[:, 2 * GLA_K:]
    lr = _dot(h, wlr_ref[...])
    xa = _dot(lr.astype(BF16), wa2_ref[...]) + ba_ref[...]
    la_ref[0] = _log_sigmoid(xa) / GLA_TAU


def _project(x3, tm, pw, lead_kv=None):
    bx, lx, d = x3.shape
    assert lx % tm == 0
    n = lx // tm
    lead = 0 if lead_kv is None else lead_kv[0].shape[0]
    assert lead % SUBLANES == 0 and lead < tm
    grid = (bx, n + 1 if lead else n)
    last = n - 1
    row = lambda w: pl.BlockSpec((1, tm, w), lambda b, i: (b, jnp.minimum(i, last), 0))
    kv_row = pl.BlockSpec((1, tm, FOX_WIDTH), lambda b, i: (b, i, 0))
    full = lambda a: pl.BlockSpec(a.shape, lambda b, i: (0,) * a.ndim)
    ws = (pw["g_mix"], pw["w_qkv"], pw["w_fft"], pw["w_gla"], pw["w_lr"], pw["w_a2"], pw["b_a"], pw["f_bias"],
          pw["w_ff"], pw["f_bias_row"]) + (tuple(lead_kv) if lead else ())
    out_shape = (
        jax.ShapeDtypeStruct((bx, lx, FOX_WIDTH), BF16),
        jax.ShapeDtypeStruct((bx, lx + lead, FOX_WIDTH), F32),
        jax.ShapeDtypeStruct((bx, lx + lead, FOX_WIDTH), F32),
        jax.ShapeDtypeStruct((bx, lx, FOX_WIDTH), BF16),
        jax.ShapeDtypeStruct((bx, lx, FOX_WIDTH), BF16),
        jax.ShapeDtypeStruct((bx, FOX_HEADS, lx), F32),
        jax.ShapeDtypeStruct((bx, lx, GLA_K), F32),
        jax.ShapeDtypeStruct((bx, lx, GLA_K), F32),
        jax.ShapeDtypeStruct((bx, lx, GLA_V), F32),
        jax.ShapeDtypeStruct((bx, lx, GLA_K), F32),
        jax.ShapeDtypeStruct((bx, lx // tm, FOX_WIDTH, tm), BF16),
        jax.ShapeDtypeStruct((bx, lx, FOX_HEADS), F32),
    )
    out_specs = (row(FOX_WIDTH), kv_row, kv_row, row(FOX_WIDTH), row(FOX_WIDTH),
                 pl.BlockSpec((1, FOX_HEADS, tm), lambda b, i: (b, 0, jnp.minimum(i, last))),
                 row(GLA_K), row(GLA_K), row(GLA_V), row(GLA_K),
                 pl.BlockSpec((1, 1, FOX_WIDTH, tm), lambda b, i: (b, jnp.minimum(i, last), 0, 0)), row(FOX_HEADS))
    return pl.pallas_call(
        functools.partial(_proj_kernel, lead=lead), grid=grid,
        in_specs=[row(d)] + [full(a) for a in ws],
        out_specs=out_specs, out_shape=out_shape,
        scratch_shapes=[pltpu.VMEM((lead, FOX_WIDTH), F32)] * 2 if lead else [],
        compiler_params=_cparams(("parallel", "arbitrary" if lead else "parallel")), name="proj",
    )(x3, *ws)


def _online_update_t(s, s_max, vt, state):
    m, acc = state
    m_new = jnp.maximum(m, s_max)
    p = jnp.exp2(s - m_new)
    return m_new, jnp.exp2(m - m_new) * acc + _dot(vt, p.astype(BF16))


def _with_ones_rows(vt):
    return jnp.concatenate([vt, jnp.ones((2 * SUBLANES, vt.shape[1]), BF16)], axis=0)


def _fox_prompt_kernel(q_ref, kb_ref, vt_ref, lf_ref, km_ref, vtm_ref, lfm_ref, tri_ref, um_ref,
                       o_ref, c_ref, s00_ref, s01_ref, s10_ref, s11_ref, cm00_ref, cm01_ref, cm10_ref, cm11_ref,
                       *, tq):
    qi = pl.program_id(1)
    n_blk = kb_ref.shape[1] // tq
    dh = FOX_HEAD_DIM
    s_refs = ((s00_ref, s01_ref), (s10_ref, s11_ref))
    cm_refs = ((cm00_ref, cm01_ref), (cm10_ref, cm11_ref))

    @pl.when(qi == 0)
    def _():
        carry = jnp.zeros((1, FOX_HEADS), F32)
        for jb in range(n_blk):
            for js in range(tq // LANES):
                off = jb * tq + js * LANES
                cs = _dot_sel_l(tri_ref[...], lf_ref[0, off:off + LANES, :]) + carry
                c_ref[jb, js * LANES:(js + 1) * LANES, :] = cs * LOG2E
                carry = cs[LANES - 1:LANES, :]

    bias_meta = _dot_sel_l(um_ref[...], lfm_ref[...]) * LOG2E
    lane = lax.broadcasted_iota(I32, (tq, LANES), 1)
    key = lax.broadcasted_iota(I32, (tq, tq), 0)
    qry = lax.broadcasted_iota(I32, (tq, tq), 1)
    in_a = lane < dh
    for hp in range(FOX_HEADS // 2):
        sl = slice(hp * LANES, (hp + 1) * LANES)
        ha, hb = 2 * hp, 2 * hp + 1
        qp = q_ref[0, :, sl].astype(F32)
        qa = jnp.where(in_a, qp, 0.0).astype(BF16)
        qb = jnp.where(in_a, 0.0, qp).astype(BF16)
        km = km_ref[:, sl]

        def first(qh, h, km=km):
            s = _dot_nt(km, qh) + bias_meta[:, h:h + 1]
            m = jnp.max(s, axis=0, keepdims=True)
            p = jnp.exp2(s - m)
            return m, _dot(_with_ones_rows(vtm_ref[h * dh:(h + 1) * dh, :]), p.astype(BF16))

        def raw_scores(j, qa=qa, qb=qb, ha=ha, hb=hb, sl=sl):
            start = pl.multiple_of(j * tq, tq)
            kc = kb_ref[0, pl.ds(start, tq), sl]
            cj = c_ref[j]
            return _dot_nt(kc, qa) - cj[:, ha:ha + 1], _dot_nt(kc, qb) - cj[:, hb:hb + 1]

        def values(j, sub, ha=ha, hb=hb):
            h = (ha, hb)[sub]
            return _with_ones_rows(vt_ref[0, j, h * dh:(h + 1) * dh, :])

        def score(j, buf):
            for sub, s in enumerate(raw_scores(j)):
                s_refs[buf][sub][...] = s
                cm_refs[buf][sub][...] = jnp.max(s, axis=0, keepdims=True)

        def advance(j, buf, states):
            return tuple(_online_update_t(s_refs[buf][sub][...], cm_refs[buf][sub][...], values(j, sub), states[sub])
                         for sub in range(2))

        def two_blocks(i, states):
            j = 2 * i
            score(j + 1, 1)
            states = advance(j, 0, states)
            score(jnp.minimum(j + 2, qi - 1), 0)
            return advance(j + 1, 1, states)

        score(0, 0)
        states = []
        for sub, (s, st) in enumerate(zip(raw_scores(qi), (first(qa, ha), first(qb, hb)))):
            s = jnp.where(key <= qry, s, -jnp.inf)
            states.append(_online_update_t(s, jnp.max(s, axis=0, keepdims=True), values(qi, sub), st))
        states = lax.fori_loop(0, qi // 2, two_blocks, tuple(states))
        (_, acc_a), (_, acc_b) = lax.cond(qi % 2 == 1, lambda st: advance(qi - 1, 0, st), lambda st: st, states)
        o_t = jnp.concatenate([acc_a[:dh] / acc_a[dh:dh + 1], acc_b[:dh] / acc_b[dh:dh + 1]], axis=0)
        o_ref[0, :, sl] = o_t.T.astype(BF16)


def _fox_prompt(q, kb, vt, lf, kb_meta, vt_meta, lf_meta, tq):
    b, l, w = q.shape
    assert l % tq == 0 and vt.shape == (b, l // tq, w, tq)
    n_blk = l // tq
    tri = jnp.asarray(np.tril(np.ones((LANES, LANES), np.float32)), BF16)
    um = jnp.asarray(np.triu(np.ones((N_META, N_META), np.float32), 1), BF16)
    full = lambda a: pl.BlockSpec(a.shape, lambda bi, qi: (0,) * a.ndim)
    return pl.pallas_call(
        functools.partial(_fox_prompt_kernel, tq=tq), grid=(b, n_blk),
        in_specs=[pl.BlockSpec((1, tq, w), lambda bi, qi: (bi, qi, 0)),
                  pl.BlockSpec((1, l, w), lambda bi, qi: (bi, 0, 0)),
                  pl.BlockSpec((1, n_blk, w, tq), lambda bi, qi: (bi, 0, 0, 0)),
                  pl.BlockSpec((1, l, FOX_HEADS), lambda bi, qi: (bi, 0, 0)),
                  full(kb_meta), full(vt_meta), full(lf_meta), full(tri), full(um)],
        out_specs=pl.BlockSpec((1, tq, w), lambda bi, qi: (bi, qi, 0)),
        out_shape=jax.ShapeDtypeStruct((b, l, w), BF16),
        scratch_shapes=([pltpu.VMEM((n_blk, tq, FOX_HEADS), F32)] + [pltpu.VMEM((tq, tq), F32)] * 4
                        + [pltpu.VMEM((1, tq), F32)] * 4),
        compiler_params=_cparams(("parallel", "arbitrary")), name="fox_prompt",
    )(q, kb, vt, lf, kb_meta, vt_meta, lf_meta, tri, um)


def _fox_sample_kernel(pt_ref, q_ref, kn_ref, vn_ref, lfn_ref, t8_ref, u_ref, *rest, g, t):
    ck = rest[:g]
    cv = rest[g:2 * g]
    clf = rest[2 * g:3 * g]
    o_ref, qbd_ref, m_ref, l_ref, acc_ref, r_ref = rest[3 * g:]
    gi = pl.program_id(1)
    rows = t * FOX_HEADS
    own = (lax.broadcasted_iota(I32, (rows, FOX_WIDTH), 0) % FOX_HEADS
           == lax.broadcasted_iota(I32, (rows, FOX_WIDTH), 1) // FOX_HEAD_DIM)

    @pl.when(gi == 0)
    def _():
        q = q_ref[0].astype(F32)
        qt = jnp.concatenate([jnp.broadcast_to(q[i:i + 1], (FOX_HEADS, FOX_WIDTH)) for i in range(t)], axis=0)
        qbd = jnp.where(own, qt, 0.0).astype(BF16)
        qbd_ref[...] = qbd
        a_t = _dot_sel_r(lfn_ref[0], t8_ref[...])
        s = _dot_nt(qbd, kn_ref[0]) - jnp.concatenate([a_t * LOG2E] * t, axis=0)
        rt = lax.broadcasted_iota(I32, (rows, t), 0) // FOX_HEADS
        ct = lax.broadcasted_iota(I32, (rows, t), 1)
        s = jnp.where(ct <= rt, s, -jnp.inf)
        m = jnp.max(s, axis=1, keepdims=True)
        p = jnp.exp2(s - m)
        m_ref[...] = m
        l_ref[...] = jnp.sum(p, axis=1, keepdims=True)
        acc_ref[...] = _dot(p.astype(BF16), vn_ref[0])
        r_ref[...] = jnp.zeros_like(r_ref)

    lf_all = jnp.concatenate([clf[jj][0] for jj in range(g)], axis=0)
    after = _dot_sel_r(lf_all, u_ref[...])
    r_run = r_ref[:, 0:1]
    qbd = qbd_ref[...]
    scores = [None] * g
    for jj in reversed(range(g)):
        bias = (after[jj * FOX_HEADS:(jj + 1) * FOX_HEADS] + r_run) * LOG2E
        r_run = r_run + jnp.sum(clf[jj][0], axis=1, keepdims=True)
        kt = ck[jj][0].reshape(FOX_WIDTH, PAGE_SIZE).astype(BF16)
        scores[jj] = _dot(qbd, kt) + jnp.concatenate([bias] * t, axis=0)
    r_ref[...] = jnp.broadcast_to(r_run, r_ref.shape)
    m = m_ref[...]
    m_new = m
    for s in scores:
        m_new = jnp.maximum(m_new, jnp.max(s, axis=1, keepdims=True))
    alpha = jnp.exp2(m - m_new)
    l = alpha * l_ref[...]
    acc = alpha * acc_ref[...]
    for jj, s in enumerate(scores):
        p = jnp.exp2(s - m_new)
        l = l + jnp.sum(p, axis=1, keepdims=True)
        acc = acc + _dot_nt(p.astype(BF16), cv[jj][0].reshape(FOX_WIDTH, PAGE_SIZE).astype(BF16))
    m_ref[...] = m_new
    l_ref[...] = l
    acc_ref[...] = acc

    @pl.when(gi == pl.num_programs(1) - 1)
    def _():
        o = jnp.where(own, acc / l, 0.0)
        o_ref[0] = jnp.concatenate(
            [jnp.sum(o[i * FOX_HEADS:(i + 1) * FOX_HEADS], axis=0, keepdims=True) for i in range(t)],
            axis=0).astype(BF16)


def _fox_sample(q, kb_new, vb_new, lft_new, cache_kt, cache_vt, cache_lft, page_table):
    db, t, w = q.shape
    n_pages = page_table.shape[1]
    g = min(PAGES_PER_STEP, n_pages)
    assert n_pages % g == 0
    ng = n_pages // g
    rows = FOX_HEADS * t
    t8 = jnp.asarray(np.triu(np.ones((t, t), np.float32)), BF16)
    u = jnp.asarray(np.tril(np.ones((PAGE_SIZE, PAGE_SIZE), np.float32), -1), BF16)
    tok = lambda wd: pl.BlockSpec((1, t, wd), lambda b, gi, pt: (b, 0, 0))
    full = lambda a: pl.BlockSpec(a.shape, lambda b, gi, pt: (0,) * a.ndim)

    def page(shape, jj):
        zeros = (0,) * (len(shape) - 1)
        return pl.BlockSpec(shape, lambda b, gi, pt: (pt[b * n_pages + (ng - 1 - gi) * g + jj],) + zeros)

    kv_page = (1, FOX_HEADS, FOX_HEAD_DIM, PAGE_SIZE)
    in_specs = ([tok(w), tok(w), tok(w), pl.BlockSpec((1, FOX_HEADS, t), lambda b, gi, pt: (b, 0, 0)),
                 full(t8), full(u)]
                + [page(kv_page, jj) for jj in range(g)]
                + [page(kv_page, jj) for jj in range(g)]
                + [page((1, FOX_HEADS, PAGE_SIZE), jj) for jj in range(g)])
    grid_spec = pltpu.PrefetchScalarGridSpec(
        num_scalar_prefetch=1, grid=(db, ng), in_specs=in_specs,
        out_specs=pl.BlockSpec((1, t, w), lambda b, gi, pt: (b, 0, 0)),
        scratch_shapes=[pltpu.VMEM((rows, w), BF16), pltpu.VMEM((rows, 1), F32), pltpu.VMEM((rows, 1), F32),
                        pltpu.VMEM((rows, w), F32), pltpu.VMEM((FOX_HEADS, LANES), F32)])
    return pl.pallas_call(
        functools.partial(_fox_sample_kernel, g=g, t=t), grid_spec=grid_spec,
        out_shape=jax.ShapeDtypeStruct((db, t, w), BF16),
        compiler_params=_cparams(("parallel", "arbitrary")), name="fox_sample",
    )(page_table.reshape(-1), q, kb_new, vb_new, lft_new, t8, u,
      *([cache_kt] * g), *([cache_vt] * g), *([cache_lft] * g))


def _gla_tables(c):
    idx = np.arange(c)
    ws = [np.tril(np.ones((c, c), np.float32))]
    ms = [np.eye(c, dtype=np.float32)]
    sz = 2
    while sz <= c:
        seg = idx // sz
        upper = (idx % sz) >= sz // 2
        mid = seg * sz + sz // 2 - 1
        j = idx[None, :]
        r = idx[:, None]
        w_up = (j > mid[:, None]) & (j <= r)
        w_lo = (j > r) & (j <= mid[:, None])
        ws.append(np.where(upper[:, None], w_up, w_lo).astype(np.float32))
        ms.append(((seg[:, None] == seg[None, :]) & upper[:, None] & ~upper[None, :]).astype(np.float32))
        sz *= 2
    return (jnp.asarray(np.concatenate(ws, axis=0), BF16),
            jnp.asarray(np.stack([np.concatenate([m, m], axis=0) for m in ms]), F32))


def _gla_kernel(q_ref, k_ref, v_ref, la_ref, s0_ref, wall_ref, msk_ref, o_ref, sfin_ref, state_ref,
                *, c, n_levels, nb, shared_s0):
    ci = pl.program_id(1)

    @pl.when(ci == 0)
    def _():
        for bi in range(nb):
            state_ref[bi] = s0_ref[0 if shared_s0 else bi]

    lane = lax.broadcasted_iota(I32, (c, LANES), 1)
    r128 = lax.broadcasted_iota(I32, (LANES, LANES), 0)
    half = GLA_KEY_DIM
    in_a = lane < half

    def stack_heads(x):
        return jnp.concatenate([jnp.where(in_a, x, 0.0), jnp.where(in_a, 0.0, x)], axis=0).astype(BF16)

    for bi in range(nb):
        la_hi, la_mid, _ = _split3(la_ref[bi])
        seg = _dot(wall_ref[...], jnp.concatenate([la_hi, la_mid], axis=1))
        seg = seg[:, GLA_K:] + seg[:, :GLA_K]
        b_all = seg[:c]
        e_all = jnp.exp(seg[c:])
        for p in range(GLA_HEADS // 2):
            ksl = slice(p * LANES, (p + 1) * LANES)
            q = q_ref[bi, :, ksl]
            k = k_ref[bi, :, ksl]
            v16 = v_ref[bi, :, 2 * p * LANES:(2 * p + 2) * LANES].astype(BF16)
            sab = msk_ref[0] * _dot_nt(stack_heads(q), k.astype(BF16))
            for lv in range(n_levels):
                e = e_all[lv * c:(lv + 1) * c, ksl]
                sab = sab + msk_ref[lv + 1] * _dot_nt(stack_heads(q * e), (k * e).astype(BF16))
            st = state_ref[bi, p]
            b = b_all[:, ksl]
            inter = _dot(stack_heads(q * jnp.exp(b)), st.astype(BF16))
            s16 = sab.astype(BF16)
            o_ref[bi, :, 2 * p * LANES:(2 * p + 1) * LANES] = inter[:c] + _dot(s16[:c], v16[:, :LANES])
            o_ref[bi, :, (2 * p + 1) * LANES:(2 * p + 2) * LANES] = inter[c:] + _dot(s16[c:], v16[:, LANES:])
            b_end = b[c - 1:c, :]
            kdec = (k * jnp.exp(b_end - b)).astype(BF16)
            upd = _dot_tn(kdec, v16)
            upd = jnp.where(r128 < half, upd[:, :LANES], upd[:, LANES:])
            state_ref[bi, p] = _col_from_row(jnp.exp(b_end), LANES) * st + upd

    @pl.when(ci == pl.num_programs(1) - 1)
    def _():
        sfin_ref[...] = state_ref[...]


def _gla(qg, kg, vg, la, s0, c):
    b, l, _ = qg.shape
    assert l % c == 0
    nb = GLA_SEQS_PER_STEP
    while b % nb:
        nb //= 2
    wall, msk = _gla_tables(c)
    n_levels = msk.shape[0] - 1
    shared_s0 = s0.shape[0] != b
    st_shape = (GLA_HEADS // 2, LANES, GLA_VAL_DIM)
    s0_spec = (pl.BlockSpec((1,) + st_shape, lambda bi, ci: (0, 0, 0, 0)) if shared_s0
               else pl.BlockSpec((nb,) + st_shape, lambda bi, ci: (bi, 0, 0, 0)))
    tok = lambda wd: pl.BlockSpec((nb, c, wd), lambda bi, ci: (bi, ci, 0))
    full = lambda a: pl.BlockSpec(a.shape, lambda bi, ci: (0,) * a.ndim)
    return pl.pallas_call(
        functools.partial(_gla_kernel, c=c, n_levels=n_levels, nb=nb, shared_s0=shared_s0),
        grid=(b // nb, l // c),
        in_specs=[tok(GLA_K), tok(GLA_K), tok(GLA_V), tok(GLA_K), s0_spec, full(wall), full(msk)],
        out_specs=(tok(GLA_V), pl.BlockSpec((nb,) + st_shape, lambda bi, ci: (bi, 0, 0, 0))),
        out_shape=(jax.ShapeDtypeStruct((b, l, GLA_V), F32), jax.ShapeDtypeStruct((b,) + st_shape, F32)),
        scratch_shapes=[pltpu.VMEM((nb,) + st_shape, F32)],
        compiler_params=_cparams(("parallel", "arbitrary")), name="gla",
    )(qg, kg, vg, la, s0, wall, msk)


def _merge_kernel(x_ref, of_ref, og_ref, gmix_ref, wrg_ref, wfo_ref, wgo_ref, wo_ref, gng_ref, gffn_ref,
                  wrt_ref, brt_ref, x2_ref, h2_ref, eid_ref, wsel_ref):
    x = x_ref[...]
    tm, d = x.shape
    h = _rmsnorm(x, gmix_ref[...]).astype(BF16)
    z = _dot(h, wrg_ref[...])
    r = z[:, :GLA_V]
    gate_f = z[:, GLA_V:GLA_V + d]
    gate_g = z[:, GLA_V + d:]
    og = og_ref[...]
    parts = []
    for hh in range(GLA_HEADS):
        blk = og[:, hh * GLA_VAL_DIM:(hh + 1) * GLA_VAL_DIM]
        parts.append(blk * lax.rsqrt(jnp.mean(blk * blk, axis=-1, keepdims=True) + EPS))
    ogn = jnp.concatenate(parts, axis=1)
    ogs = (ogn * gng_ref[...] * (r * _sigmoid(r))).astype(BF16)
    branch_f = _dot(of_ref[...], wfo_ref[...])
    branch_g = _dot(ogs, wgo_ref[...])
    merged = _sigmoid(gate_f) * branch_f + _sigmoid(gate_g) * branch_g
    x2 = x + _dot(merged.astype(BF16), wo_ref[...])
    x2_ref[...] = x2
    h2 = _rmsnorm(x2, gffn_ref[...])
    h2_ref[...] = h2.astype(BF16)

    a = _split3(h2)
    wr = (wrt_ref[0], wrt_ref[1], wrt_ref[2])
    lt = (_dot_nt(wr[1], a[1]) + _dot_nt(wr[0], a[2]) + _dot_nt(wr[2], a[0])
          + _dot_nt(wr[0], a[1]) + _dot_nt(wr[1], a[0]) + _dot_nt(wr[0], a[0])) + brt_ref[...]
    row8 = lax.broadcasted_iota(I32, (SUBLANES, tm), 0)
    gl = jnp.where(row8 < N_GROUPS, lt[:SUBLANES], -jnp.inf)
    gmax = jnp.max(gl, axis=0, keepdims=True)
    g_idx = jnp.min(jnp.where(gl == gmax, row8, SUBLANES), axis=0, keepdims=True)
    g_sel = 1.0 / jnp.sum(jnp.exp(gl - gmax), axis=0, keepdims=True)
    rowe = lax.broadcasted_iota(I32, (N_EXPERTS, tm), 0)
    e1 = jnp.where(rowe // EXPERTS_PER_GROUP == g_idx, lt[SUBLANES:], -jnp.inf)
    v1 = jnp.max(e1, axis=0, keepdims=True)
    i1 = jnp.min(jnp.where(e1 == v1, rowe, N_EXPERTS), axis=0, keepdims=True)
    e2 = jnp.where(rowe == i1, -jnp.inf, e1)
    v2 = jnp.max(e2, axis=0, keepdims=True)
    i2 = jnp.min(jnp.where(e2 == v2, rowe, N_EXPERTS), axis=0, keepdims=True)
    tt = jnp.exp(v2 - v1)
    eid_ref[...] = jnp.concatenate([i1, i2], axis=0)
    wsel_ref[...] = jnp.concatenate([g_sel / (1.0 + tt), g_sel * tt / (1.0 + tt)], axis=0)


def _merge(x, o_fox, o_gla, mw, tm):
    n, d = x.shape
    assert n % tm == 0
    row = lambda wd: pl.BlockSpec((tm, wd), lambda i: (i, 0))
    full = lambda a: pl.BlockSpec(a.shape, lambda i: (0,) * a.ndim, pipeline_mode=pl.Buffered(1))
    ws = (mw["g_mix"], mw["w_rg"], mw["w_fo"], mw["w_go"], mw["w_o"], mw["g_gla"], mw["g_ffn"],
          mw["w_rt"], mw["b_rt"])
    return pl.pallas_call(
        _merge_kernel, grid=(n // tm,),
        in_specs=[row(d), row(FOX_WIDTH), row(GLA_V)] + [full(a) for a in ws],
        out_specs=(row(d), row(d), pl.BlockSpec((2, tm), lambda i: (0, i)), pl.BlockSpec((2, tm), lambda i: (0, i))),
        out_shape=(jax.ShapeDtypeStruct((n, d), F32), jax.ShapeDtypeStruct((n, d), BF16),
                   jax.ShapeDtypeStruct((2, n), I32), jax.ShapeDtypeStruct((2, n), F32)),
        compiler_params=_cparams(("parallel",)), name="merge",
    )(x, o_fox, o_gla, *ws)


def _run_sizes(max_rows):
    sizes = []
    s = RUN_ALIGN
    while s <= max_rows:
        sizes.append(s)
        s *= 2
    return tuple(reversed(sizes))


def _for_each_run_piece(n_rows, a_off, b_off, sizes, fn):
    def pieces(n, a, b, szs):
        for size in szs:
            bit = (n & size) != 0

            @pl.when(bit)
            def _(a=a, b=b, size=size):
                fn(pl.multiple_of(a, RUN_ALIGN), pl.multiple_of(b, RUN_ALIGN), size)

            inc = jnp.where(bit, size, 0)
            a = a + inc
            b = b + inc

    large = [s for s in sizes if s >= RUN_LARGE]
    small = [s for s in sizes if s < RUN_LARGE]
    n_large = n_rows - (n_rows & (RUN_LARGE - 1))
    if large:
        @pl.when(n_large != 0)
        def _():
            pieces(n_large, a_off, b_off, large)
    pieces(n_rows, a_off + n_large, b_off + n_large, small)


def _for_each_listed_piece(tile, cnt_ref, loc_ref, dst_ref, sizes, fn):
    for cls, size in enumerate(sizes):
        first = (tile * len(sizes) + cls) * N_EXPERTS

        def body(k, carry, first=first, size=size):
            fn(pl.multiple_of(loc_ref[first + k], RUN_ALIGN), pl.multiple_of(dst_ref[first + k], RUN_ALIGN), size)
            return carry

        lax.fori_loop(0, cnt_ref[tile * len(sizes) + cls], body, 0)


def _dispatch_kernel(cnt_ref, loc_ref, dst_ref, toff_ref, tlen_ref, nu_ref,
                     h2_ref, eid_ref, wsel_ref, locf_ref, triu_ref,
                     xs_ref, pos_ref, xloc_ref, zero_ref, sems, *, nt, bt, sl, d, n_tiles):
    i = pl.program_id(0)
    slot = i % 2
    sizes = _run_sizes(bt)
    t = zero_ref.shape[0]

    def runs(tile, buf, act):
        _for_each_listed_piece(
            tile, cnt_ref, loc_ref, dst_ref, sizes,
            lambda lo, do, size: act(pltpu.make_async_copy(
                xloc_ref.at[buf, pl.ds(lo, size)], xs_ref.at[pl.ds(do, size)], sems.at[buf])))

    @pl.when(i < nt)
    def _():
        eid = eid_ref[...]
        wsel = wsel_ref[...]
        rowe = lax.broadcasted_iota(I32, (N_EXPERTS, bt), 0)
        oh0 = jnp.where(rowe == eid[0:1], 1.0, 0.0)
        oh1 = jnp.where(rowe == eid[1:2], 1.0, 0.0)
        pref0 = _dot(oh0.astype(BF16), triu_ref[...])
        pref1 = _dot(oh1.astype(BF16), triu_ref[...])
        n0 = jnp.sum(oh0, axis=1, keepdims=True)
        loc = locf_ref[0]
        pos0 = jnp.sum(oh0 * (loc + pref0), axis=0, keepdims=True)
        pos1 = jnp.sum(oh1 * (loc + n0 + pref1), axis=0, keepdims=True)
        pos_ref[...] = jnp.concatenate([pos0, pos1], axis=0)
        srow = lax.broadcasted_iota(I32, (sl, bt), 0).astype(F32)
        m0 = srow == pos0
        m1 = srow == pos1
        perm = jnp.where(m0, 1.0, jnp.where(m1, 1.0, 0.0)).astype(BF16)
        wslot = jnp.sum(jnp.where(m0, wsel[0:1], jnp.where(m1, wsel[1:2], 0.0)), axis=1, keepdims=True)
        xloc_ref[slot, :, :d] = _dot(perm, h2_ref[...])
        xloc_ref[slot, :, d:] = jnp.broadcast_to(wslot, (sl, W_COLS))
        runs(i, slot, lambda cp: cp.start())

    @pl.when(i > 0)
    def _():
        runs(i - 1, 1 - slot, lambda cp: cp.wait())

    @pl.when(i == nt)
    def _():
        zero_ref[...] = jnp.zeros_like(zero_ref)
        tsizes = _run_sizes(t // 2)

        def tail_copy(zo, do, size):
            return pltpu.make_async_copy(zero_ref.at[pl.ds(0, size)], xs_ref.at[pl.ds(do, size)], sems.at[slot])

        def tails(act):
            def body(e, carry):
                _for_each_run_piece(tlen_ref[e], jnp.int32(0), toff_ref[e], tsizes,
                                    lambda zo, do, size: act(tail_copy(zo, do, size)))
                return carry
            lax.fori_loop(0, N_EXPERTS, body, 0)

            def unused(j, carry):
                act(tail_copy(0, pl.multiple_of(j * t, t), t))
                return carry
            lax.fori_loop(nu_ref[0], n_tiles, unused, 0)

        tails(lambda cp: cp.start())
        tails(lambda cp: cp.wait())


def _ffn_kernel(te_ref, nu_ref, xs_ref, wg_ref, wu_ref, wd_ref, y_ref, *, d):
    i = pl.program_id(0)

    @pl.when(i < nu_ref[0])
    def _():
        x = xs_ref[:, :d].astype(BF16)
        g = _dot(x, wg_ref[0])
        u = _dot(x, wu_ref[0])
        a = (g * _sigmoid(g) * u).astype(BF16)
        y = _dot(a, wd_ref[0])
        wrow = xs_ref[:, d:]
        y_ref[...] = y * jnp.concatenate([wrow] * (d // W_COLS), axis=1)

    @pl.when(i >= nu_ref[0])
    def _():
        y_ref[...] = jnp.zeros_like(y_ref)


def _combine_kernel(cnt_ref, loc_ref, dst_ref, pos_ref, x2_ref, gfin_ref, y_hbm, out_ref, yloc_ref, sems,
                    *, nt, bt, sl):
    i = pl.program_id(0)
    slot = i % 2
    sizes = _run_sizes(bt)

    def runs(tile, buf, act):
        _for_each_listed_piece(
            tile, cnt_ref, loc_ref, dst_ref, sizes,
            lambda lo, do, size: act(pltpu.make_async_copy(
                y_hbm.at[pl.ds(do, size)], yloc_ref.at[buf, pl.ds(lo, size)], sems.at[buf])))

    @pl.when(i == 0)
    def _():
        yloc_ref[...] = jnp.zeros_like(yloc_ref)
        runs(0, 0, lambda cp: cp.start())

    @pl.when(i + 1 < nt)
    def _():
        runs(i + 1, 1 - slot, lambda cp: cp.start())

    runs(i, slot, lambda cp: cp.wait())
    pos = pos_ref[...]
    pos0 = _col_from_row(pos[0:1], bt)
    pos1 = _col_from_row(pos[1:2], bt)
    scol = lax.broadcasted_iota(I32, (bt, sl), 1).astype(F32)
    perm_t = jnp.where(scol == pos0, 1.0, jnp.where(scol == pos1, 1.0, 0.0)).astype(BF16)
    moe = _dot(perm_t, yloc_ref[slot].astype(BF16))
    out_ref[...] = _rmsnorm(x2_ref[...] + moe, gfin_ref[...])


def _moe_prompt(x2, h2, eid, wsel, wg, wu, wd, g_fin):
    n, d = x2.shape
    bt, t = MOE_TOKENS, MOE_ROWS
    assert n % bt == 0
    nt = n // bt
    sl = -(-(2 * bt + N_EXPERTS * (RUN_ALIGN - 1)) // LANES) * LANES
    onehot = eid.reshape(2, nt, bt)[..., None] == jnp.arange(N_EXPERTS, dtype=I32)
    cnt = jnp.sum(onehot, axis=(0, 2), dtype=I32)
    p = (cnt + RUN_ALIGN - 1) // RUN_ALIGN * RUN_ALIGN
    loc = jnp.cumsum(p, axis=1) - p
    tot = jnp.sum(p, axis=0)
    reg = (tot + t - 1) // t * t
    reg_end = jnp.cumsum(reg)
    reg_start = reg_end - reg
    dst = reg_start[None, :] + jnp.cumsum(p, axis=0) - p
    n_tiles = (2 * n + nt * N_EXPERTS * (RUN_ALIGN - 1) + N_EXPERTS * (t - 1)) // t + 1
    s_max = n_tiles * t
    n_used = (reg_end[-1] // t).astype(I32).reshape(1)
    tile_start = jnp.arange(n_tiles, dtype=I32) * t
    tile_expert = jnp.minimum(jnp.sum(reg_end[None, :] <= tile_start[:, None], axis=1, dtype=I32), N_EXPERTS - 1)
    sz = jnp.asarray(_run_sizes(bt), I32)[None, :, None]
    p3 = p[:, None, :]
    has = (p3 & sz) != 0
    covered = p3 & ~(2 * sz - 1)
    rank = jnp.cumsum(has, axis=-1, dtype=I32) - 1
    slot_of = has[..., None] & (rank[..., None] == jnp.arange(N_EXPERTS, dtype=I32))
    compact = lambda a: jnp.sum(jnp.where(slot_of, a[..., None], 0), axis=2, dtype=I32).reshape(-1)
    p_f = jnp.sum(has, axis=-1, dtype=I32).reshape(-1)
    loc_f = compact(loc[:, None, :] + covered)
    dst_f = compact(dst[:, None, :].astype(I32) + covered)
    locf = loc.astype(F32).reshape(nt, N_EXPERTS, 1)
    triu = jnp.asarray(np.triu(np.ones((bt, bt), np.float32), 1), BF16)

    last = nt - 1
    xs, pos = pl.pallas_call(
        functools.partial(_dispatch_kernel, nt=nt, bt=bt, sl=sl, d=d, n_tiles=n_tiles),
        grid_spec=pltpu.PrefetchScalarGridSpec(
            num_scalar_prefetch=6, grid=(nt + 1,),
            in_specs=[pl.BlockSpec((bt, d), lambda i, *_: (jnp.minimum(i, last), 0)),
                      pl.BlockSpec((2, bt), lambda i, *_: (0, jnp.minimum(i, last))),
                      pl.BlockSpec((2, bt), lambda i, *_: (0, jnp.minimum(i, last))),
                      pl.BlockSpec((1, N_EXPERTS, 1), lambda i, *_: (jnp.minimum(i, last), 0, 0)),
                      pl.BlockSpec((bt, bt), lambda i, *_: (0, 0))],
            out_specs=(pl.BlockSpec(memory_space=pl.ANY),
                       pl.BlockSpec((2, bt), lambda i, *_: (0, jnp.minimum(i, last)))),
            scratch_shapes=[pltpu.VMEM((2, sl, d + W_COLS), F32), pltpu.VMEM((t, d + W_COLS), F32),
                            pltpu.SemaphoreType.DMA((2,))]),
        out_shape=(jax.ShapeDtypeStruct((s_max, d + W_COLS), F32), jax.ShapeDtypeStruct((2, n), F32)),
        compiler_params=_cparams(("arbitrary",)), name="moe_dispatch",
    )(p_f, loc_f, dst_f, (reg_start + tot).astype(I32), (reg - tot).astype(I32), n_used, h2, eid, wsel, locf, triu)

    de = wg.shape[2]
    used = lambda i, te, nu: jnp.minimum(i, nu[0] - 1)
    y = pl.pallas_call(
        functools.partial(_ffn_kernel, d=d),
        grid_spec=pltpu.PrefetchScalarGridSpec(
            num_scalar_prefetch=2, grid=(n_tiles,),
            in_specs=[pl.BlockSpec((t, d + W_COLS), lambda i, te, nu: (used(i, te, nu), 0)),
                      pl.BlockSpec((1, d, de), lambda i, te, nu: (te[used(i, te, nu)], 0, 0)),
                      pl.BlockSpec((1, d, de), lambda i, te, nu: (te[used(i, te, nu)], 0, 0)),
                      pl.BlockSpec((1, de, d), lambda i, te, nu: (te[used(i, te, nu)], 0, 0))],
            out_specs=pl.BlockSpec((t, d), lambda i, te, nu: (i, 0))),
        out_shape=jax.ShapeDtypeStruct((s_max, d), F32),
        compiler_params=_cparams(("arbitrary",)), name="moe_ffn",
    )(tile_expert, n_used, xs, wg, wu, wd)

    return pl.pallas_call(
        functools.partial(_combine_kernel, nt=nt, bt=bt, sl=sl),
        grid_spec=pltpu.PrefetchScalarGridSpec(
            num_scalar_prefetch=3, grid=(nt,),
            in_specs=[pl.BlockSpec((2, bt), lambda i, *_: (0, i)),
                      pl.BlockSpec((bt, d), lambda i, *_: (i, 0)),
                      pl.BlockSpec((1, d), lambda i, *_: (0, 0)),
                      pl.BlockSpec(memory_space=pl.ANY)],
            out_specs=pl.BlockSpec((bt, d), lambda i, *_: (i, 0)),
            scratch_shapes=[pltpu.VMEM((2, sl, d), F32), pltpu.SemaphoreType.DMA((2,))]),
        out_shape=jax.ShapeDtypeStruct((n, d), F32),
        compiler_params=_cparams(("arbitrary",)), name="moe_combine",
    )(p_f, loc_f, dst_f, pos, x2, g_fin, y)


def _moe_dense_kernel(x2_ref, h2_ref, eid_ref, wsel_ref, gfin_ref, wg_ref, wu_ref, wd_ref, out_ref, acc_ref):
    e = pl.program_id(0)
    n = x2_ref.shape[0]

    @pl.when(e == 0)
    def _():
        acc_ref[...] = jnp.zeros_like(acc_ref)

    eid = eid_ref[...]
    wsel = wsel_ref[...]
    gate_row = jnp.where(eid[0:1] == e, wsel[0:1], 0.0) + jnp.where(eid[1:2] == e, wsel[1:2], 0.0)
    gate = _col_from_row(gate_row, n)
    x = h2_ref[...]
    g = _dot(x, wg_ref[0])
    u = _dot(x, wu_ref[0])
    a = (g * _sigmoid(g) * u).astype(BF16)
    acc_ref[...] += gate * _dot(a, wd_ref[0])

    @pl.when(e == pl.num_programs(0) - 1)
    def _():
        out_ref[...] = _rmsnorm(x2_ref[...] + acc_ref[...], gfin_ref[...])


def _moe_dense(x2, h2, eid, wsel, wg, wu, wd, g_fin):
    n, d = x2.shape
    de = wg.shape[2]
    full = lambda a: pl.BlockSpec(a.shape, lambda e: (0,) * a.ndim)
    return pl.pallas_call(
        _moe_dense_kernel, grid=(wg.shape[0],),
        in_specs=[full(x2), full(h2), full(eid), full(wsel), full(g_fin),
                  pl.BlockSpec((1, d, de), lambda e: (e, 0, 0)),
                  pl.BlockSpec((1, d, de), lambda e: (e, 0, 0)),
                  pl.BlockSpec((1, de, d), lambda e: (e, 0, 0))],
        out_specs=pl.BlockSpec((n, d), lambda e: (0, 0)),
        out_shape=jax.ShapeDtypeStruct((n, d), F32),
        scratch_shapes=[pltpu.VMEM((n, d), F32)],
        compiler_params=_cparams(("arbitrary",)), name="moe_dense",
    )(x2, h2, eid, wsel, g_fin, wg, wu, wd)


def _layer_weights(norm_mix_g, w_in, fox_f_bias, gla_w_a2, gla_b_a, gla_norm_g, w_fox_out, w_gla_out, w_o,
                   norm_ffn_g, w_group_router, b_group_router, w_expert_router, b_expert_router):
    d = w_in.shape[0]
    c = 0
    parts = {}
    for name, width in (("q", FOX_WIDTH), ("k", FOX_WIDTH), ("v", FOX_WIDTH), ("ff", FOX_HEADS),
                        ("qg", GLA_K), ("kg", GLA_K), ("vg", GLA_V), ("lr", GLA_RANK), ("r", GLA_V),
                        ("gf", d), ("gg", d)):
        parts[name] = w_in[:, c:c + width]
        c += width
    assert c == w_in.shape[1]
    g_mix = norm_mix_g.reshape(1, d)
    w_fft = jnp.zeros((2 * SUBLANES, d), F32).at[:FOX_HEADS].set(parts["ff"].T).astype(BF16)
    pw = dict(
        g_mix=g_mix,
        w_qkv=jnp.concatenate([parts["q"], parts["k"], parts["v"]], axis=1).astype(BF16),
        w_fft=w_fft,
        w_gla=jnp.concatenate([parts["qg"], parts["kg"], parts["vg"]], axis=1).astype(BF16),
        w_lr=parts["lr"].astype(BF16),
        w_a2=gla_w_a2.astype(BF16),
        b_a=gla_b_a.reshape(1, GLA_K),
        f_bias=fox_f_bias.reshape(FOX_HEADS, 1),
        w_ff=parts["ff"].astype(BF16),
        f_bias_row=fox_f_bias.reshape(1, FOX_HEADS),
    )
    w_rt = jnp.zeros((SUBLANES + N_EXPERTS, d), F32)
    w_rt = w_rt.at[:N_GROUPS].set(w_group_router.T).at[SUBLANES:].set(w_expert_router.T)
    b_rt = jnp.zeros((SUBLANES + N_EXPERTS, 1), F32)
    b_rt = b_rt.at[:N_GROUPS, 0].set(b_group_router).at[SUBLANES:, 0].set(b_expert_router)
    mw = dict(
        g_mix=g_mix,
        w_rg=jnp.concatenate([parts["r"], parts["gf"], parts["gg"]], axis=1).astype(BF16),
        w_fo=w_fox_out.astype(BF16), w_go=w_gla_out.astype(BF16), w_o=w_o.astype(BF16),
        g_gla=gla_norm_g.reshape(1, GLA_V), g_ffn=norm_ffn_g.reshape(1, d),
        w_rt=jnp.stack(_split3(w_rt)), b_rt=b_rt,
    )
    return pw, mw


def kernel(x_prompt, x_sample, cache_k, cache_v, cache_log_f, state_gla, page_table, meta_tokens, norm_mix_g,
           w_in, fox_f_bias, gla_w_a2, gla_b_a, gla_norm_g, w_fox_out, w_gla_out, w_o, norm_ffn_g,
           w_group_router, b_group_router, w_expert_router, b_expert_router, w_expert_gate, w_expert_up,
           w_expert_down, norm_final_g):
    depth = w_in.shape[0]
    assert depth == 1, "meta rows are only carried through one layer"
    b, seq, d = x_prompt.shape
    db, t_dec, _ = x_sample.shape
    pw, mw = _layer_weights(norm_mix_g[0], w_in[0], fox_f_bias[0], gla_w_a2[0], gla_b_a[0], gla_norm_g[0],
                            w_fox_out[0], w_gla_out[0], w_o[0], norm_ffn_g[0], w_group_router[0],
                            b_group_router[0], w_expert_router[0], b_expert_router[0])
    wg = w_expert_gate[0].astype(BF16)
    wu = w_expert_up[0].astype(BF16)
    wd = w_expert_down[0].astype(BF16)
    g_fin = norm_final_g.reshape(1, d)

    (_, k_m, v_m, kb_m, _, _, qg_m, kg_m, vg_m, la_m, vt_m, lf_m) = _project(
        meta_tokens.reshape(1, N_META, d), N_META, pw)
    zero_state = jnp.zeros((1, GLA_HEADS // 2, LANES, GLA_VAL_DIM), F32)
    _, s_meta = _gla(qg_m, kg_m, vg_m, la_m, zero_state, N_META)

    tile = min(ATTN_TILE, seq)
    q, k_all, v_all, kb, _, _, qg, kg, vg, la, vt, lf = _project(x_prompt, tile, pw, lead_kv=(k_m[0], v_m[0]))
    o_fox = _fox_prompt(q, kb, vt, lf, kb_m[0], vt_m[0, 0], lf_m[0], tile)
    o_gla, s_gla_p = _gla(qg, kg, vg, la, s_meta, min(GLA_CHUNK, seq))
    n = b * seq
    x2, h2, eid, wsel = _merge(x_prompt.reshape(n, d), o_fox.reshape(n, FOX_WIDTH), o_gla.reshape(n, GLA_V),
                               mw, min(MERGE_TILE, n))
    y_prompt = _moe_prompt(x2, h2, eid, wsel, wg, wu, wd, g_fin).reshape(b, seq, d)

    ns = db * t_dec
    qs, ks, vs, kbs, vbs, lfts, qgs, kgs, vgs, las, _, lfs = _project(x_sample.reshape(1, ns, d), ns, pw)
    tok = lambda a: a.reshape(db, t_dec, a.shape[-1])
    lft_s = lfts.reshape(FOX_HEADS, db, t_dec).transpose(1, 0, 2)
    o_fox_s = _fox_sample(tok(qs), tok(kbs), tok(vbs), lft_s, cache_k[0].transpose(0, 2, 3, 1),
                          cache_v[0].transpose(0, 2, 3, 1), cache_log_f[0].transpose(0, 2, 1), page_table)
    s0 = state_gla[0].reshape(db, GLA_HEADS // 2, LANES, GLA_VAL_DIM)
    o_gla_s, s_gla_s = _gla(tok(qgs), tok(kgs), tok(vgs), tok(las), s0, t_dec)
    x2s, h2s, eids, wsels = _merge(x_sample.reshape(ns, d), o_fox_s.reshape(ns, FOX_WIDTH),
                                   o_gla_s.reshape(ns, GLA_V), mw, ns)
    y_sample = _moe_dense(x2s, h2s, eids, wsels, wg, wu, wd, g_fin).reshape(db, t_dec, d)

    heads = lambda a: a.reshape(a.shape[:-1] + (FOX_HEADS, FOX_HEAD_DIM))
    with_meta = lambda m, r: jnp.concatenate([jnp.broadcast_to(m, (b,) + m.shape[1:]), r], axis=1)
    new_k_prompt = heads(k_all)[None]
    new_v_prompt = heads(v_all)[None]
    new_log_f_prompt = with_meta(lf_m, lf)[None]
    new_gla_prompt = s_gla_p.reshape(1, b, GLA_HEADS, GLA_KEY_DIM, GLA_VAL_DIM)
    new_k_sample = heads(tok(ks))[None]
    new_v_sample = heads(tok(vs))[None]
    new_log_f_sample = lfs.reshape(1, db, t_dec, FOX_HEADS)
    new_gla_sample = s_gla_s.reshape(1, db, GLA_HEADS, GLA_KEY_DIM, GLA_VAL_DIM)
    return (y_prompt, y_sample, new_k_prompt, new_v_prompt, new_log_f_prompt, new_gla_prompt,
            new_k_sample, new_v_sample, new_log_f_sample, new_gla_sample)
```
